```python
import math
import jax, jax.numpy as jnp
from jax import lax
import numpy as np

D_MODEL = 1024
BATCH = 4
SEQ = 8192
DEPTH = 1

D_MIX = D_MODEL
D_POOL = D_MIX // 2
D_CONV = D_MIX - D_POOL
POOL_WINDOWS = (2, 4, 8, 16)
N_POOL_GROUPS = len(POOL_WINDOWS)
POOL_GROUP_W = D_POOL // N_POOL_GROUPS
CONV_WIDTH = 31
N_CONV_GROUPS = 8
LN_EPS = 1e-5
DEEPNORM_ALPHA = (2.0 * DEPTH) ** 0.25
DEEPNORM_BETA = (8.0 * DEPTH) ** -0.25
IN_SPLITS = (D_POOL, D_POOL, 2 * D_CONV, D_CONV)
D_IN = sum(IN_SPLITS)

kernel_name = "hybrid_pool_conformer_conv_adaln_deepnorm"


def layer_norm(x, eps=LN_EPS):
    x32 = x.astype(jnp.float32)
    mu = jnp.mean(x32, axis=-1, keepdims=True)
    var = jnp.mean(jnp.square(x32 - mu), axis=-1, keepdims=True)
    return ((x32 - mu) * lax.rsqrt(var + eps)).astype(x.dtype)


def causal_multiscale_pool(u):
    b, s, _ = u.shape
    ug = u.reshape(b, s, N_POOL_GROUPS, POOL_GROUP_W).astype(jnp.float32)
    cs = jnp.cumsum(ug, axis=1)
    t = jnp.arange(1, s + 1, dtype=jnp.float32)
    outs = []
    for g, w in enumerate(POOL_WINDOWS):
        cg = cs[:, :, g]
        lagged = jnp.pad(cg[:, :-w], ((0, 0), (w, 0), (0, 0)))
        count = jnp.minimum(t, float(w))[:, None]
        outs.append((cg - lagged) / count)
    pooled = jnp.stack(outs, axis=2)
    return (pooled - ug).astype(u.dtype)


def causal_depthwise_conv(v, w_dw, b_dw):
    y = lax.conv_general_dilated(
        v, w_dw,
        window_strides=(1,),
        padding=[(CONV_WIDTH - 1, 0)],
        dimension_numbers=("NWC", "WIO", "NWC"),
        feature_group_count=v.shape[-1],
    )
    return y + b_dw


def setup_inputs(seed: int = 0) -> dict:
    key = jax.random.key(seed)
    ks = jax.random.split(key, 20)
    f32 = jnp.float32
    nrm = lambda k, shape, s: (jax.random.normal(k, shape, f32) * s)
    inputs = {
        "x": nrm(ks[0], (BATCH, SEQ, D_MODEL), 1.0),
        "c": nrm(ks[1], (BATCH, D_MODEL), 1.0),
        "w_ada": nrm(ks[2], (D_MODEL, 3 * D_MODEL), 0.5 * D_MODEL ** -0.5),
        "b_ada": nrm(ks[3], (3 * D_MODEL,), 0.01),
        "w_in": nrm(ks[4], (D_MODEL, D_IN), D_MODEL ** -0.5),
        "b_in": nrm(ks[5], (D_IN,), 0.01),
        "w_pool": nrm(ks[6], (N_POOL_GROUPS, POOL_GROUP_W, POOL_GROUP_W), DEEPNORM_BETA * POOL_GROUP_W ** -0.5),
        "b_pool": nrm(ks[7], (N_POOL_GROUPS, POOL_GROUP_W), 0.01),
        "ls_pool": 1.0 + nrm(ks[8], (D_POOL,), 0.02),
        "w_dw": nrm(ks[9], (CONV_WIDTH, 1, D_CONV), CONV_WIDTH ** -0.5),
        "b_dw": nrm(ks[10], (D_CONV,), 0.01),
        "ln_conv_g": 1.0 + nrm(ks[11], (D_CONV,), 0.02),
        "ln_conv_b": nrm(ks[12], (D_CONV,), 0.01),
        "w_pw": nrm(ks[13], (D_CONV, D_CONV), DEEPNORM_BETA * D_CONV ** -0.5),
        "b_pw": nrm(ks[14], (D_CONV,), 0.01),
        "w_out": nrm(ks[15], (D_MIX, D_MODEL), DEEPNORM_BETA * D_MIX ** -0.5),
        "b_out": nrm(ks[16], (D_MODEL,), 0.01),
        "ln_post_g": 1.0 + nrm(ks[17], (D_MODEL,), 0.02),
        "ln_post_b": nrm(ks[18], (D_MODEL,), 0.01),
    }
    return inputs


def reference(x, c, w_ada, b_ada, w_in, b_in, w_pool, b_pool, ls_pool, w_dw, b_dw,
              ln_conv_g, ln_conv_b, w_pw, b_pw, w_out, b_out, ln_post_g, ln_post_b):
    for _ in range(DEPTH):
        mod = jax.nn.silu(c) @ w_ada + b_ada
        shift, scale, gate = jnp.split(mod, 3, axis=-1)
        h = layer_norm(x) * (1.0 + scale[:, None, :]) + shift[:, None, :]

        proj = h @ w_in + b_in
        o1 = IN_SPLITS[0]
        o2 = o1 + IN_SPLITS[1]
        o3 = o2 + IN_SPLITS[2]
        u_a, z_a, glu_b, z_b = proj[..., :o1], proj[..., o1:o2], proj[..., o2:o3], proj[..., o3:]

        pooled = causal_multiscale_pool(u_a)
        y_a = jnp.einsum("bsgc,gcd->bsgd", pooled, w_pool) + b_pool
        y_a = y_a.reshape(y_a.shape[0], y_a.shape[1], D_POOL) * ls_pool * jax.nn.silu(z_a)

        v = glu_b[..., :D_CONV] * jax.nn.sigmoid(glu_b[..., D_CONV:])
        v = causal_depthwise_conv(v, w_dw, b_dw)
        v = jax.nn.silu(layer_norm(v) * ln_conv_g + ln_conv_b)
        y_b = (v @ w_pw + b_pw) * jax.nn.silu(z_b)

        y = jnp.concatenate([y_a, y_b], axis=-1) @ w_out + b_out

        x = layer_norm(DEEPNORM_ALPHA * x + gate[:, None, :] * y) * ln_post_g + ln_post_b
    return x
```

```python
import functools

import numpy as np
import jax
import jax.numpy as jnp
from jax import lax
from jax.experimental import pallas as pl
from jax.experimental.pallas import tpu as pltpu

F32 = jnp.float32
BF16 = jnp.bfloat16

POOL_WINDOWS = (2, 4, 8, 16)
POOL_GROUP_W = 128
CONV_WIDTH = 31
LN_EPS = 1e-5
DEEPNORM_ALPHA = 2.0 ** 0.25

SUBLANES = 8
SEQ_TILE = 256
HALO = 32
LN_ROWS = 16
POOL_ROWS = 32
CONV_ROWS = 32
GATE_ROWS = 32
MOD_COLS = 768
VMEM_LIMIT_BYTES = 48 * 1024 * 1024


def _rows(p8, n):
    return p8 if n == SUBLANES else jnp.concatenate([p8] * (n // SUBLANES), axis=0)


def _silu(z):
    return z * jax.nn.sigmoid(z)


def _layer_norm(t):
    mu = jnp.mean(t, axis=-1, keepdims=True)
    tc = t - mu
    var = jnp.mean(tc * tc, axis=-1, keepdims=True)
    return tc * lax.rsqrt(var + LN_EPS)


def _mod_kernel(c_ref, w_ref, b_ref, o_ref):
    c = c_ref[...]
    s = _silu(c).astype(BF16)
    o_ref[...] = jnp.dot(s, w_ref[...].astype(BF16), preferred_element_type=F32) + b_ref[...]


def _layer_kernel(x_ref, mod_ref, w_in_ref, b_in_ref, w_pool_ref, b_pool_ref, ls_ref, invc_ref,
                  w_dw_ref, b_dw_ref, lncg_ref, lncb_ref, w_pw_ref, b_pw_ref, w_out_ref, b_out_ref,
                  lnpg_ref, lnpb_ref, out_ref,
                  h_s, u_s, proj_s, p_s, v_s, c_s, cat_s, y_s):
    ts = SEQ_TILE
    d = x_ref.shape[-1]
    dp = u_s.shape[-1]
    i = pl.program_id(1)

    @pl.when(i == 0)
    def _():
        u_s[0:HALO, :] = jnp.zeros((HALO, dp), F32)
        v_s[0:HALO, :] = jnp.zeros((HALO, dp), F32)

    @pl.when(i > 0)
    def _():
        u_s[0:HALO, :] = u_s[ts:ts + HALO, :]
        v_s[0:HALO, :] = v_s[ts:ts + HALO, :]

    shift = mod_ref[0, :, 0:d]
    scale1 = 1.0 + mod_ref[0, :, d:2 * d]
    for r in range(0, ts, LN_ROWS):
        x = x_ref[0, r:r + LN_ROWS, :]
        h = _layer_norm(x) * _rows(scale1, LN_ROWS) + _rows(shift, LN_ROWS)
        h_s[r:r + LN_ROWS, :] = h.astype(BF16)

    h = h_s[...]
    u_s[HALO:HALO + ts, :] = (jnp.dot(h, w_in_ref[:, 0:dp], preferred_element_type=F32)
                              + _rows(b_in_ref[:, 0:dp], ts))
    proj_s[...] = (jnp.dot(h, w_in_ref[:, dp:], preferred_element_type=F32)
                   + _rows(b_in_ref[:, dp:], ts))

    for r in range(0, ts, POOL_ROWS):
        for g, w in enumerate(POOL_WINDOWS):
            lanes = slice(g * POOL_GROUP_W, (g + 1) * POOL_GROUP_W)
            win = u_s[HALO + r - 16:HALO + r + POOL_ROWS, lanes]
            s = win + pltpu.roll(win, 1, 0)
            if w >= 4:
                s = s + pltpu.roll(s, 2, 0)
            if w >= 8:
                s = s + pltpu.roll(s, 4, 0)
            if w >= 16:
                s = s[16:, :] + s[8:-8, :]
            else:
                s = s[16:, :]
            if r == 0:
                inv = jnp.where(i == 0, invc_ref[:, lanes], 1.0 / w)
            else:
                inv = 1.0 / w
            pooled = s * inv - win[16:, :]
            p_s[r:r + POOL_ROWS, lanes] = pooled.astype(BF16)

    p = p_s[...]
    half = dp // 2
    for j in range(2):
        cols = slice(j * half, (j + 1) * half)
        y_s[:, cols] = (jnp.dot(p[:, cols], w_pool_ref[j], preferred_element_type=F32)
                        + _rows(b_pool_ref[:, cols], ts))

    for r in range(0, ts, GATE_ROWS):
        a = proj_s[r:r + GATE_ROWS, dp:2 * dp]
        gt = proj_s[r:r + GATE_ROWS, 2 * dp:3 * dp]
        v_s[HALO + r:HALO + r + GATE_ROWS, :] = a * jax.nn.sigmoid(gt)

    for r in range(0, ts, CONV_ROWS):
        acc = _rows(b_dw_ref[...], CONV_ROWS)
        for k in range(CONV_WIDTH):
            lo = HALO + r - (CONV_WIDTH - 1) + k
            acc = acc + v_s[lo:lo + CONV_ROWS, :] * _rows(w_dw_ref[k], CONV_ROWS)
        vn = _layer_norm(acc) * _rows(lncg_ref[...], CONV_ROWS) + _rows(lncb_ref[...], CONV_ROWS)
        c_s[r:r + CONV_ROWS, :] = _silu(vn).astype(BF16)

    y_s[:, dp:] = (jnp.dot(c_s[...], w_pw_ref[...], preferred_element_type=F32)
                   + _rows(b_pw_ref[...], ts))

    for r in range(0, ts, GATE_ROWS):
        rows = slice(r, r + GATE_ROWS)
        ya = y_s[rows, 0:dp] * _rows(ls_ref[...], GATE_ROWS) * _silu(proj_s[rows, 0:dp])
        cat_s[rows, 0:dp] = ya.astype(BF16)
        yb = y_s[rows, dp:] * _silu(proj_s[rows, 3 * dp:4 * dp])
        cat_s[rows, dp:] = yb.astype(BF16)

    out_ref[0] = (jnp.dot(cat_s[...], w_out_ref[...], preferred_element_type=F32)
                  + _rows(b_out_ref[...], ts))
    gate = mod_ref[0, :, 2 * d:3 * d]
    for r in range(0, ts, LN_ROWS):
        rows = slice(r, r + LN_ROWS)
        t = DEEPNORM_ALPHA * x_ref[0, rows, :] + _rows(gate, LN_ROWS) * out_ref[0, rows, :]
        o = _layer_norm(t) * _rows(lnpg_ref[...], LN_ROWS) + _rows(lnpb_ref[...], LN_ROWS)
        out_ref[0, rows, :] = o


def _tile8(v):
    return jnp.broadcast_to(v.astype(F32)[None, :], (SUBLANES, v.shape[0]))


def _inv_count_table():
    t = np.arange(POOL_ROWS, dtype=np.float64)[:, None] + 1.0
    cols = [np.broadcast_to(1.0 / np.minimum(t, float(w)), (POOL_ROWS, POOL_GROUP_W))
            for w in POOL_WINDOWS]
    return jnp.asarray(np.concatenate(cols, axis=1), dtype=F32)


def _const_spec(shape):
    return pl.BlockSpec(shape, lambda b, i: (0,) * len(shape))


@jax.jit
def kernel(x, c, w_ada, b_ada, w_in, b_in, w_pool, b_pool, ls_pool, w_dw, b_dw, ln_conv_g, ln_conv_b,
           w_pw, b_pw, w_out, b_out, ln_post_g, ln_post_b):
    bsz, seq, d = x.shape
    dp = w_pool.shape[0] * w_pool.shape[1]
    dc = w_pw.shape[0]
    d_in = w_in.shape[1]
    assert seq % SEQ_TILE == 0 and dp == dc == len(POOL_WINDOWS) * POOL_GROUP_W
    assert d_in == 2 * dp + 3 * dc and w_dw.shape[0] == CONV_WIDTH

    c8 = jnp.broadcast_to(c[:, None, :], (bsz, SUBLANES, d)).reshape(bsz * SUBLANES, d)
    mod = pl.pallas_call(
        _mod_kernel,
        grid=(3 * d // MOD_COLS,),
        in_specs=[pl.BlockSpec((bsz * SUBLANES, d), lambda j: (0, 0)),
                  pl.BlockSpec((d, MOD_COLS), lambda j: (0, j)),
                  pl.BlockSpec((1, MOD_COLS), lambda j: (0, j))],
        out_specs=pl.BlockSpec((bsz * SUBLANES, MOD_COLS), lambda j: (0, j)),
        out_shape=jax.ShapeDtypeStruct((bsz * SUBLANES, 3 * d), F32),
        name="adaln_mod",
    )(c8, w_ada, b_ada[None, :])
    mod = mod.reshape(bsz, SUBLANES, 3 * d)

    gw = POOL_GROUP_W
    z = jnp.zeros((gw, gw), w_pool.dtype)
    w_pool2 = jnp.stack([
        jnp.block([[w_pool[0], z], [z, w_pool[1]]]),
        jnp.block([[w_pool[2], z], [z, w_pool[3]]]),
    ]).astype(BF16)
    w_dw8 = jnp.broadcast_to(w_dw.reshape(CONV_WIDTH, 1, dc), (CONV_WIDTH, SUBLANES, dc))

    operands = (
        x, mod, w_in.astype(BF16), _tile8(b_in), w_pool2, _tile8(b_pool.reshape(-1)), _tile8(ls_pool),
        _inv_count_table(), w_dw8, _tile8(b_dw), _tile8(ln_conv_g), _tile8(ln_conv_b),
        w_pw.astype(BF16), _tile8(b_pw), w_out.astype(BF16), _tile8(b_out),
        _tile8(ln_post_g), _tile8(ln_post_b),
    )
    in_specs = [
        pl.BlockSpec((1, SEQ_TILE, d), lambda b, i: (b, i, 0)),
        pl.BlockSpec((1, SUBLANES, 3 * d), lambda b, i: (b, 0, 0)),
    ] + [_const_spec(op.shape) for op in operands[2:]]

    return pl.pallas_call(
        _layer_kernel,
        grid=(bsz, seq // SEQ_TILE),
        in_specs=in_specs,
        out_specs=pl.BlockSpec((1, SEQ_TILE, d), lambda b, i: (b, i, 0)),
        out_shape=jax.ShapeDtypeStruct((bsz, seq, d), x.dtype),
        scratch_shapes=[
            pltpu.VMEM((SEQ_TILE, d), BF16),
            pltpu.VMEM((HALO + SEQ_TILE, dp), F32),
            pltpu.VMEM((SEQ_TILE, dp + 3 * dc), F32),
            pltpu.VMEM((SEQ_TILE, dp), BF16),
            pltpu.VMEM((HALO + SEQ_TILE, dc), F32),
            pltpu.VMEM((SEQ_TILE, dc), BF16),
            pltpu.VMEM((SEQ_TILE, dp + dc), BF16),
            pltpu.VMEM((SEQ_TILE, dp + dc), F32),
        ],
        compiler_params=pltpu.CompilerParams(
            dimension_semantics=("arbitrary", "arbitrary"),
            vmem_limit_bytes=VMEM_LIMIT_BYTES,
        ),
        name="hybrid_layer",
    )(*operands)
```

```python
import numpy as np
import jax
import jax.numpy as jnp
from jax import lax
from jax.experimental import pallas as pl
from jax.experimental.pallas import tpu as pltpu

F32 = jnp.float32
BF16 = jnp.bfloat16

POOL_WINDOWS = (2, 4, 8, 16)
CONV_WIDTH = 31
LN_EPS = 1e-5
DEEPNORM_ALPHA = 2.0 ** 0.25

SUBLANES = 8
LANES = 128
SEQ_TILE = 256
HALO = 32
ROW_STRIDE = 4
BLOCK_ROWS = SUBLANES * ROW_STRIDE
LN_ROWS = 16
GATE_ROWS = 32
MOD_COLS = 768
VMEM_LIMIT_BYTES = 48 * 1024 * 1024


def _rows(p8, n):
    return p8 if n == SUBLANES else jnp.concatenate([p8] * (n // SUBLANES), axis=0)


def _silu(z):
    return z * jax.nn.sigmoid(z)


def _layer_norm(t):
    mu = jnp.mean(t, axis=-1, keepdims=True)
    tc = t - mu
    var = jnp.mean(tc * tc, axis=-1, keepdims=True)
    return tc * lax.rsqrt(var + LN_EPS)


def _strided_rows(ref, slab, start):
    return ref[slab, pl.ds(start, SUBLANES, stride=ROW_STRIDE), :]


def _mod_kernel(c_ref, w_ref, b_ref, o_ref):
    c = c_ref[...]
    s = _silu(c).astype(BF16)
    o_ref[...] = jnp.dot(s, w_ref[...].astype(BF16), preferred_element_type=F32) + b_ref[...]


def _layer_kernel(x_ref, mod_ref, w_in_ref, b_in_ref, w_pool_ref, b_pool_ref, ls_ref, invc_ref,
                  w_dw_ref, b_dw_ref, lncg_ref, lncb_ref, w_pw_ref, b_pw_ref, w_out_ref, b_out_ref,
                  lnpg_ref, lnpb_ref, out_ref,
                  h_s, u_s, proj_s, pn_s, v_s, cn_s, c_s, cat_s, y_s):
    ts = SEQ_TILE
    d = x_ref.shape[-1]
    n_slab = u_s.shape[0]
    dp = n_slab * LANES
    i = pl.program_id(1)

    @pl.when(i == 0)
    def _():
        u_s[:, 0:HALO, :] = jnp.zeros((n_slab, HALO, LANES), F32)
        v_s[:, 0:HALO, :] = jnp.zeros((n_slab, HALO, LANES), F32)

    @pl.when(i > 0)
    def _():
        u_s[:, 0:HALO, :] = u_s[:, ts:ts + HALO, :]
        v_s[:, 0:HALO, :] = v_s[:, ts:ts + HALO, :]

    shift = mod_ref[0, :, 0:d]
    scale1 = 1.0 + mod_ref[0, :, d:2 * d]
    for r in range(0, ts, LN_ROWS):
        x = x_ref[0, r:r + LN_ROWS, :]
        h = _layer_norm(x) * _rows(scale1, LN_ROWS) + _rows(shift, LN_ROWS)
        h_s[r:r + LN_ROWS, :] = h.astype(BF16)

    h = h_s[...]
    u = jnp.dot(h, w_in_ref[:, 0:dp], preferred_element_type=F32) + _rows(b_in_ref[:, 0:dp], ts)
    for g in range(n_slab):
        u_s[g, HALO:HALO + ts, :] = u[:, g * LANES:(g + 1) * LANES]
    proj_s[...] = (jnp.dot(h, w_in_ref[:, dp:], preferred_element_type=F32)
                   + _rows(b_in_ref[:, dp:], ts))

    for t0 in range(0, ts, BLOCK_ROWS):
        for g, w in enumerate(POOL_WINDOWS):
            s = {q: _strided_rows(u_s, g, HALO + t0 + q) for q in range(1 - w, ROW_STRIDE)}
            tok = s
            span = 1
            while span < w:
                s = {q: s[q] + s[q - span] for q in s if q - span in s}
                span *= 2
            for q in range(ROW_STRIDE):
                inv = jnp.where(i == 0, invc_ref[g, q], 1.0 / w) if t0 == 0 else 1.0 / w
                pn_s[g, pl.ds(t0 + q, SUBLANES, stride=ROW_STRIDE), :] = s[q] * inv - tok[q]

    p = jnp.concatenate([pn_s[g] for g in range(n_slab)], axis=1).astype(BF16)
    half = dp // 2
    for j in range(2):
        cols = slice(j * half, (j + 1) * half)
        y_s[:, cols] = (jnp.dot(p[:, cols], w_pool_ref[j], preferred_element_type=F32)
                        + _rows(b_pool_ref[:, cols], ts))

    for r in range(0, ts, GATE_ROWS):
        a = proj_s[r:r + GATE_ROWS, dp:2 * dp]
        gt = proj_s[r:r + GATE_ROWS, 2 * dp:3 * dp]
        v = a * jax.nn.sigmoid(gt)
        for cb in range(n_slab):
            v_s[cb, HALO + r:HALO + r + GATE_ROWS, :] = v[:, cb * LANES:(cb + 1) * LANES]

    for cb in range(n_slab):
        lanes = slice(cb * LANES, (cb + 1) * LANES)
        taps = [w_dw_ref[k, :, lanes] for k in range(CONV_WIDTH)]
        bias = b_dw_ref[:, lanes]
        for t0 in range(0, ts, BLOCK_ROWS):
            acc = [bias] * ROW_STRIDE
            for qi in range(1 - CONV_WIDTH, ROW_STRIDE):
                vin = _strided_rows(v_s, cb, HALO + t0 + qi)
                for q in range(ROW_STRIDE):
                    delay = q - qi
                    if 0 <= delay < CONV_WIDTH:
                        acc[q] = acc[q] + vin * taps[CONV_WIDTH - 1 - delay]
            for q in range(ROW_STRIDE):
                cn_s[cb, pl.ds(t0 + q, SUBLANES, stride=ROW_STRIDE), :] = acc[q]

    for r in range(0, ts, LN_ROWS):
        cv = jnp.concatenate([cn_s[cb, r:r + LN_ROWS, :] for cb in range(n_slab)], axis=1)
        vn = _layer_norm(cv) * _rows(lncg_ref[...], LN_ROWS) + _rows(lncb_ref[...], LN_ROWS)
        c_s[r:r + LN_ROWS, :] = _silu(vn).astype(BF16)

    y_s[:, dp:] = (jnp.dot(c_s[...], w_pw_ref[...], preferred_element_type=F32)
                   + _rows(b_pw_ref[...], ts))

    for r in range(0, ts, GATE_ROWS):
        rows = slice(r, r + GATE_ROWS)
        ya = y_s[rows, 0:dp] * _rows(ls_ref[...], GATE_ROWS) * _silu(proj_s[rows, 0:dp])
        cat_s[rows, 0:dp] = ya.astype(BF16)
        yb = y_s[rows, dp:] * _silu(proj_s[rows, 3 * dp:4 * dp])
        cat_s[rows, dp:] = yb.astype(BF16)

    out_ref[0] = (jnp.dot(cat_s[...], w_out_ref[...], preferred_element_type=F32)
                  + _rows(b_out_ref[...], ts))
    gate = mod_ref[0, :, 2 * d:3 * d]
    for r in range(0, ts, LN_ROWS):
        rows = slice(r, r + LN_ROWS)
        t = DEEPNORM_ALPHA * x_ref[0, rows, :] + _rows(gate, LN_ROWS) * out_ref[0, rows, :]
        o = _layer_norm(t) * _rows(lnpg_ref[...], LN_ROWS) + _rows(lnpb_ref[...], LN_ROWS)
        out_ref[0, rows, :] = o


def _tile8(v):
    return jnp.broadcast_to(v.astype(F32)[None, :], (SUBLANES, v.shape[0]))


def _inv_count_table():
    q = np.arange(ROW_STRIDE, dtype=np.float64)[:, None]
    s = np.arange(SUBLANES, dtype=np.float64)[None, :]
    t1 = q + ROW_STRIDE * s + 1.0
    tab = np.stack([1.0 / np.minimum(t1, float(w)) for w in POOL_WINDOWS])
    return jnp.asarray(np.broadcast_to(tab[..., None], tab.shape + (LANES,)), dtype=F32)


def _const_spec(shape):
    return pl.BlockSpec(shape, lambda b, i: (0,) * len(shape))


@jax.jit
def kernel(x, c, w_ada, b_ada, w_in, b_in, w_pool, b_pool, ls_pool, w_dw, b_dw, ln_conv_g, ln_conv_b,
           w_pw, b_pw, w_out, b_out, ln_post_g, ln_post_b):
    bsz, seq, d = x.shape
    n_grp, gw = w_pool.shape[0], w_pool.shape[1]
    dp = n_grp * gw
    dc = w_pw.shape[0]
    d_in = w_in.shape[1]
    assert seq % SEQ_TILE == 0 and SEQ_TILE % BLOCK_ROWS == 0
    assert gw == LANES and n_grp == len(POOL_WINDOWS) and dp == dc
    assert d_in == 2 * dp + 3 * dc and w_dw.shape[0] == CONV_WIDTH
    assert HALO >= CONV_WIDTH - 1 and HALO >= max(POOL_WINDOWS) - 1
    n_slab = dp // LANES

    c8 = jnp.broadcast_to(c[:, None, :], (bsz, SUBLANES, d)).reshape(bsz * SUBLANES, d)
    mod = pl.pallas_call(
        _mod_kernel,
        grid=(3 * d // MOD_COLS,),
        in_specs=[pl.BlockSpec((bsz * SUBLANES, d), lambda j: (0, 0)),
                  pl.BlockSpec((d, MOD_COLS), lambda j: (0, j)),
                  pl.BlockSpec((1, MOD_COLS), lambda j: (0, j))],
        out_specs=pl.BlockSpec((bsz * SUBLANES, MOD_COLS), lambda j: (0, j)),
        out_shape=jax.ShapeDtypeStruct((bsz * SUBLANES, 3 * d), F32),
        name="adaln_mod",
    )(c8, w_ada, b_ada[None, :])
    mod = mod.reshape(bsz, SUBLANES, 3 * d)

    z = jnp.zeros((gw, gw), w_pool.dtype)
    w_pool2 = jnp.stack([
        jnp.block([[w_pool[0], z], [z, w_pool[1]]]),
        jnp.block([[w_pool[2], z], [z, w_pool[3]]]),
    ]).astype(BF16)
    w_dw8 = jnp.broadcast_to(w_dw.reshape(CONV_WIDTH, 1, dc), (CONV_WIDTH, SUBLANES, dc))

    operands = (
        x, mod, w_in.astype(BF16), _tile8(b_in), w_pool2, _tile8(b_pool.reshape(-1)), _tile8(ls_pool),
        _inv_count_table(), w_dw8, _tile8(b_dw), _tile8(ln_conv_g), _tile8(ln_conv_b),
        w_pw.astype(BF16), _tile8(b_pw), w_out.astype(BF16), _tile8(b_out),
        _tile8(ln_post_g), _tile8(ln_post_b),
    )
    in_specs = [
        pl.BlockSpec((1, SEQ_TILE, d), lambda b, i: (b, i, 0)),
        pl.BlockSpec((1, SUBLANES, 3 * d), lambda b, i: (b, 0, 0)),
    ] + [_const_spec(op.shape) for op in operands[2:]]

    return pl.pallas_call(
        _layer_kernel,
        grid=(bsz, seq // SEQ_TILE),
        in_specs=in_specs,
        out_specs=pl.BlockSpec((1, SEQ_TILE, d), lambda b, i: (b, i, 0)),
        out_shape=jax.ShapeDtypeStruct((bsz, seq, d), x.dtype),
        scratch_shapes=[
            pltpu.VMEM((SEQ_TILE, d), BF16),
            pltpu.VMEM((n_slab, HALO + SEQ_TILE, LANES), F32),
            pltpu.VMEM((SEQ_TILE, dp + 3 * dc), F32),
            pltpu.VMEM((n_slab, SEQ_TILE, LANES), F32),
            pltpu.VMEM((n_slab, HALO + SEQ_TILE, LANES), F32),
            pltpu.VMEM((n_slab, SEQ_TILE, LANES), F32),
            pltpu.VMEM((SEQ_TILE, dc), BF16),
            pltpu.VMEM((SEQ_TILE, dp + dc), BF16),
            pltpu.VMEM((SEQ_TILE, dp + dc), F32),
        ],
        compiler_params=pltpu.CompilerParams(
            dimension_semantics=("arbitrary", "arbitrary"),
            vmem_limit_bytes=VMEM_LIMIT_BYTES,
        ),
        name="hybrid_layer",
    )(*operands)
```

```python
import collections
import functools

import numpy as np
import jax
import jax.numpy as jnp
from jax import lax
from jax.experimental import pallas as pl
from jax.experimental.pallas import tpu as pltpu

F32 = jnp.float32
BF16 = jnp.bfloat16

POOL_WINDOWS = (2, 4, 8, 16)
CONV_WIDTH = 31
LN_EPS = 1e-5
DEEPNORM_ALPHA = 2.0 ** 0.25

SUBLANES = 8
LANES = 128
MXU_COLS = 256
SEQ_TILE = 256
HALO = 32
ROW_STRIDE = 4
BLOCK_ROWS = SUBLANES * ROW_STRIDE
LN_ROWS = 16
GATE_ROWS = 32
MOD_COLS = 768
PIPE_DEPTH = 3
TOKEN_LAG = 3
VMEM_LIMIT_BYTES = 48 * 1024 * 1024

_Piece = collections.namedtuple("_Piece", "name cost emit after")


def _emit_interleaved(mxu_pieces, vpu_pieces):
    done = set()
    t = {"m": 0.0, "v": 0.0}
    queues = {"m": list(mxu_pieces), "v": list(vpu_pieces)}
    tokens = []

    def ready(k):
        return bool(queues[k]) and (queues[k][0].after is None or queues[k][0].after in done)

    while queues["m"] or queues["v"]:
        assert ready("m") or ready("v"), "piece order has a dependency cycle"
        k = "m" if ready("m") and (not ready("v") or t["m"] <= t["v"]) else "v"
        piece = queues[k].pop(0)
        if k == "m":
            tokens.append(piece.emit())
        else:
            piece.emit(tokens.pop(0) if len(tokens) > TOKEN_LAG else None)
        done.add(piece.name)
        t[k] += piece.cost


def _after(never, token, val):
    if token is None:
        return val
    first = jnp.where(never, token, val[0:SUBLANES, 0:LANES])
    top = first if val.shape[1] == LANES else jnp.concatenate([first, val[0:SUBLANES, LANES:]], axis=1)
    return top if val.shape[0] == SUBLANES else jnp.concatenate([top, val[SUBLANES:, :]], axis=0)


def _token(res):
    return res[-SUBLANES:, -LANES:]


def _rows(p8, n):
    return p8 if n == SUBLANES else jnp.concatenate([p8] * (n // SUBLANES), axis=0)


def _silu(z):
    return z * jax.nn.sigmoid(z)


def _layer_norm(t):
    mu = jnp.mean(t, axis=-1, keepdims=True)
    tc = t - mu
    var = jnp.mean(tc * tc, axis=-1, keepdims=True)
    return tc * lax.rsqrt(var + LN_EPS)


def _strided_rows(ref, slab, start):
    return ref[slab, pl.ds(start, SUBLANES, stride=ROW_STRIDE), :]


def _mod_kernel(c_ref, w_ref, b_ref, o_ref):
    c = c_ref[...]
    s = _silu(c).astype(BF16)
    o_ref[...] = jnp.dot(s, w_ref[...].astype(BF16), preferred_element_type=F32) + b_ref[...]


def _step(first, never, xn_ref, xo_ref, modn_ref, modo_ref, w_in_ref, b_in_ref, w_pool_ref, b_pool_ref, ls_ref,
          invc_ref, w_dw_ref, b_dw_ref, lncg_ref, lncb_ref, w_pw_ref, b_pw_ref, w_out_ref, b_out_ref,
          lnpg_ref, lnpb_ref, out_ref, h_new, h_old, proj_new, proj_old, cat_new, cat_old,
          u_s, pn_s, v_s, cn_s, c_s, y_s):
    ts = SEQ_TILE
    d = xn_ref.shape[-1]
    d_in = w_in_ref.shape[-1]
    n_slab = u_s.shape[0]
    dp = n_slab * LANES

    def norm_in(r, tok):
        x = _after(never, tok, xn_ref[0, r:r + LN_ROWS, :])
        scale1 = 1.0 + modn_ref[0, :, d:2 * d]
        h = _layer_norm(x) * _rows(scale1, LN_ROWS) + _rows(modn_ref[0, :, 0:d], LN_ROWS)
        h_new[r:r + LN_ROWS, :] = h.astype(BF16)

    def project(c0):
        cols = slice(c0, c0 + MXU_COLS)
        res = (jnp.dot(h_old[...], w_in_ref[:, cols], preferred_element_type=F32)
               + _rows(b_in_ref[:, cols], ts))
        proj_new[:, cols] = res
        return _token(res)

    def carry_rows(tok):
        del tok
        u_s[:, 0:HALO, :] = jnp.where(first, 0.0, u_s[:, ts:ts + HALO, :])
        v_s[:, 0:HALO, :] = jnp.where(first, 0.0, v_s[:, ts:ts + HALO, :])
        for g in range(n_slab):
            u_s[g, HALO:HALO + ts, :] = proj_old[:, g * LANES:(g + 1) * LANES]

    def pool(t0, tok):
        for g, w in enumerate(POOL_WINDOWS):
            s = {q: _strided_rows(u_s, g, HALO + t0 + q) for q in range(1 - w, ROW_STRIDE)}
            if g == 0:
                s[1 - w] = _after(never, tok, s[1 - w])
            cur = s
            span = 1
            while span < w:
                s = {q: s[q] + s[q - span] for q in s if q - span in s}
                span *= 2
            for q in range(ROW_STRIDE):
                inv = jnp.where(first, invc_ref[g, q], 1.0 / w) if t0 == 0 else 1.0 / w
                pn_s[g, pl.ds(t0 + q, SUBLANES, stride=ROW_STRIDE), :] = s[q] * inv - cur[q]

    def pool_dot():
        p = jnp.concatenate([pn_s[g] for g in range(n_slab)], axis=1).astype(BF16)
        for j in range(dp // MXU_COLS):
            cols = slice(j * MXU_COLS, (j + 1) * MXU_COLS)
            res = (jnp.dot(p[:, cols], w_pool_ref[j], preferred_element_type=F32)
                   + _rows(b_pool_ref[:, cols], ts))
            y_s[:, cols] = res
        return _token(res)

    def glu(r, tok):
        a = _after(never, tok, proj_old[r:r + GATE_ROWS, 2 * dp:3 * dp])
        gt = proj_old[r:r + GATE_ROWS, 3 * dp:4 * dp]
        v = a * jax.nn.sigmoid(gt)
        for cb in range(n_slab):
            v_s[cb, HALO + r:HALO + r + GATE_ROWS, :] = v[:, cb * LANES:(cb + 1) * LANES]

    def conv(cb, t0, tok):
        lanes = slice(cb * LANES, (cb + 1) * LANES)
        acc = [b_dw_ref[:, lanes]] * ROW_STRIDE
        for qi in range(1 - CONV_WIDTH, ROW_STRIDE):
            vin = _strided_rows(v_s, cb, HALO + t0 + qi)
            if qi == 1 - CONV_WIDTH:
                vin = _after(never, tok, vin)
            for q in range(ROW_STRIDE):
                delay = q - qi
                if 0 <= delay < CONV_WIDTH:
                    acc[q] = acc[q] + vin * w_dw_ref[CONV_WIDTH - 1 - delay, :, lanes]
        for q in range(ROW_STRIDE):
            cn_s[cb, pl.ds(t0 + q, SUBLANES, stride=ROW_STRIDE), :] = acc[q]

    def conv_norm(r, tok):
        cv = jnp.concatenate([cn_s[cb, r:r + LN_ROWS, :] for cb in range(n_slab)], axis=1)
        cv = _after(never, tok, cv)
        vn = _layer_norm(cv) * _rows(lncg_ref[...], LN_ROWS) + _rows(lncb_ref[...], LN_ROWS)
        c_s[r:r + LN_ROWS, :] = _silu(vn).astype(BF16)

    def pointwise_dot():
        res = (jnp.dot(c_s[...], w_pw_ref[...], preferred_element_type=F32)
               + _rows(b_pw_ref[...], ts))
        y_s[:, dp:] = res
        return _token(res)

    def gates(r, tok):
        rows = slice(r, r + GATE_ROWS)
        ya = (_after(never, tok, y_s[rows, 0:dp]) * _rows(ls_ref[...], GATE_ROWS)
              * _silu(proj_old[rows, dp:2 * dp]))
        cat_new[rows, 0:dp] = ya.astype(BF16)
        yb = y_s[rows, dp:] * _silu(proj_old[rows, 4 * dp:5 * dp])
        cat_new[rows, dp:] = yb.astype(BF16)

    def out_dot(c0):
        cols = slice(c0, c0 + MXU_COLS)
        res = (jnp.dot(cat_old[...], w_out_ref[:, cols], preferred_element_type=F32)
               + _rows(b_out_ref[:, cols], ts))
        out_ref[0, :, cols] = res
        return _token(res)

    def norm_out(r, tok):
        rows = slice(r, r + LN_ROWS)
        gate = modo_ref[0, :, 2 * d:3 * d]
        xr = _after(never, tok, xo_ref[0, rows, :])
        t = DEEPNORM_ALPHA * xr + _rows(gate, LN_ROWS) * out_ref[0, rows, :]
        o = _layer_norm(t) * _rows(lnpg_ref[...], LN_ROWS) + _rows(lnpb_ref[...], LN_ROWS)
        out_ref[0, rows, :] = o

    P = functools.partial
    ln_chunks = range(0, ts, LN_ROWS)
    gate_chunks = range(0, ts, GATE_ROWS)
    blocks = range(0, ts, BLOCK_ROWS)
    proj_chunks = list(range(0, d_in, MXU_COLS))
    out_chunks = list(range(0, d, MXU_COLS))
    n_early = 8

    mxu = ([_Piece(f"out_dot{c}", 260, P(out_dot, c), None) for c in out_chunks]
           + [_Piece("pool_dot", 130, pool_dot, f"pool{blocks[-1]}")]
           + [_Piece(f"proj{c}", 260, P(project, c), None) for c in proj_chunks[:n_early]]
           + [_Piece("pw_dot", 260, pointwise_dot, f"conv_norm{ln_chunks[-1]}")]
           + [_Piece(f"proj{c}", 260, P(project, c), None) for c in proj_chunks[n_early:]])
    vpu = ([_Piece("carry", 10, carry_rows, None)]
           + [_Piece(f"pool{t0}", 68, P(pool, t0), None) for t0 in blocks]
           + [_Piece(f"glu{r}", 40, P(glu, r), None) for r in gate_chunks]
           + [_Piece(f"conv{cb}_{t0}", 62, P(conv, cb, t0), None) for cb in range(n_slab) for t0 in blocks]
           + [_Piece(f"conv_norm{r}", 25, P(conv_norm, r), None) for r in ln_chunks]
           + [_Piece(f"norm_out{r}", 44, P(norm_out, r), f"out_dot{out_chunks[-1]}") for r in ln_chunks]
           + [_Piece(f"norm_in{r}", 30, P(norm_in, r), None) for r in ln_chunks]
           + [_Piece(f"gates{r}", 55, P(gates, r), "pw_dot") for r in gate_chunks])
    _emit_interleaved(mxu, vpu)


def _layer_kernel(tiles_per_seq, xn_ref, xo_ref, modn_ref, modo_ref, *rest):
    params, out_ref = rest[:16], rest[16]
    h0_s, h1_s, proj0_s, proj1_s, cat0_s, cat1_s = rest[17:23]
    mix_scratch = rest[23:]
    g = pl.program_id(0)

    @pl.when(g == 0)
    def _():
        for ref in (h0_s, h1_s, proj0_s, proj1_s, cat0_s, cat1_s, mix_scratch[0], mix_scratch[2]):
            ref[...] = jnp.zeros(ref.shape, ref.dtype)

    first = (jnp.maximum(g - 2, 0) % tiles_per_seq) == 0
    never = g < 0

    def step(h_new, h_old, proj_new, proj_old, cat_new, cat_old):
        _step(first, never, xn_ref, xo_ref, modn_ref, modo_ref, *params, out_ref,
              h_new, h_old, proj_new, proj_old, cat_new, cat_old, *mix_scratch)

    @pl.when(g % 2 == 0)
    def _():
        step(h0_s, h1_s, proj1_s, proj0_s, cat0_s, cat1_s)

    @pl.when(g % 2 == 1)
    def _():
        step(h1_s, h0_s, proj0_s, proj1_s, cat1_s, cat0_s)


def _tile8(v):
    return jnp.broadcast_to(v.astype(F32)[None, :], (SUBLANES, v.shape[0]))


def _inv_count_table():
    q = np.arange(ROW_STRIDE, dtype=np.float64)[:, None]
    s = np.arange(SUBLANES, dtype=np.float64)[None, :]
    t1 = q + ROW_STRIDE * s + 1.0
    tab = np.stack([1.0 / np.minimum(t1, float(w)) for w in POOL_WINDOWS])
    return jnp.asarray(np.broadcast_to(tab[..., None], tab.shape + (LANES,)), dtype=F32)


def _const_spec(shape):
    return pl.BlockSpec(shape, lambda g: (0,) * len(shape))


@jax.jit
def kernel(x, c, w_ada, b_ada, w_in, b_in, w_pool, b_pool, ls_pool, w_dw, b_dw, ln_conv_g, ln_conv_b,
           w_pw, b_pw, w_out, b_out, ln_post_g, ln_post_b):
    bsz, seq, d = x.shape
    n_grp, gw = w_pool.shape[0], w_pool.shape[1]
    dp = n_grp * gw
    dc = w_pw.shape[0]
    d_in = w_in.shape[1]
    assert seq % SEQ_TILE == 0 and SEQ_TILE % BLOCK_ROWS == 0
    assert gw == LANES and n_grp == len(POOL_WINDOWS) and dp == dc
    assert d_in == 2 * dp + 3 * dc and w_dw.shape[0] == CONV_WIDTH
    assert d_in % MXU_COLS == 0 and d % MXU_COLS == 0 and dp % MXU_COLS == 0
    assert HALO >= CONV_WIDTH - 1 and HALO >= max(POOL_WINDOWS) - 1
    n_slab = dp // LANES
    tiles_per_seq = seq // SEQ_TILE
    n_tiles = bsz * tiles_per_seq

    c8 = jnp.broadcast_to(c[:, None, :], (bsz, SUBLANES, d)).reshape(bsz * SUBLANES, d)
    mod = pl.pallas_call(
        _mod_kernel,
        grid=(3 * d // MOD_COLS,),
        in_specs=[pl.BlockSpec((bsz * SUBLANES, d), lambda j: (0, 0)),
                  pl.BlockSpec((d, MOD_COLS), lambda j: (0, j)),
                  pl.BlockSpec((1, MOD_COLS), lambda j: (0, j))],
        out_specs=pl.BlockSpec((bsz * SUBLANES, MOD_COLS), lambda j: (0, j)),
        out_shape=jax.ShapeDtypeStruct((bsz * SUBLANES, 3 * d), F32),
        name="adaln_mod",
    )(c8, w_ada, b_ada[None, :])
    mod = mod.reshape(bsz, SUBLANES, 3 * d)

    z = jnp.zeros((gw, gw), w_pool.dtype)
    w_pool2 = jnp.stack([
        jnp.block([[w_pool[0], z], [z, w_pool[1]]]),
        jnp.block([[w_pool[2], z], [z, w_pool[3]]]),
    ]).astype(BF16)
    w_dw8 = jnp.broadcast_to(w_dw.reshape(CONV_WIDTH, 1, dc), (CONV_WIDTH, SUBLANES, dc))

    params = (
        w_in.astype(BF16), _tile8(b_in), w_pool2, _tile8(b_pool.reshape(-1)), _tile8(ls_pool),
        _inv_count_table(), w_dw8, _tile8(b_dw), _tile8(ln_conv_g), _tile8(ln_conv_b),
        w_pw.astype(BF16), _tile8(b_pw), w_out.astype(BF16), _tile8(b_out),
        _tile8(ln_post_g), _tile8(ln_post_b),
    )

    def tile_n(g):
        return jnp.minimum(g, n_tiles - 1)

    def tile_o(g):
        return jnp.maximum(g - PIPE_DEPTH, 0)

    def x_spec(tile):
        return pl.BlockSpec((1, SEQ_TILE, d),
                            lambda g: (tile(g) // tiles_per_seq, tile(g) % tiles_per_seq, 0))

    def mod_spec(tile):
        return pl.BlockSpec((1, SUBLANES, 3 * d), lambda g: (tile(g) // tiles_per_seq, 0, 0))

    in_specs = ([x_spec(tile_n), x_spec(tile_o), mod_spec(tile_n), mod_spec(tile_o)]
                + [_const_spec(p.shape) for p in params])

    return pl.pallas_call(
        functools.partial(_layer_kernel, tiles_per_seq),
        grid=(n_tiles + PIPE_DEPTH,),
        in_specs=in_specs,
        out_specs=x_spec(tile_o),
        out_shape=jax.ShapeDtypeStruct((bsz, seq, d), x.dtype),
        scratch_shapes=[
            pltpu.VMEM((SEQ_TILE, d), BF16),
            pltpu.VMEM((SEQ_TILE, d), BF16),
            pltpu.VMEM((SEQ_TILE, d_in), F32),
            pltpu.VMEM((SEQ_TILE, d_in), F32),
            pltpu.VMEM((SEQ_TILE, dp + dc), BF16),
            pltpu.VMEM((SEQ_TILE, dp + dc), BF16),
            pltpu.VMEM((n_slab, HALO + SEQ_TILE, LANES), F32),
            pltpu.VMEM((n_slab, SEQ_TILE, LANES), F32),
            pltpu.VMEM((n_slab, HALO + SEQ_TILE, LANES), F32),
            pltpu.VMEM((n_slab, SEQ_TILE, LANES), F32),
            pltpu.VMEM((SEQ_TILE, dc), BF16),
            pltpu.VMEM((SEQ_TILE, dp + dc), F32),
        ],
        compiler_params=pltpu.CompilerParams(
            dimension_semantics=("arbitrary",),
            vmem_limit_bytes=VMEM_LIMIT_BYTES,
        ),
        name="hybrid_layer",
    )(x, x, mod, mod, *params)
```

```python
import collections
import functools

import numpy as np
import jax
import jax.numpy as jnp
from jax import lax
from jax.experimental import pallas as pl
from jax.experimental.pallas import tpu as pltpu

F32 = jnp.float32
BF16 = jnp.bfloat16

POOL_WINDOWS = (2, 4, 8, 16)
CONV_WIDTH = 31
LN_EPS = 1e-5
DEEPNORM_ALPHA = 2.0 ** 0.25

SUBLANES = 8
LANES = 128
MXU_COLS = 256
SEQ_TILE = 256
HALO = 32
ROW_STRIDE = 4
BLOCK_ROWS = SUBLANES * ROW_STRIDE
LN_ROWS = 16
GATE_ROWS = 32
MOD_COLS = 768
PIPE_DEPTH = 3
TOKEN_LAG = 3
VMEM_LIMIT_BYTES = 48 * 1024 * 1024

_Piece = collections.namedtuple("_Piece", "name cost emit after")


def _emit_interleaved(mxu_pieces, vpu_pieces):
    done = set()
    t = {"m": 0.0, "v": 0.0}
    queues = {"m": list(mxu_pieces), "v": list(vpu_pieces)}
    tokens = []

    def ready(k):
        return bool(queues[k]) and (queues[k][0].after is None or queues[k][0].after in done)

    while queues["m"] or queues["v"]:
        assert ready("m") or ready("v"), "piece order has a dependency cycle"
        k = "m" if ready("m") and (not ready("v") or t["m"] <= t["v"]) else "v"
        piece = queues[k].pop(0)
        if k == "m":
            tokens.append(piece.emit())
        else:
            piece.emit(tokens.pop(0) if len(tokens) > TOKEN_LAG else None)
        done.add(piece.name)
        t[k] += piece.cost


def _after(never, token, val):
    if token is None:
        return val
    first = jnp.where(never, token, val[0:SUBLANES, 0:LANES])
    top = first if val.shape[1] == LANES else jnp.concatenate([first, val[0:SUBLANES, LANES:]], axis=1)
    return top if val.shape[0] == SUBLANES else jnp.concatenate([top, val[SUBLANES:, :]], axis=0)


def _token(res):
    return res[-SUBLANES:, -LANES:]


def _rows(p8, n):
    return p8 if n == SUBLANES else jnp.concatenate([p8] * (n // SUBLANES), axis=0)


def _silu(z):
    return z * jax.nn.sigmoid(z)


def _layer_norm(t):
    mu = jnp.mean(t, axis=-1, keepdims=True)
    tc = t - mu
    var = jnp.mean(tc * tc, axis=-1, keepdims=True)
    return tc * lax.rsqrt(var + LN_EPS)


def _strided_rows(ref, slab, start):
    return ref[slab, pl.ds(start, SUBLANES, stride=ROW_STRIDE), :]


def _mod_kernel(c_ref, w_ref, b_ref, o_ref):
    c = c_ref[...]
    s = _silu(c).astype(BF16)
    o_ref[...] = jnp.dot(s, w_ref[...].astype(BF16), preferred_element_type=F32) + b_ref[...]


def _layer_kernel(tiles_per_seq, xn_ref, xo_ref, modn_ref, modo_ref, w_in_ref, b_in_ref, w_pool_ref,
                  b_pool_ref, ls_ref, invc_ref, w_dw_ref, b_dw_ref, lncg_ref, lncb_ref, w_pw_ref, b_pw_ref,
                  w_out_ref, b_out_ref, lnpg_ref, lnpb_ref, out_ref,
                  h_s, proj_s, cat_s, u_s, pn_s, v_s, cn_s, c_s, zg_s, y_s):
    ts = SEQ_TILE
    d = xn_ref.shape[-1]
    d_in = w_in_ref.shape[-1]
    n_slab = u_s.shape[0]
    dp = n_slab * LANES
    g = pl.program_id(0)

    @pl.when(g == 0)
    def _():
        for ref in (h_s, proj_s, cat_s, u_s, v_s):
            ref[...] = jnp.zeros(ref.shape, ref.dtype)

    first = (jnp.maximum(g - 2, 0) % tiles_per_seq) == 0
    never = g < 0

    def norm_in(r, tok):
        x = _after(never, tok, xn_ref[0, r:r + LN_ROWS, :])
        scale1 = 1.0 + modn_ref[0, :, d:2 * d]
        h = _layer_norm(x) * _rows(scale1, LN_ROWS) + _rows(modn_ref[0, :, 0:d], LN_ROWS)
        h_s[r:r + LN_ROWS, :] = h.astype(BF16)

    def project(c0):
        cols = slice(c0, c0 + MXU_COLS)
        res = (jnp.dot(h_s[...], w_in_ref[:, cols], preferred_element_type=F32)
               + _rows(b_in_ref[:, cols], ts))
        proj_s[:, cols] = res
        return _token(res)

    def carry_rows(tok):
        del tok
        u_s[:, 0:HALO, :] = jnp.where(first, 0.0, u_s[:, ts:ts + HALO, :])
        v_s[:, 0:HALO, :] = jnp.where(first, 0.0, v_s[:, ts:ts + HALO, :])
        for grp in range(n_slab):
            u_s[grp, HALO:HALO + ts, :] = proj_s[:, grp * LANES:(grp + 1) * LANES]

    def glu(r, tok):
        a = _after(never, tok, proj_s[r:r + GATE_ROWS, 2 * dp:3 * dp])
        gt = proj_s[r:r + GATE_ROWS, 3 * dp:4 * dp]
        v = a * jax.nn.sigmoid(gt)
        for cb in range(n_slab):
            v_s[cb, HALO + r:HALO + r + GATE_ROWS, :] = v[:, cb * LANES:(cb + 1) * LANES]

    def z_gates(r, tok):
        rows = slice(r, r + GATE_ROWS)
        za = _after(never, tok, proj_s[rows, dp:2 * dp])
        zg_s[rows, 0:dp] = _silu(za) * _rows(ls_ref[...], GATE_ROWS)
        zg_s[rows, dp:] = _silu(proj_s[rows, 4 * dp:5 * dp])

    def pool(t0, tok):
        for grp, w in enumerate(POOL_WINDOWS):
            s = {q: _strided_rows(u_s, grp, HALO + t0 + q) for q in range(1 - w, ROW_STRIDE)}
            if grp == 0:
                s[1 - w] = _after(never, tok, s[1 - w])
            cur = s
            span = 1
            while span < w:
                s = {q: s[q] + s[q - span] for q in s if q - span in s}
                span *= 2
            for q in range(ROW_STRIDE):
                inv = jnp.where(first, invc_ref[grp, q], 1.0 / w) if t0 == 0 else 1.0 / w
                pn_s[grp, pl.ds(t0 + q, SUBLANES, stride=ROW_STRIDE), :] = s[q] * inv - cur[q]

    def pool_dot():
        p = jnp.concatenate([pn_s[grp] for grp in range(n_slab)], axis=1).astype(BF16)
        for j in range(dp // MXU_COLS):
            cols = slice(j * MXU_COLS, (j + 1) * MXU_COLS)
            res = (jnp.dot(p[:, cols], w_pool_ref[j], preferred_element_type=F32)
                   + _rows(b_pool_ref[:, cols], ts))
            y_s[:, cols] = res
        return _token(res)

    def conv(cb, t0, tok):
        lanes = slice(cb * LANES, (cb + 1) * LANES)
        acc = [b_dw_ref[:, lanes]] * ROW_STRIDE
        for qi in range(1 - CONV_WIDTH, ROW_STRIDE):
            vin = _strided_rows(v_s, cb, HALO + t0 + qi)
            if qi == 1 - CONV_WIDTH:
                vin = _after(never, tok, vin)
            for q in range(ROW_STRIDE):
                delay = q - qi
                if 0 <= delay < CONV_WIDTH:
                    acc[q] = acc[q] + vin * w_dw_ref[CONV_WIDTH - 1 - delay, :, lanes]
        for q in range(ROW_STRIDE):
            cn_s[cb, pl.ds(t0 + q, SUBLANES, stride=ROW_STRIDE), :] = acc[q]

    def conv_norm(r, tok):
        cv = jnp.concatenate([cn_s[cb, r:r + LN_ROWS, :] for cb in range(n_slab)], axis=1)
        cv = _after(never, tok, cv)
        vn = _layer_norm(cv) * _rows(lncg_ref[...], LN_ROWS) + _rows(lncb_ref[...], LN_ROWS)
        c_s[r:r + LN_ROWS, :] = _silu(vn).astype(BF16)

    def pointwise_dot():
        res = (jnp.dot(c_s[...], w_pw_ref[...], preferred_element_type=F32)
               + _rows(b_pw_ref[...], ts))
        y_s[:, dp:] = res
        return _token(res)

    def gates(r, tok):
        rows = slice(r, r + GATE_ROWS)
        y = _after(never, tok, y_s[rows, :]) * zg_s[rows, :]
        cat_s[rows, :] = y.astype(BF16)

    def out_dot(c0):
        cols = slice(c0, c0 + MXU_COLS)
        res = (jnp.dot(cat_s[...], w_out_ref[:, cols], preferred_element_type=F32)
               + _rows(b_out_ref[:, cols], ts))
        out_ref[0, :, cols] = res
        return _token(res)

    def norm_out(r, tok):
        rows = slice(r, r + LN_ROWS)
        gate = modo_ref[0, :, 2 * d:3 * d]
        xr = _after(never, tok, xo_ref[0, rows, :])
        t = DEEPNORM_ALPHA * xr + _rows(gate, LN_ROWS) * out_ref[0, rows, :]
        o = _layer_norm(t) * _rows(lnpg_ref[...], LN_ROWS) + _rows(lnpb_ref[...], LN_ROWS)
        out_ref[0, rows, :] = o

    P = functools.partial
    ln_chunks = range(0, ts, LN_ROWS)
    gate_chunks = range(0, ts, GATE_ROWS)
    blocks = range(0, ts, BLOCK_ROWS)
    proj_chunks = list(range(0, d_in, MXU_COLS))
    out_chunks = list(range(0, d, MXU_COLS))
    n_early = 7
    proj_free = f"z_gates{gate_chunks[-1]}"
    h_free = f"proj{proj_chunks[-1]}"

    mxu = ([_Piece(f"out_dot{c}", 260, P(out_dot, c), None) for c in out_chunks]
           + [_Piece("pool_dot", 130, pool_dot, f"pool{blocks[-1]}")]
           + [_Piece(f"proj{c}", 260, P(project, c), proj_free) for c in proj_chunks[:n_early]]
           + [_Piece("pw_dot", 260, pointwise_dot, f"conv_norm{ln_chunks[-1]}")]
           + [_Piece(f"proj{c}", 260, P(project, c), proj_free) for c in proj_chunks[n_early:]])
    vpu = ([_Piece("carry", 10, carry_rows, None)]
           + [_Piece(f"glu{r}", 40, P(glu, r), None) for r in gate_chunks]
           + [_Piece(f"z_gates{r}", 60, P(z_gates, r), None) for r in gate_chunks]
           + [_Piece(f"pool{t0}", 68, P(pool, t0), None) for t0 in blocks]
           + [_Piece(f"conv{cb}_{t0}", 62, P(conv, cb, t0), None) for cb in range(n_slab) for t0 in blocks]
           + [_Piece(f"conv_norm{r}", 25, P(conv_norm, r), None) for r in ln_chunks]
           + [_Piece(f"norm_out{r}", 44, P(norm_out, r), f"out_dot{out_chunks[-1]}") for r in ln_chunks]
           + [_Piece(f"gates{r}", 10, P(gates, r), "pw_dot") for r in gate_chunks]
           + [_Piece(f"norm_in{r}", 30, P(norm_in, r), h_free) for r in ln_chunks])
    _emit_interleaved(mxu, vpu)


def _tile8(v):
    return jnp.broadcast_to(v.astype(F32)[None, :], (SUBLANES, v.shape[0]))


def _inv_count_table():
    q = np.arange(ROW_STRIDE, dtype=np.float64)[:, None]
    s = np.arange(SUBLANES, dtype=np.float64)[None, :]
    t1 = q + ROW_STRIDE * s + 1.0
    tab = np.stack([1.0 / np.minimum(t1, float(w)) for w in POOL_WINDOWS])
    return jnp.asarray(np.broadcast_to(tab[..., None], tab.shape + (LANES,)), dtype=F32)


def _const_spec(shape):
    return pl.BlockSpec(shape, lambda g: (0,) * len(shape))


@jax.jit
def kernel(x, c, w_ada, b_ada, w_in, b_in, w_pool, b_pool, ls_pool, w_dw, b_dw, ln_conv_g, ln_conv_b,
           w_pw, b_pw, w_out, b_out, ln_post_g, ln_post_b):
    bsz, seq, d = x.shape
    n_grp, gw = w_pool.shape[0], w_pool.shape[1]
    dp = n_grp * gw
    dc = w_pw.shape[0]
    d_in = w_in.shape[1]
    assert seq % SEQ_TILE == 0 and SEQ_TILE % BLOCK_ROWS == 0
    assert gw == LANES and n_grp == len(POOL_WINDOWS) and dp == dc
    assert d_in == 2 * dp + 3 * dc and w_dw.shape[0] == CONV_WIDTH
    assert d_in % MXU_COLS == 0 and d % MXU_COLS == 0 and dp % MXU_COLS == 0
    assert HALO >= CONV_WIDTH - 1 and HALO >= max(POOL_WINDOWS) - 1
    n_slab = dp // LANES
    tiles_per_seq = seq // SEQ_TILE
    n_tiles = bsz * tiles_per_seq

    c8 = jnp.broadcast_to(c[:, None, :], (bsz, SUBLANES, d)).reshape(bsz * SUBLANES, d)
    mod = pl.pallas_call(
        _mod_kernel,
        grid=(3 * d // MOD_COLS,),
        in_specs=[pl.BlockSpec((bsz * SUBLANES, d), lambda j: (0, 0)),
                  pl.BlockSpec((d, MOD_COLS), lambda j: (0, j)),
                  pl.BlockSpec((1, MOD_COLS), lambda j: (0, j))],
        out_specs=pl.BlockSpec((bsz * SUBLANES, MOD_COLS), lambda j: (0, j)),
        out_shape=jax.ShapeDtypeStruct((bsz * SUBLANES, 3 * d), F32),
        name="adaln_mod",
    )(c8, w_ada, b_ada[None, :])
    mod = mod.reshape(bsz, SUBLANES, 3 * d)

    z = jnp.zeros((gw, gw), w_pool.dtype)
    w_pool2 = jnp.stack([
        jnp.block([[w_pool[0], z], [z, w_pool[1]]]),
        jnp.block([[w_pool[2], z], [z, w_pool[3]]]),
    ]).astype(BF16)
    w_dw8 = jnp.broadcast_to(w_dw.reshape(CONV_WIDTH, 1, dc), (CONV_WIDTH, SUBLANES, dc))

    params = (
        w_in.astype(BF16), _tile8(b_in), w_pool2, _tile8(b_pool.reshape(-1)), _tile8(ls_pool),
        _inv_count_table(), w_dw8, _tile8(b_dw), _tile8(ln_conv_g), _tile8(ln_conv_b),
        w_pw.astype(BF16), _tile8(b_pw), w_out.astype(BF16), _tile8(b_out),
        _tile8(ln_post_g), _tile8(ln_post_b),
    )

    def tile_n(g):
        return jnp.minimum(g, n_tiles - 1)

    def tile_o(g):
        return jnp.maximum(g - PIPE_DEPTH, 0)

    def x_spec(tile):
        return pl.BlockSpec((1, SEQ_TILE, d),
                            lambda g: (tile(g) // tiles_per_seq, tile(g) % tiles_per_seq, 0))

    def mod_spec(tile):
        return pl.BlockSpec((1, SUBLANES, 3 * d), lambda g: (tile(g) // tiles_per_seq, 0, 0))

    in_specs = ([x_spec(tile_n), x_spec(tile_o), mod_spec(tile_n), mod_spec(tile_o)]
                + [_const_spec(p.shape) for p in params])

    return pl.pallas_call(
        functools.partial(_layer_kernel, tiles_per_seq),
        grid=(n_tiles + PIPE_DEPTH,),
        in_specs=in_specs,
        out_specs=x_spec(tile_o),
        out_shape=jax.ShapeDtypeStruct((bsz, seq, d), x.dtype),
        scratch_shapes=[
            pltpu.VMEM((SEQ_TILE, d), BF16),
            pltpu.VMEM((SEQ_TILE, d_in), F32),
            pltpu.VMEM((SEQ_TILE, dp + dc), BF16),
            pltpu.VMEM((n_slab, HALO + SEQ_TILE, LANES), F32),
            pltpu.VMEM((n_slab, SEQ_TILE, LANES), F32),
            pltpu.VMEM((n_slab, HALO + SEQ_TILE, LANES), F32),
            pltpu.VMEM((n_slab, SEQ_TILE, LANES), F32),
            pltpu.VMEM((SEQ_TILE, dc), BF16),
            pltpu.VMEM((SEQ_TILE, dp + dc), F32),
            pltpu.VMEM((SEQ_TILE, dp + dc), F32),
        ],
        compiler_params=pltpu.CompilerParams(
            dimension_semantics=("arbitrary",),
            vmem_limit_bytes=VMEM_LIMIT_BYTES,
        ),
        name="hybrid_layer",
    )(x, x, mod, mod, *params)
```

```python
import collections
import functools

import numpy as np
import jax
import jax.numpy as jnp
from jax import lax
from jax.experimental import pallas as pl
from jax.experimental.pallas import tpu as pltpu

F32 = jnp.float32
BF16 = jnp.bfloat16

POOL_WINDOWS = (2, 4, 8, 16)
CONV_WIDTH = 31
LN_EPS = 1e-5
DEEPNORM_ALPHA = 2.0 ** 0.25

SUBLANES = 8
LANES = 128
MXU_COLS = 256
SEQ_TILE = 256
HALO = 32
ROW_STRIDE = 4
BLOCK_ROWS = SUBLANES * ROW_STRIDE
LN_ROWS = 16
GATE_ROWS = 32
MOD_COLS = 768
PIPE_DEPTH = 3
MXU_START = 1400.0
MXU_TOKEN_LAG = 2
VPU_TOKEN_LAG = 2
VMEM_LIMIT_BYTES = 48 * 1024 * 1024

_Piece = collections.namedtuple("_Piece", "name cost emit after")


def _emit_interleaved(mxu_pieces, vpu_pieces):
    done = set()
    t = {"m": MXU_START, "v": 0.0}
    queues = {"m": list(mxu_pieces), "v": list(vpu_pieces)}
    tokens = {"m": [], "v": []}
    lag = {"m": MXU_TOKEN_LAG, "v": VPU_TOKEN_LAG}

    def ready(k):
        return bool(queues[k]) and (queues[k][0].after is None or queues[k][0].after in done)

    while queues["m"] or queues["v"]:
        assert ready("m") or ready("v"), "piece order has a dependency cycle"
        k = "m" if ready("m") and (not ready("v") or t["m"] <= t["v"]) else "v"
        other = "v" if k == "m" else "m"
        piece = queues[k].pop(0)
        waits = tokens[other][-1 - lag[other]] if len(tokens[other]) > lag[other] else None
        tok = piece.emit(waits)
        if tok is not None:
            tokens[k].append(tok)
        done.add(piece.name)
        t[k] += piece.cost


def _after(never, token, val):
    if token is None:
        return val
    rows = SUBLANES * (4 // val.dtype.itemsize)
    if rows != SUBLANES:
        token = jnp.concatenate([token] * (rows // SUBLANES), axis=0).astype(val.dtype)
    first = jnp.where(never, token, val[0:rows, 0:LANES])
    top = first if val.shape[1] == LANES else jnp.concatenate([first, val[0:rows, LANES:]], axis=1)
    return top if val.shape[0] == rows else jnp.concatenate([top, val[rows:, :]], axis=0)


def _token(res):
    return res[-SUBLANES:, -LANES:]


def _rows(p8, n):
    return p8 if n == SUBLANES else jnp.concatenate([p8] * (n // SUBLANES), axis=0)


def _silu(z):
    return z * jax.nn.sigmoid(z)


def _layer_norm(t):
    mu = jnp.mean(t, axis=-1, keepdims=True)
    tc = t - mu
    var = jnp.mean(tc * tc, axis=-1, keepdims=True)
    return tc * lax.rsqrt(var + LN_EPS)


def _strided_rows(ref, slab, start):
    return ref[slab, pl.ds(start, SUBLANES, stride=ROW_STRIDE), :]


def _mod_kernel(c_ref, w_ref, b_ref, o_ref):
    c = c_ref[...]
    s = _silu(c).astype(BF16)
    o_ref[...] = jnp.dot(s, w_ref[...].astype(BF16), preferred_element_type=F32) + b_ref[...]


def _layer_kernel(tiles_per_seq, xn_ref, xo_ref, modn_ref, modo_ref, w_in_ref, b_in_ref, w_pool_ref,
                  b_pool_ref, ls_ref, invc_ref, w_dw_ref, b_dw_ref, lncg_ref, lncb_ref, w_pw_ref, b_pw_ref,
                  w_out_ref, b_out_ref, lnpg_ref, lnpb_ref, out_ref,
                  h_s, proj_s, cat_s, u_s, pn_s, v_s, cn_s, c_s, zg_s, y_s):
    ts = SEQ_TILE
    d = xn_ref.shape[-1]
    d_in = w_in_ref.shape[-1]
    n_slab = u_s.shape[0]
    dp = n_slab * LANES
    g = pl.program_id(0)

    @pl.when(g == 0)
    def _():
        for ref in (h_s, proj_s, cat_s, u_s, v_s):
            ref[...] = jnp.zeros(ref.shape, ref.dtype)

    first = (jnp.maximum(g - 2, 0) % tiles_per_seq) == 0
    never = g < 0

    def norm_in(r, tok):
        x = _after(never, tok, xn_ref[0, r:r + LN_ROWS, :])
        scale1 = 1.0 + modn_ref[0, :, d:2 * d]
        h = _layer_norm(x) * _rows(scale1, LN_ROWS) + _rows(modn_ref[0, :, 0:d], LN_ROWS)
        h_s[r:r + LN_ROWS, :] = h.astype(BF16)
        return _token(h)

    def project(c0, tok):
        cols = slice(c0, c0 + MXU_COLS)
        res = (jnp.dot(h_s[...], _after(never, tok, w_in_ref[:, cols]), preferred_element_type=F32)
               + _rows(b_in_ref[:, cols], ts))
        proj_s[:, cols] = res
        return _token(res)

    def carry_rows(tok):
        del tok
        u_s[:, 0:HALO, :] = jnp.where(first, 0.0, u_s[:, ts:ts + HALO, :])
        v_s[:, 0:HALO, :] = jnp.where(first, 0.0, v_s[:, ts:ts + HALO, :])
        for grp in range(n_slab):
            u_s[grp, HALO:HALO + ts, :] = proj_s[:, grp * LANES:(grp + 1) * LANES]

    def glu(r, tok):
        a = _after(never, tok, proj_s[r:r + GATE_ROWS, 2 * dp:3 * dp])
        gt = proj_s[r:r + GATE_ROWS, 3 * dp:4 * dp]
        v = a * jax.nn.sigmoid(gt)
        for cb in range(n_slab):
            v_s[cb, HALO + r:HALO + r + GATE_ROWS, :] = v[:, cb * LANES:(cb + 1) * LANES]
        return _token(v)

    def z_gates(r, tok):
        rows = slice(r, r + GATE_ROWS)
        za = _after(never, tok, proj_s[rows, dp:2 * dp])
        zg_s[rows, 0:dp] = _silu(za) * _rows(ls_ref[...], GATE_ROWS)
        zb = _silu(proj_s[rows, 4 * dp:5 * dp])
        zg_s[rows, dp:] = zb
        return _token(zb)

    def pool(t0, tok):
        for grp, w in enumerate(POOL_WINDOWS):
            s = {q: _strided_rows(u_s, grp, HALO + t0 + q) for q in range(1 - w, ROW_STRIDE)}
            if grp == 0:
                s[1 - w] = _after(never, tok, s[1 - w])
            cur = s
            span = 1
            while span < w:
                s = {q: s[q] + s[q - span] for q in s if q - span in s}
                span *= 2
            for q in range(ROW_STRIDE):
                inv = jnp.where(first, invc_ref[grp, q], 1.0 / w) if t0 == 0 else 1.0 / w
                pooled = s[q] * inv - cur[q]
                pn_s[grp, pl.ds(t0 + q, SUBLANES, stride=ROW_STRIDE), :] = pooled
        return pooled

    def pool_dot(tok):
        p = jnp.concatenate([pn_s[grp] for grp in range(n_slab)], axis=1).astype(BF16)
        p = _after(never, tok, p)
        for j in range(dp // MXU_COLS):
            cols = slice(j * MXU_COLS, (j + 1) * MXU_COLS)
            res = (jnp.dot(p[:, cols], w_pool_ref[j], preferred_element_type=F32)
                   + _rows(b_pool_ref[:, cols], ts))
            y_s[:, cols] = res
        return _token(res)

    def conv(cb, t0, tok):
        lanes = slice(cb * LANES, (cb + 1) * LANES)
        acc = [b_dw_ref[:, lanes]] * ROW_STRIDE
        for qi in range(1 - CONV_WIDTH, ROW_STRIDE):
            vin = _strided_rows(v_s, cb, HALO + t0 + qi)
            if qi == 1 - CONV_WIDTH:
                vin = _after(never, tok, vin)
            for q in range(ROW_STRIDE):
                delay = q - qi
                if 0 <= delay < CONV_WIDTH:
                    acc[q] = acc[q] + vin * w_dw_ref[CONV_WIDTH - 1 - delay, :, lanes]
        for q in range(ROW_STRIDE):
            cn_s[cb, pl.ds(t0 + q, SUBLANES, stride=ROW_STRIDE), :] = acc[q]
        return acc[-1]

    def conv_norm(r, tok):
        cv = jnp.concatenate([cn_s[cb, r:r + LN_ROWS, :] for cb in range(n_slab)], axis=1)
        cv = _after(never, tok, cv)
        vn = _layer_norm(cv) * _rows(lncg_ref[...], LN_ROWS) + _rows(lncb_ref[...], LN_ROWS)
        act = _silu(vn)
        c_s[r:r + LN_ROWS, :] = act.astype(BF16)
        return _token(act)

    def pointwise_dot(tok):
        res = (jnp.dot(c_s[...], _after(never, tok, w_pw_ref[...]), preferred_element_type=F32)
               + _rows(b_pw_ref[...], ts))
        y_s[:, dp:] = res
        return _token(res)

    def gates(r, tok):
        rows = slice(r, r + GATE_ROWS)
        y = _after(never, tok, y_s[rows, :]) * zg_s[rows, :]
        cat_s[rows, :] = y.astype(BF16)
        return _token(y)

    def out_dot(c0, tok):
        cols = slice(c0, c0 + MXU_COLS)
        res = (jnp.dot(cat_s[...], _after(never, tok, w_out_ref[:, cols]), preferred_element_type=F32)
               + _rows(b_out_ref[:, cols], ts))
        out_ref[0, :, cols] = res
        return _token(res)

    def norm_out(r, tok):
        rows = slice(r, r + LN_ROWS)
        gate = modo_ref[0, :, 2 * d:3 * d]
        xr = _after(never, tok, xo_ref[0, rows, :])
        t = DEEPNORM_ALPHA * xr + _rows(gate, LN_ROWS) * out_ref[0, rows, :]
        o = _layer_norm(t) * _rows(lnpg_ref[...], LN_ROWS) + _rows(lnpb_ref[...], LN_ROWS)
        out_ref[0, rows, :] = o
        return _token(o)

    P = functools.partial
    ln_chunks = range(0, ts, LN_ROWS)
    gate_chunks = range(0, ts, GATE_ROWS)
    blocks = range(0, ts, BLOCK_ROWS)
    proj_chunks = list(range(0, d_in, MXU_COLS))
    out_chunks = list(range(0, d, MXU_COLS))
    n_early = 10
    proj_free = f"z_gates{gate_chunks[-1]}"
    h_free = f"proj{proj_chunks[-1]}"

    mxu = ([_Piece(f"out_dot{c}", 260, P(out_dot, c), None) for c in out_chunks]
           + [_Piece("pool_dot", 130, pool_dot, f"pool{blocks[-1]}")]
           + [_Piece(f"proj{c}", 260, P(project, c), proj_free) for c in proj_chunks[:n_early]]
           + [_Piece("pw_dot", 260, pointwise_dot, f"conv_norm{ln_chunks[-1]}")]
           + [_Piece(f"proj{c}", 260, P(project, c), proj_free) for c in proj_chunks[n_early:]])
    vpu = ([_Piece("carry", 10, carry_rows, None)]
           + [_Piece(f"glu{r}", 40, P(glu, r), None) for r in gate_chunks]
           + [_Piece(f"z_gates{r}", 60, P(z_gates, r), None) for r in gate_chunks]
           + [_Piece(f"pool{t0}", 68, P(pool, t0), None) for t0 in blocks]
           + [_Piece(f"conv{cb}_{t0}", 62, P(conv, cb, t0), None) for cb in range(n_slab) for t0 in blocks]
           + [_Piece(f"conv_norm{r}", 25, P(conv_norm, r), None) for r in ln_chunks]
           + [_Piece(f"norm_out{r}", 44, P(norm_out, r), f"out_dot{out_chunks[-1]}") for r in ln_chunks]
           + [_Piece(f"gates{r}", 10, P(gates, r), "pw_dot") for r in gate_chunks]
           + [_Piece(f"norm_in{r}", 30, P(norm_in, r), h_free) for r in ln_chunks])
    _emit_interleaved(mxu, vpu)


def _tile8(v):
    return jnp.broadcast_to(v.astype(F32)[None, :], (SUBLANES, v.shape[0]))


def _inv_count_table():
    q = np.arange(ROW_STRIDE, dtype=np.float64)[:, None]
    s = np.arange(SUBLANES, dtype=np.float64)[None, :]
    t1 = q + ROW_STRIDE * s + 1.0
    tab = np.stack([1.0 / np.minimum(t1, float(w)) for w in POOL_WINDOWS])
    return jnp.asarray(np.broadcast_to(tab[..., None], tab.shape + (LANES,)), dtype=F32)


def _const_spec(shape):
    return pl.BlockSpec(shape, lambda g: (0,) * len(shape))


@jax.jit
def kernel(x, c, w_ada, b_ada, w_in, b_in, w_pool, b_pool, ls_pool, w_dw, b_dw, ln_conv_g, ln_conv_b,
           w_pw, b_pw, w_out, b_out, ln_post_g, ln_post_b):
    bsz, seq, d = x.shape
    n_grp, gw = w_pool.shape[0], w_pool.shape[1]
    dp = n_grp * gw
    dc = w_pw.shape[0]
    d_in = w_in.shape[1]
    assert seq % SEQ_TILE == 0 and SEQ_TILE % BLOCK_ROWS == 0
    assert gw == LANES and n_grp == len(POOL_WINDOWS) and dp == dc
    assert d_in == 2 * dp + 3 * dc and w_dw.shape[0] == CONV_WIDTH
    assert d_in % MXU_COLS == 0 and d % MXU_COLS == 0 and dp % MXU_COLS == 0
    assert HALO >= CONV_WIDTH - 1 and HALO >= max(POOL_WINDOWS) - 1
    n_slab = dp // LANES
    tiles_per_seq = seq // SEQ_TILE
    n_tiles = bsz * tiles_per_seq

    c8 = jnp.broadcast_to(c[:, None, :], (bsz, SUBLANES, d)).reshape(bsz * SUBLANES, d)
    mod = pl.pallas_call(
        _mod_kernel,
        grid=(3 * d // MOD_COLS,),
        in_specs=[pl.BlockSpec((bsz * SUBLANES, d), lambda j: (0, 0)),
                  pl.BlockSpec((d, MOD_COLS), lambda j: (0, j)),
                  pl.BlockSpec((1, MOD_COLS), lambda j: (0, j))],
        out_specs=pl.BlockSpec((bsz * SUBLANES, MOD_COLS), lambda j: (0, j)),
        out_shape=jax.ShapeDtypeStruct((bsz * SUBLANES, 3 * d), F32),
        name="adaln_mod",
    )(c8, w_ada, b_ada[None, :])
    mod = mod.reshape(bsz, SUBLANES, 3 * d)

    z = jnp.zeros((gw, gw), w_pool.dtype)
    w_pool2 = jnp.stack([
        jnp.block([[w_pool[0], z], [z, w_pool[1]]]),
        jnp.block([[w_pool[2], z], [z, w_pool[3]]]),
    ]).astype(BF16)
    w_dw8 = jnp.broadcast_to(w_dw.reshape(CONV_WIDTH, 1, dc), (CONV_WIDTH, SUBLANES, dc))

    params = (
        w_in.astype(BF16), _tile8(b_in), w_pool2, _tile8(b_pool.reshape(-1)), _tile8(ls_pool),
        _inv_count_table(), w_dw8, _tile8(b_dw), _tile8(ln_conv_g), _tile8(ln_conv_b),
        w_pw.astype(BF16), _tile8(b_pw), w_out.astype(BF16), _tile8(b_out),
        _tile8(ln_post_g), _tile8(ln_post_b),
    )

    def tile_n(g):
        return jnp.minimum(g, n_tiles - 1)

    def tile_o(g):
        return jnp.maximum(g - PIPE_DEPTH, 0)

    def x_spec(tile):
        return pl.BlockSpec((1, SEQ_TILE, d),
                            lambda g: (tile(g) // tiles_per_seq, tile(g) % tiles_per_seq, 0))

    def mod_spec(tile):
        return pl.BlockSpec((1, SUBLANES, 3 * d), lambda g: (tile(g) // tiles_per_seq, 0, 0))

    in_specs = ([x_spec(tile_n), x_spec(tile_o), mod_spec(tile_n), mod_spec(tile_o)]
                + [_const_spec(p.shape) for p in params])

    return pl.pallas_call(
        functools.partial(_layer_kernel, tiles_per_seq),
        grid=(n_tiles + PIPE_DEPTH,),
        in_specs=in_specs,
        out_specs=x_spec(tile_o),
        out_shape=jax.ShapeDtypeStruct((bsz, seq, d), x.dtype),
        scratch_shapes=[
            pltpu.VMEM((SEQ_TILE, d), BF16),
            pltpu.VMEM((SEQ_TILE, d_in), F32),
            pltpu.VMEM((SEQ_TILE, dp + dc), BF16),
            pltpu.VMEM((n_slab, HALO + SEQ_TILE, LANES), F32),
            pltpu.VMEM((n_slab, SEQ_TILE, LANES), F32),
            pltpu.VMEM((n_slab, HALO + SEQ_TILE, LANES), F32),
            pltpu.VMEM((n_slab, SEQ_TILE, LANES), F32),
            pltpu.VMEM((SEQ_TILE, dc), BF16),
            pltpu.VMEM((SEQ_TILE, dp + dc), F32),
            pltpu.VMEM((SEQ_TILE, dp + dc), F32),
        ],
        compiler_params=pltpu.CompilerParams(
            dimension_semantics=("arbitrary",),
            vmem_limit_bytes=VMEM_LIMIT_BYTES,
        ),
        name="hybrid_layer",
    )(x, x, mod, mod, *params)
```

```python
import functools

import numpy as np
import jax
import jax.numpy as jnp
from jax import lax
from jax.experimental import pallas as pl
from jax.experimental.pallas import tpu as pltpu

F32 = jnp.float32
BF16 = jnp.bfloat16

POOL_WINDOWS = (2, 4, 8, 16)
CONV_WIDTH = 31
LN_EPS = 1e-5
DEEPNORM_ALPHA = 2.0 ** 0.25

SUBLANES = 8
LANES = 128
SEQ_TILE = 256
TILES_PER_STEP = 4
HALO = 32
ROW_STRIDE = 4
BLOCK_ROWS = SUBLANES * ROW_STRIDE
LN_ROWS = 16
GATE_ROWS = 32
MOD_COLS = 768
VMEM_LIMIT_BYTES = 48 * 1024 * 1024

_VEC_NAMES = ("b_in", "b_pool", "ls", "b_dw", "lncg", "lncb", "b_pw", "b_out", "lnpg", "lnpb")


def _rows(p8, n):
    return p8 if n == SUBLANES else jnp.concatenate([p8] * (n // SUBLANES), axis=0)


def _silu(z):
    return z * jax.nn.sigmoid(z)


def _layer_norm(t, eps=LN_EPS):
    mu = jnp.mean(t, axis=-1, keepdims=True)
    tc = t - mu
    var = jnp.mean(tc * tc, axis=-1, keepdims=True)
    return tc * lax.rsqrt(var + eps)


def _strided_rows(ref, slab, start):
    return ref[slab, pl.ds(start, SUBLANES, stride=ROW_STRIDE), :]


def _mod_kernel(c_ref, w_ref, b_ref, o_ref):
    c = c_ref[...]
    s = _silu(c).astype(BF16)
    o_ref[...] = jnp.dot(s, w_ref[...].astype(BF16), preferred_element_type=F32) + b_ref[...]


def _layer_kernel(vec_at, x_ref, mod_ref, vec_ref, w_in_ref, w_pool_ref, invc_ref, w_dw_ref, w_pw_ref,
                  w_out_ref, out_ref, h_s, u_s, proj_s, pn_s, v_s, cn_s, c_s, cat_s, y_s, yo_s):
    ts = SEQ_TILE
    d = x_ref.shape[-1]
    n_slab = u_s.shape[0]
    dp = n_slab * LANES

    def vec(name, lo=0, hi=None):
        off, size = vec_at[name]
        return vec_ref[:, off + lo:off + (size if hi is None else hi)]

    @pl.when(jnp.logical_and(pl.program_id(0) == 0, pl.program_id(1) == 0))
    def _():
        u_s[:, ts:ts + HALO, :] = jnp.zeros((n_slab, HALO, LANES), F32)
        v_s[:, ts:ts + HALO, :] = jnp.zeros((n_slab, HALO, LANES), F32)

    shift = mod_ref[0, :, 0:d]
    scale1 = 1.0 + mod_ref[0, :, d:2 * d]
    gate = mod_ref[0, :, 2 * d:3 * d] * (1.0 / DEEPNORM_ALPHA)
    eps_post = LN_EPS / (DEEPNORM_ALPHA * DEEPNORM_ALPHA)

    def tile(j, carry):
        r0 = pl.multiple_of(j * ts, ts)
        first = jnp.logical_and(pl.program_id(1) == 0, j == 0)

        def x_rows(r, n):
            return x_ref[0, pl.ds(r0 + r, n), :]

        u_s[:, 0:HALO, :] = jnp.where(first, 0.0, u_s[:, ts:ts + HALO, :])
        v_s[:, 0:HALO, :] = jnp.where(first, 0.0, v_s[:, ts:ts + HALO, :])

        for r in range(0, ts, LN_ROWS):
            h = _layer_norm(x_rows(r, LN_ROWS)) * _rows(scale1, LN_ROWS) + _rows(shift, LN_ROWS)
            h_s[r:r + LN_ROWS, :] = h.astype(BF16)

        h = h_s[...]
        u = jnp.dot(h, w_in_ref[:, 0:dp], preferred_element_type=F32) + _rows(vec("b_in", 0, dp), ts)
        for g in range(n_slab):
            u_s[g, HALO:HALO + ts, :] = u[:, g * LANES:(g + 1) * LANES]
        proj_s[...] = (jnp.dot(h, w_in_ref[:, dp:], preferred_element_type=F32)
                       + _rows(vec("b_in", dp), ts))

        for t0 in range(0, ts, BLOCK_ROWS):
            for g, w in enumerate(POOL_WINDOWS):
                s = {q: _strided_rows(u_s, g, HALO + t0 + q) for q in range(1 - w, ROW_STRIDE)}
                cur = s
                span = 1
                while span < w:
                    s = {q: s[q] + s[q - span] for q in s if q - span in s}
                    span *= 2
                for q in range(ROW_STRIDE):
                    inv = jnp.where(first, invc_ref[g, q], 1.0 / w) if t0 == 0 else 1.0 / w
                    pn_s[g, pl.ds(t0 + q, SUBLANES, stride=ROW_STRIDE), :] = s[q] * inv - cur[q]

        p = jnp.concatenate([pn_s[g] for g in range(n_slab)], axis=1).astype(BF16)
        half = dp // 2
        for k in range(2):
            cols = slice(k * half, (k + 1) * half)
            y_s[:, cols] = (jnp.dot(p[:, cols], w_pool_ref[k], preferred_element_type=F32)
                            + _rows(vec("b_pool", k * half, (k + 1) * half), ts))

        for r in range(0, ts, GATE_ROWS):
            a = proj_s[r:r + GATE_ROWS, dp:2 * dp]
            gt = proj_s[r:r + GATE_ROWS, 2 * dp:3 * dp]
            v = a * jax.nn.sigmoid(gt)
            for cb in range(n_slab):
                v_s[cb, HALO + r:HALO + r + GATE_ROWS, :] = v[:, cb * LANES:(cb + 1) * LANES]

        for cb in range(n_slab):
            lanes = slice(cb * LANES, (cb + 1) * LANES)
            taps = [w_dw_ref[k, :, lanes] for k in range(CONV_WIDTH)]
            bias = vec("b_dw", cb * LANES, (cb + 1) * LANES)
            for t0 in range(0, ts, BLOCK_ROWS):
                acc = [bias] * ROW_STRIDE
                for qi in range(1 - CONV_WIDTH, ROW_STRIDE):
                    vin = _strided_rows(v_s, cb, HALO + t0 + qi)
                    for q in range(ROW_STRIDE):
                        delay = q - qi
                        if 0 <= delay < CONV_WIDTH:
                            acc[q] = acc[q] + vin * taps[CONV_WIDTH - 1 - delay]
                for q in range(ROW_STRIDE):
                    cn_s[cb, pl.ds(t0 + q, SUBLANES, stride=ROW_STRIDE), :] = acc[q]

        for r in range(0, ts, LN_ROWS):
            cv = jnp.concatenate([cn_s[cb, r:r + LN_ROWS, :] for cb in range(n_slab)], axis=1)
            vn = _layer_norm(cv) * _rows(vec("lncg"), LN_ROWS) + _rows(vec("lncb"), LN_ROWS)
            c_s[r:r + LN_ROWS, :] = _silu(vn).astype(BF16)

        y_s[:, dp:] = (jnp.dot(c_s[...], w_pw_ref[...], preferred_element_type=F32)
                       + _rows(vec("b_pw"), ts))

        for r in range(0, ts, GATE_ROWS):
            rows = slice(r, r + GATE_ROWS)
            ya = y_s[rows, 0:dp] * _rows(vec("ls"), GATE_ROWS) * _silu(proj_s[rows, 0:dp])
            cat_s[rows, 0:dp] = ya.astype(BF16)
            yb = y_s[rows, dp:] * _silu(proj_s[rows, 3 * dp:4 * dp])
            cat_s[rows, dp:] = yb.astype(BF16)

        yo_s[...] = (jnp.dot(cat_s[...], w_out_ref[...], preferred_element_type=F32)
                     + _rows(vec("b_out"), ts))
        for r in range(0, ts, LN_ROWS):
            t = x_rows(r, LN_ROWS) + _rows(gate, LN_ROWS) * yo_s[r:r + LN_ROWS, :]
            o = _layer_norm(t, eps_post) * _rows(vec("lnpg"), LN_ROWS) + _rows(vec("lnpb"), LN_ROWS)
            out_ref[0, pl.ds(r0 + r, LN_ROWS), :] = o
        return carry

    lax.fori_loop(0, TILES_PER_STEP, tile, 0)


def _inv_count_table():
    q = np.arange(ROW_STRIDE, dtype=np.float64)[:, None]
    s = np.arange(SUBLANES, dtype=np.float64)[None, :]
    t1 = q + ROW_STRIDE * s + 1.0
    tab = np.stack([1.0 / np.minimum(t1, float(w)) for w in POOL_WINDOWS])
    return jnp.asarray(np.broadcast_to(tab[..., None], tab.shape + (LANES,)), dtype=F32)


def _const_spec(shape):
    return pl.BlockSpec(shape, lambda b, i: (0,) * len(shape))


@jax.jit
def kernel(x, c, w_ada, b_ada, w_in, b_in, w_pool, b_pool, ls_pool, w_dw, b_dw, ln_conv_g, ln_conv_b,
           w_pw, b_pw, w_out, b_out, ln_post_g, ln_post_b):
    bsz, seq, d = x.shape
    n_grp, gw = w_pool.shape[0], w_pool.shape[1]
    dp = n_grp * gw
    dc = w_pw.shape[0]
    d_in = w_in.shape[1]
    step_rows = SEQ_TILE * TILES_PER_STEP
    assert seq % step_rows == 0 and SEQ_TILE % BLOCK_ROWS == 0
    assert gw == LANES and n_grp == len(POOL_WINDOWS) and dp == dc
    assert d_in == 2 * dp + 3 * dc and w_dw.shape[0] == CONV_WIDTH
    assert HALO >= CONV_WIDTH - 1 and HALO >= max(POOL_WINDOWS) - 1
    n_slab = dp // LANES

    c8 = jnp.broadcast_to(c[:, None, :], (bsz, SUBLANES, d)).reshape(bsz * SUBLANES, d)
    mod = pl.pallas_call(
        _mod_kernel,
        grid=(3 * d // MOD_COLS,),
        in_specs=[pl.BlockSpec((bsz * SUBLANES, d), lambda j: (0, 0)),
                  pl.BlockSpec((d, MOD_COLS), lambda j: (0, j)),
                  pl.BlockSpec((1, MOD_COLS), lambda j: (0, j))],
        out_specs=pl.BlockSpec((bsz * SUBLANES, MOD_COLS), lambda j: (0, j)),
        out_shape=jax.ShapeDtypeStruct((bsz * SUBLANES, 3 * d), F32),
        name="adaln_mod",
    )(c8, w_ada, b_ada[None, :])
    mod = mod.reshape(bsz, SUBLANES, 3 * d)

    z = jnp.zeros((gw, gw), w_pool.dtype)
    w_pool2 = jnp.stack([
        jnp.block([[w_pool[0], z], [z, w_pool[1]]]),
        jnp.block([[w_pool[2], z], [z, w_pool[3]]]),
    ]).astype(BF16)
    w_dw8 = jnp.broadcast_to(w_dw.reshape(CONV_WIDTH, 1, dc), (CONV_WIDTH, SUBLANES, dc))

    vectors = (b_in, b_pool.reshape(-1), ls_pool, b_dw, ln_conv_g, ln_conv_b, b_pw, b_out,
               ln_post_g, ln_post_b)
    vec_at, off = {}, 0
    for name, v in zip(_VEC_NAMES, vectors):
        assert v.shape[0] % LANES == 0
        vec_at[name] = (off, v.shape[0])
        off += v.shape[0]
    vec8 = jnp.broadcast_to(jnp.concatenate(vectors).astype(F32)[None, :], (SUBLANES, off))

    operands = (x, mod, vec8, w_in.astype(BF16), w_pool2, _inv_count_table(), w_dw8,
                w_pw.astype(BF16), w_out.astype(BF16))
    in_specs = [
        pl.BlockSpec((1, step_rows, d), lambda b, i: (b, i, 0)),
        pl.BlockSpec((1, SUBLANES, 3 * d), lambda b, i: (b, 0, 0)),
    ] + [_const_spec(op.shape) for op in operands[2:]]

    return pl.pallas_call(
        functools.partial(_layer_kernel, vec_at),
        grid=(bsz, seq // step_rows),
        in_specs=in_specs,
        out_specs=pl.BlockSpec((1, step_rows, d), lambda b, i: (b, i, 0)),
        out_shape=jax.ShapeDtypeStruct((bsz, seq, d), x.dtype),
        scratch_shapes=[
            pltpu.VMEM((SEQ_TILE, d), BF16),
            pltpu.VMEM((n_slab, HALO + SEQ_TILE, LANES), F32),
            pltpu.VMEM((SEQ_TILE, dp + 3 * dc), F32),
            pltpu.VMEM((n_slab, SEQ_TILE, LANES), F32),
            pltpu.VMEM((n_slab, HALO + SEQ_TILE, LANES), F32),
            pltpu.VMEM((n_slab, SEQ_TILE, LANES), F32),
            pltpu.VMEM((SEQ_TILE, dc), BF16),
            pltpu.VMEM((SEQ_TILE, dp + dc), BF16),
            pltpu.VMEM((SEQ_TILE, dp + dc), F32),
            pltpu.VMEM((SEQ_TILE, d), F32),
        ],
        compiler_params=pltpu.CompilerParams(
            dimension_semantics=("arbitrary", "arbitrary"),
            vmem_limit_bytes=VMEM_LIMIT_BYTES,
        ),
        name="hybrid_layer",
    )(*operands)
```

```python
import functools

import numpy as np
import jax
import jax.numpy as jnp
from jax import lax
from jax.experimental import pallas as pl
from jax.experimental.pallas import tpu as pltpu

F32 = jnp.float32
BF16 = jnp.bfloat16

POOL_WINDOWS = (2, 4, 8, 16)
CONV_WIDTH = 31
LN_EPS = 1e-5
DEEPNORM_ALPHA = 2.0 ** 0.25

SUBLANES = 8
LANES = 128
SEQ_TILE = 512
TILES_PER_STEP = 2
HALO = 32
ROW_STRIDE = 4
BLOCK_ROWS = SUBLANES * ROW_STRIDE
LN_ROWS = 16
GATE_ROWS = 32
MOD_COLS = 768
VMEM_LIMIT_BYTES = 48 * 1024 * 1024

_VEC_NAMES = ("b_in", "b_pool", "ls", "b_dw", "lncg", "lncb", "b_pw", "b_out", "lnpg", "lnpb")


def _rows(p8, n):
    return p8 if n == SUBLANES else jnp.concatenate([p8] * (n // SUBLANES), axis=0)


def _silu(z):
    return z * jax.nn.sigmoid(z)


def _layer_norm(t, eps=LN_EPS):
    mu = jnp.mean(t, axis=-1, keepdims=True)
    tc = t - mu
    var = jnp.mean(tc * tc, axis=-1, keepdims=True)
    return tc * lax.rsqrt(var + eps)


def _strided_rows(ref, slab, start):
    return ref[slab, pl.ds(start, SUBLANES, stride=ROW_STRIDE), :]


def _mod_kernel(c_ref, w_ref, b_ref, o_ref):
    c = c_ref[...]
    s = _silu(c).astype(BF16)
    o_ref[...] = jnp.dot(s, w_ref[...].astype(BF16), preferred_element_type=F32) + b_ref[...]


def _layer_kernel(vec_at, x_ref, mod_ref, vec_ref, w_in_ref, w_pool_ref, invc_ref, w_dw_ref, w_pw_ref,
                  w_out_ref, out_ref, h_s, u_s, proj_s, pn_s, v_s, cn_s, c_s, cat_s, y_s, yo_s):
    ts = SEQ_TILE
    d = x_ref.shape[-1]
    n_slab = u_s.shape[0]
    dp = n_slab * LANES

    def vec(name, lo=0, hi=None):
        off, size = vec_at[name]
        return vec_ref[:, off + lo:off + (size if hi is None else hi)]

    @pl.when(jnp.logical_and(pl.program_id(0) == 0, pl.program_id(1) == 0))
    def _():
        u_s[:, ts:ts + HALO, :] = jnp.zeros((n_slab, HALO, LANES), F32)
        v_s[:, ts:ts + HALO, :] = jnp.zeros((n_slab, HALO, LANES), F32)

    shift = mod_ref[0, :, 0:d]
    scale1 = 1.0 + mod_ref[0, :, d:2 * d]
    gate = mod_ref[0, :, 2 * d:3 * d] * (1.0 / DEEPNORM_ALPHA)
    eps_post = LN_EPS / (DEEPNORM_ALPHA * DEEPNORM_ALPHA)

    def tile(j, carry):
        r0 = pl.multiple_of(j * ts, ts)
        first = jnp.logical_and(pl.program_id(1) == 0, j == 0)

        def x_rows(r, n):
            return x_ref[0, pl.ds(r0 + r, n), :]

        u_s[:, 0:HALO, :] = jnp.where(first, 0.0, u_s[:, ts:ts + HALO, :])
        v_s[:, 0:HALO, :] = jnp.where(first, 0.0, v_s[:, ts:ts + HALO, :])

        for r in range(0, ts, LN_ROWS):
            h = _layer_norm(x_rows(r, LN_ROWS)) * _rows(scale1, LN_ROWS) + _rows(shift, LN_ROWS)
            h_s[r:r + LN_ROWS, :] = h.astype(BF16)

        h = h_s[...]
        u = jnp.dot(h, w_in_ref[:, 0:dp], preferred_element_type=F32) + _rows(vec("b_in", 0, dp), ts)
        for g in range(n_slab):
            u_s[g, HALO:HALO + ts, :] = u[:, g * LANES:(g + 1) * LANES]
        proj_s[...] = (jnp.dot(h, w_in_ref[:, dp:], preferred_element_type=F32)
                       + _rows(vec("b_in", dp), ts))

        for t0 in range(0, ts, BLOCK_ROWS):
            for g, w in enumerate(POOL_WINDOWS):
                s = {q: _strided_rows(u_s, g, HALO + t0 + q) for q in range(1 - w, ROW_STRIDE)}
                cur = s
                span = 1
                while span < w:
                    s = {q: s[q] + s[q - span] for q in s if q - span in s}
                    span *= 2
                for q in range(ROW_STRIDE):
                    inv = jnp.where(first, invc_ref[g, q], 1.0 / w) if t0 == 0 else 1.0 / w
                    pn_s[g, pl.ds(t0 + q, SUBLANES, stride=ROW_STRIDE), :] = s[q] * inv - cur[q]

        p = jnp.concatenate([pn_s[g] for g in range(n_slab)], axis=1).astype(BF16)
        half = dp // 2
        for k in range(2):
            cols = slice(k * half, (k + 1) * half)
            y_s[:, cols] = (jnp.dot(p[:, cols], w_pool_ref[k], preferred_element_type=F32)
                            + _rows(vec("b_pool", k * half, (k + 1) * half), ts))

        for r in range(0, ts, GATE_ROWS):
            a = proj_s[r:r + GATE_ROWS, dp:2 * dp]
            gt = proj_s[r:r + GATE_ROWS, 2 * dp:3 * dp]
            v = a * jax.nn.sigmoid(gt)
            for cb in range(n_slab):
                v_s[cb, HALO + r:HALO + r + GATE_ROWS, :] = v[:, cb * LANES:(cb + 1) * LANES]

        for cb in range(n_slab):
            lanes = slice(cb * LANES, (cb + 1) * LANES)
            taps = [w_dw_ref[k, :, lanes] for k in range(CONV_WIDTH)]
            bias = vec("b_dw", cb * LANES, (cb + 1) * LANES)
            for t0 in range(0, ts, BLOCK_ROWS):
                acc = [bias] * ROW_STRIDE
                for qi in range(1 - CONV_WIDTH, ROW_STRIDE):
                    vin = _strided_rows(v_s, cb, HALO + t0 + qi)
                    for q in range(ROW_STRIDE):
                        delay = q - qi
                        if 0 <= delay < CONV_WIDTH:
                            acc[q] = acc[q] + vin * taps[CONV_WIDTH - 1 - delay]
                for q in range(ROW_STRIDE):
                    cn_s[cb, pl.ds(t0 + q, SUBLANES, stride=ROW_STRIDE), :] = acc[q]

        for r in range(0, ts, LN_ROWS):
            cv = jnp.concatenate([cn_s[cb, r:r + LN_ROWS, :] for cb in range(n_slab)], axis=1)
            vn = _layer_norm(cv) * _rows(vec("lncg"), LN_ROWS) + _rows(vec("lncb"), LN_ROWS)
            c_s[r:r + LN_ROWS, :] = _silu(vn).astype(BF16)

        y_s[:, dp:] = (jnp.dot(c_s[...], w_pw_ref[...], preferred_element_type=F32)
                       + _rows(vec("b_pw"), ts))

        for r in range(0, ts, GATE_ROWS):
            rows = slice(r, r + GATE_ROWS)
            ya = y_s[rows, 0:dp] * _rows(vec("ls"), GATE_ROWS) * _silu(proj_s[rows, 0:dp])
            cat_s[rows, 0:dp] = ya.astype(BF16)
            yb = y_s[rows, dp:] * _silu(proj_s[rows, 3 * dp:4 * dp])
            cat_s[rows, dp:] = yb.astype(BF16)

        yo_s[...] = (jnp.dot(cat_s[...], w_out_ref[...], preferred_element_type=F32)
                     + _rows(vec("b_out"), ts))
        for r in range(0, ts, LN_ROWS):
            t = x_rows(r, LN_ROWS) + _rows(gate, LN_ROWS) * yo_s[r:r + LN_ROWS, :]
            o = _layer_norm(t, eps_post) * _rows(vec("lnpg"), LN_ROWS) + _rows(vec("lnpb"), LN_ROWS)
            out_ref[0, pl.ds(r0 + r, LN_ROWS), :] = o
        return carry

    lax.fori_loop(0, TILES_PER_STEP, tile, 0)


def _inv_count_table():
    q = np.arange(ROW_STRIDE, dtype=np.float64)[:, None]
    s = np.arange(SUBLANES, dtype=np.float64)[None, :]
    t1 = q + ROW_STRIDE * s + 1.0
    tab = np.stack([1.0 / np.minimum(t1, float(w)) for w in POOL_WINDOWS])
    return jnp.asarray(np.broadcast_to(tab[..., None], tab.shape + (LANES,)), dtype=F32)


def _const_spec(shape):
    return pl.BlockSpec(shape, lambda b, i: (0,) * len(shape))


@jax.jit
def kernel(x, c, w_ada, b_ada, w_in, b_in, w_pool, b_pool, ls_pool, w_dw, b_dw, ln_conv_g, ln_conv_b,
           w_pw, b_pw, w_out, b_out, ln_post_g, ln_post_b):
    bsz, seq, d = x.shape
    n_grp, gw = w_pool.shape[0], w_pool.shape[1]
    dp = n_grp * gw
    dc = w_pw.shape[0]
    d_in = w_in.shape[1]
    step_rows = SEQ_TILE * TILES_PER_STEP
    assert seq % step_rows == 0 and SEQ_TILE % BLOCK_ROWS == 0
    assert gw == LANES and n_grp == len(POOL_WINDOWS) and dp == dc
    assert d_in == 2 * dp + 3 * dc and w_dw.shape[0] == CONV_WIDTH
    assert HALO >= CONV_WIDTH - 1 and HALO >= max(POOL_WINDOWS) - 1
    n_slab = dp // LANES

    c8 = jnp.broadcast_to(c[:, None, :], (bsz, SUBLANES, d)).reshape(bsz * SUBLANES, d)
    mod = pl.pallas_call(
        _mod_kernel,
        grid=(3 * d // MOD_COLS,),
        in_specs=[pl.BlockSpec((bsz * SUBLANES, d), lambda j: (0, 0)),
                  pl.BlockSpec((d, MOD_COLS), lambda j: (0, j)),
                  pl.BlockSpec((1, MOD_COLS), lambda j: (0, j))],
        out_specs=pl.BlockSpec((bsz * SUBLANES, MOD_COLS), lambda j: (0, j)),
        out_shape=jax.ShapeDtypeStruct((bsz * SUBLANES, 3 * d), F32),
        name="adaln_mod",
    )(c8, w_ada, b_ada[None, :])
    mod = mod.reshape(bsz, SUBLANES, 3 * d)

    z = jnp.zeros((gw, gw), w_pool.dtype)
    w_pool2 = jnp.stack([
        jnp.block([[w_pool[0], z], [z, w_pool[1]]]),
        jnp.block([[w_pool[2], z], [z, w_pool[3]]]),
    ]).astype(BF16)
    w_dw8 = jnp.broadcast_to(w_dw.reshape(CONV_WIDTH, 1, dc), (CONV_WIDTH, SUBLANES, dc))

    vectors = (b_in, b_pool.reshape(-1), ls_pool, b_dw, ln_conv_g, ln_conv_b, b_pw, b_out,
               ln_post_g, ln_post_b)
    vec_at, off = {}, 0
    for name, v in zip(_VEC_NAMES, vectors):
        assert v.shape[0] % LANES == 0
        vec_at[name] = (off, v.shape[0])
        off += v.shape[0]
    vec8 = jnp.broadcast_to(jnp.concatenate(vectors).astype(F32)[None, :], (SUBLANES, off))

    operands = (x, mod, vec8, w_in.astype(BF16), w_pool2, _inv_count_table(), w_dw8,
                w_pw.astype(BF16), w_out.astype(BF16))
    in_specs = [
        pl.BlockSpec((1, step_rows, d), lambda b, i: (b, i, 0)),
        pl.BlockSpec((1, SUBLANES, 3 * d), lambda b, i: (b, 0, 0)),
    ] + [_const_spec(op.shape) for op in operands[2:]]

    return pl.pallas_call(
        functools.partial(_layer_kernel, vec_at),
        grid=(bsz, seq // step_rows),
        in_specs=in_specs,
        out_specs=pl.BlockSpec((1, step_rows, d), lambda b, i: (b, i, 0)),
        out_shape=jax.ShapeDtypeStruct((bsz, seq, d), x.dtype),
        scratch_shapes=[
            pltpu.VMEM((SEQ_TILE, d), BF16),
            pltpu.VMEM((n_slab, HALO + SEQ_TILE, LANES), F32),
            pltpu.VMEM((SEQ_TILE, dp + 3 * dc), F32),
            pltpu.VMEM((n_slab, SEQ_TILE, LANES), F32),
            pltpu.VMEM((n_slab, HALO + SEQ_TILE, LANES), F32),
            pltpu.VMEM((n_slab, SEQ_TILE, LANES), F32),
            pltpu.VMEM((SEQ_TILE, dc), BF16),
            pltpu.VMEM((SEQ_TILE, dp + dc), BF16),
            pltpu.VMEM((SEQ_TILE, dp + dc), F32),
            pltpu.VMEM((SEQ_TILE, d), F32),
        ],
        compiler_params=pltpu.CompilerParams(
            dimension_semantics=("arbitrary", "arbitrary"),
            vmem_limit_bytes=VMEM_LIMIT_BYTES,
        ),
        name="hybrid_layer",
    )(*operands)
```

```python
import functools

import numpy as np
import jax
import jax.numpy as jnp
from jax import lax
from jax.experimental import pallas as pl
from jax.experimental.pallas import tpu as pltpu

F32 = jnp.float32
BF16 = jnp.bfloat16

POOL_WINDOWS = (2, 4, 8, 16)
CONV_WIDTH = 31
LN_EPS = 1e-5
DEEPNORM_ALPHA = 2.0 ** 0.25

SUBLANES = 8
LANES = 128
SEQ_TILE = 512
TILES_PER_STEP = 2
HALO = 32
ROW_STRIDE = 4
BLOCK_ROWS = SUBLANES * ROW_STRIDE
LN_ROWS = 16
MOD_COLS = 768
VMEM_LIMIT_BYTES = 48 * 1024 * 1024

_VEC_NAMES = ("b_in", "b_pool", "ls", "b_dw", "lncg", "lncb", "b_pw", "b_out", "lnpg", "lnpb")


def _rows(p8, n):
    return p8 if n == SUBLANES else jnp.concatenate([p8] * (n // SUBLANES), axis=0)


def _silu(z):
    return z * jax.nn.sigmoid(z)


def _layer_norm(t, eps=LN_EPS):
    mu = jnp.mean(t, axis=-1, keepdims=True)
    tc = t - mu
    var = jnp.mean(tc * tc, axis=-1, keepdims=True)
    return tc * lax.rsqrt(var + eps)


def _strided_rows(ref, slab, start):
    return ref[slab, pl.ds(start, SUBLANES, stride=ROW_STRIDE), :]


def _mod_kernel(c_ref, w_ref, b_ref, o_ref):
    c = c_ref[...]
    s = _silu(c).astype(BF16)
    o_ref[...] = jnp.dot(s, w_ref[...].astype(BF16), preferred_element_type=F32) + b_ref[...]


def _layer_kernel(vec_at, x_ref, mod_ref, vec_ref, w_in_ref, w_pool_ref, invc_ref, w_dw_ref, w_pw_ref,
                  w_out_ref, out_ref, h_s, u_s, zg_s, pn_s, v_s, cn_s, c_s, cat_s, yo_s):
    ts = SEQ_TILE
    d = x_ref.shape[-1]
    n_slab = u_s.shape[0]
    dp = n_slab * LANES

    def vec(name, lo=0, hi=None):
        off, size = vec_at[name]
        return vec_ref[:, off + lo:off + (size if hi is None else hi)]

    @pl.when(jnp.logical_and(pl.program_id(0) == 0, pl.program_id(1) == 0))
    def _():
        u_s[:, ts:ts + HALO, :] = jnp.zeros((n_slab, HALO, LANES), F32)
        v_s[:, ts:ts + HALO, :] = jnp.zeros((n_slab, HALO, LANES), F32)

    shift = mod_ref[0, :, 0:d]
    scale1 = 1.0 + mod_ref[0, :, d:2 * d]
    gate = mod_ref[0, :, 2 * d:3 * d] * (1.0 / DEEPNORM_ALPHA)
    eps_post = LN_EPS / (DEEPNORM_ALPHA * DEEPNORM_ALPHA)

    def tile(j, carry):
        r0 = pl.multiple_of(j * ts, ts)
        first = jnp.logical_and(pl.program_id(1) == 0, j == 0)

        def x_rows(r, n):
            return x_ref[0, pl.ds(r0 + r, n), :]

        u_s[:, 0:HALO, :] = jnp.where(first, 0.0, u_s[:, ts:ts + HALO, :])
        v_s[:, 0:HALO, :] = jnp.where(first, 0.0, v_s[:, ts:ts + HALO, :])

        for r in range(0, ts, LN_ROWS):
            h = _layer_norm(x_rows(r, LN_ROWS)) * _rows(scale1, LN_ROWS) + _rows(shift, LN_ROWS)
            h_s[r:r + LN_ROWS, :] = h.astype(BF16)

        h = h_s[...]

        def project(lo, hi):
            return (jnp.dot(h, w_in_ref[:, lo:hi], preferred_element_type=F32)
                    + _rows(vec("b_in", lo, hi), ts))

        u = project(0, dp)
        for g in range(n_slab):
            u_s[g, HALO:HALO + ts, :] = u[:, g * LANES:(g + 1) * LANES]
        zg_s[:, 0:dp] = _silu(project(dp, 2 * dp)) * _rows(vec("ls"), ts)
        zg_s[:, dp:] = _silu(project(2 * dp, 3 * dp))
        for cb in range(n_slab):
            ag = project(3 * dp + 2 * cb * LANES, 3 * dp + 2 * (cb + 1) * LANES)
            v_s[cb, HALO:HALO + ts, :] = ag[:, 0:LANES] * jax.nn.sigmoid(ag[:, LANES:])

        for t0 in range(0, ts, BLOCK_ROWS):
            for g, w in enumerate(POOL_WINDOWS):
                s = {q: _strided_rows(u_s, g, HALO + t0 + q) for q in range(1 - w, ROW_STRIDE)}
                cur = s
                span = 1
                while span < w:
                    s = {q: s[q] + s[q - span] for q in s if q - span in s}
                    span *= 2
                for q in range(ROW_STRIDE):
                    inv = jnp.where(first, invc_ref[g, q], 1.0 / w) if t0 == 0 else 1.0 / w
                    pn_s[g, pl.ds(t0 + q, SUBLANES, stride=ROW_STRIDE), :] = s[q] * inv - cur[q]

        p = jnp.concatenate([pn_s[g] for g in range(n_slab)], axis=1).astype(BF16)
        half = dp // 2
        for k in range(2):
            cols = slice(k * half, (k + 1) * half)
            ya = (jnp.dot(p[:, cols], w_pool_ref[k], preferred_element_type=F32)
                  + _rows(vec("b_pool", k * half, (k + 1) * half), ts)) * zg_s[:, cols]
            cat_s[:, cols] = ya.astype(BF16)

        for cb in range(n_slab):
            lanes = slice(cb * LANES, (cb + 1) * LANES)
            taps = [w_dw_ref[k, :, lanes] for k in range(CONV_WIDTH)]
            bias = vec("b_dw", cb * LANES, (cb + 1) * LANES)
            for t0 in range(0, ts, BLOCK_ROWS):
                acc = [bias] * ROW_STRIDE
                for qi in range(1 - CONV_WIDTH, ROW_STRIDE):
                    vin = _strided_rows(v_s, cb, HALO + t0 + qi)
                    for q in range(ROW_STRIDE):
                        delay = q - qi
                        if 0 <= delay < CONV_WIDTH:
                            acc[q] = acc[q] + vin * taps[CONV_WIDTH - 1 - delay]
                for q in range(ROW_STRIDE):
                    cn_s[cb, pl.ds(t0 + q, SUBLANES, stride=ROW_STRIDE), :] = acc[q]

        for r in range(0, ts, LN_ROWS):
            cv = jnp.concatenate([cn_s[cb, r:r + LN_ROWS, :] for cb in range(n_slab)], axis=1)
            vn = _layer_norm(cv) * _rows(vec("lncg"), LN_ROWS) + _rows(vec("lncb"), LN_ROWS)
            c_s[r:r + LN_ROWS, :] = _silu(vn).astype(BF16)

        yb = (jnp.dot(c_s[...], w_pw_ref[...], preferred_element_type=F32)
              + _rows(vec("b_pw"), ts)) * zg_s[:, dp:]
        cat_s[:, dp:] = yb.astype(BF16)

        yo_s[...] = (jnp.dot(cat_s[...], w_out_ref[...], preferred_element_type=F32)
                     + _rows(vec("b_out"), ts))
        for r in range(0, ts, LN_ROWS):
            t = x_rows(r, LN_ROWS) + _rows(gate, LN_ROWS) * yo_s[r:r + LN_ROWS, :]
            o = _layer_norm(t, eps_post) * _rows(vec("lnpg"), LN_ROWS) + _rows(vec("lnpb"), LN_ROWS)
            out_ref[0, pl.ds(r0 + r, LN_ROWS), :] = o
        return carry

    lax.fori_loop(0, TILES_PER_STEP, tile, 0)


def _inv_count_table():
    q = np.arange(ROW_STRIDE, dtype=np.float64)[:, None]
    s = np.arange(SUBLANES, dtype=np.float64)[None, :]
    t1 = q + ROW_STRIDE * s + 1.0
    tab = np.stack([1.0 / np.minimum(t1, float(w)) for w in POOL_WINDOWS])
    return jnp.asarray(np.broadcast_to(tab[..., None], tab.shape + (LANES,)), dtype=F32)


def _const_spec(shape):
    return pl.BlockSpec(shape, lambda b, i: (0,) * len(shape))


@jax.jit
def kernel(x, c, w_ada, b_ada, w_in, b_in, w_pool, b_pool, ls_pool, w_dw, b_dw, ln_conv_g, ln_conv_b,
           w_pw, b_pw, w_out, b_out, ln_post_g, ln_post_b):
    bsz, seq, d = x.shape
    n_grp, gw = w_pool.shape[0], w_pool.shape[1]
    dp = n_grp * gw
    dc = w_pw.shape[0]
    d_in = w_in.shape[1]
    step_rows = SEQ_TILE * TILES_PER_STEP
    assert seq % step_rows == 0 and SEQ_TILE % BLOCK_ROWS == 0
    assert gw == LANES and n_grp == len(POOL_WINDOWS) and dp == dc
    assert d_in == 2 * dp + 3 * dc and w_dw.shape[0] == CONV_WIDTH
    assert HALO >= CONV_WIDTH - 1 and HALO >= max(POOL_WINDOWS) - 1
    n_slab = dp // LANES

    c8 = jnp.broadcast_to(c[:, None, :], (bsz, SUBLANES, d)).reshape(bsz * SUBLANES, d)
    mod = pl.pallas_call(
        _mod_kernel,
        grid=(3 * d // MOD_COLS,),
        in_specs=[pl.BlockSpec((bsz * SUBLANES, d), lambda j: (0, 0)),
                  pl.BlockSpec((d, MOD_COLS), lambda j: (0, j)),
                  pl.BlockSpec((1, MOD_COLS), lambda j: (0, j))],
        out_specs=pl.BlockSpec((bsz * SUBLANES, MOD_COLS), lambda j: (0, j)),
        out_shape=jax.ShapeDtypeStruct((bsz * SUBLANES, 3 * d), F32),
        name="adaln_mod",
    )(c8, w_ada, b_ada[None, :])
    mod = mod.reshape(bsz, SUBLANES, 3 * d)

    z = jnp.zeros((gw, gw), w_pool.dtype)
    w_pool2 = jnp.stack([
        jnp.block([[w_pool[0], z], [z, w_pool[1]]]),
        jnp.block([[w_pool[2], z], [z, w_pool[3]]]),
    ]).astype(BF16)
    w_dw8 = jnp.broadcast_to(w_dw.reshape(CONV_WIDTH, 1, dc), (CONV_WIDTH, SUBLANES, dc))

    def regroup(a):
        glu_a, glu_g = 2 * dp, 2 * dp + dc
        parts = [a[..., 0:2 * dp], a[..., 2 * dp + 2 * dc:]]
        for cb in range(dc // LANES):
            parts += [a[..., glu_a + cb * LANES:glu_a + (cb + 1) * LANES],
                      a[..., glu_g + cb * LANES:glu_g + (cb + 1) * LANES]]
        return jnp.concatenate(parts, axis=-1)

    w_in, b_in = regroup(w_in), regroup(b_in)
    vectors = (b_in, b_pool.reshape(-1), ls_pool, b_dw, ln_conv_g, ln_conv_b, b_pw, b_out,
               ln_post_g, ln_post_b)
    vec_at, off = {}, 0
    for name, v in zip(_VEC_NAMES, vectors):
        assert v.shape[0] % LANES == 0
        vec_at[name] = (off, v.shape[0])
        off += v.shape[0]
    vec8 = jnp.broadcast_to(jnp.concatenate(vectors).astype(F32)[None, :], (SUBLANES, off))

    operands = (x, mod, vec8, w_in.astype(BF16), w_pool2, _inv_count_table(), w_dw8,
                w_pw.astype(BF16), w_out.astype(BF16))
    in_specs = [
        pl.BlockSpec((1, step_rows, d), lambda b, i: (b, i, 0)),
        pl.BlockSpec((1, SUBLANES, 3 * d), lambda b, i: (b, 0, 0)),
    ] + [_const_spec(op.shape) for op in operands[2:]]

    return pl.pallas_call(
        functools.partial(_layer_kernel, vec_at),
        grid=(bsz, seq // step_rows),
        in_specs=in_specs,
        out_specs=pl.BlockSpec((1, step_rows, d), lambda b, i: (b, i, 0)),
        out_shape=jax.ShapeDtypeStruct((bsz, seq, d), x.dtype),
        scratch_shapes=[
            pltpu.VMEM((SEQ_TILE, d), BF16),
            pltpu.VMEM((n_slab, HALO + SEQ_TILE, LANES), F32),
            pltpu.VMEM((SEQ_TILE, dp + dc), F32),
            pltpu.VMEM((n_slab, SEQ_TILE, LANES), F32),
            pltpu.VMEM((n_slab, HALO + SEQ_TILE, LANES), F32),
            pltpu.VMEM((n_slab, SEQ_TILE, LANES), F32),
            pltpu.VMEM((SEQ_TILE, dc), BF16),
            pltpu.VMEM((SEQ_TILE, dp + dc), BF16),
            pltpu.VMEM((SEQ_TILE, d), F32),
        ],
        compiler_params=pltpu.CompilerParams(
            dimension_semantics=("arbitrary", "arbitrary"),
            vmem_limit_bytes=VMEM_LIMIT_BYTES,
        ),
        name="hybrid_layer",
    )(*operands)
```

```python
import functools

import numpy as np
import jax
import jax.numpy as jnp
from jax import lax
from jax.experimental import pallas as pl
from jax.experimental.pallas import tpu as pltpu

F32 = jnp.float32
BF16 = jnp.bfloat16

POOL_WINDOWS = (2, 4, 8, 16)
CONV_WIDTH = 31
LN_EPS = 1e-5
DEEPNORM_ALPHA = 2.0 ** 0.25

SUBLANES = 8
LANES = 128
SEQ_TILE = 512
TILES_PER_STEP = 2
HALO = 32
ROW_STRIDE = 4
BLOCK_ROWS = SUBLANES * ROW_STRIDE
LN_ROWS = 16
MOD_COLS = 768
VMEM_LIMIT_BYTES = 48 * 1024 * 1024

_VEC_NAMES = ("b_in", "b_pool", "ls", "b_dw", "lncg", "lncb", "b_pw", "b_out", "lnpg", "lnpb")


def _rows(p8, n):
    return p8 if n == SUBLANES else jnp.concatenate([p8] * (n // SUBLANES), axis=0)


def _silu(z):
    return z * jax.nn.sigmoid(z)


def _layer_norm(t, eps=LN_EPS):
    mu = jnp.mean(t, axis=-1, keepdims=True)
    tc = t - mu
    var = jnp.mean(tc * tc, axis=-1, keepdims=True)
    return tc * lax.rsqrt(var + eps)


def _strided_rows(ref, slab, start):
    return ref[slab, pl.ds(start, SUBLANES, stride=ROW_STRIDE), :]


def _mod_kernel(c_ref, w_ref, b_ref, o_ref):
    c = c_ref[...]
    s = _silu(c).astype(BF16)
    o_ref[...] = jnp.dot(s, w_ref[...].astype(BF16), preferred_element_type=F32) + b_ref[...]


def _layer_kernel(vec_at, x_ref, mod_ref, *refs):
    n_vec = len(_VEC_NAMES)
    vec_refs = refs[:n_vec]
    w_in_ref, w_pool_ref, invc_ref, w_dw_ref, w_pw_ref, w_out_ref, out_ref = refs[n_vec:n_vec + 7]
    (h_s, u_s, zg_s, pn_s, v_s, cn_s, c_s, cat_s, yo_s, vec_s, w_glu_s, w_pool_s) = refs[n_vec + 7:]
    ts = SEQ_TILE
    d = x_ref.shape[-1]
    n_slab = u_s.shape[0]
    dp = n_slab * LANES
    glu_a, glu_g, z_b = 2 * dp, 3 * dp, 4 * dp

    def vec(name, lo=0, hi=None):
        off, size = vec_at[name]
        return vec_s[:, off + lo:off + (size if hi is None else hi)]

    @pl.when(jnp.logical_and(pl.program_id(0) == 0, pl.program_id(1) == 0))
    def _():
        u_s[:, ts:ts + HALO, :] = jnp.zeros((n_slab, HALO, LANES), F32)
        v_s[:, ts:ts + HALO, :] = jnp.zeros((n_slab, HALO, LANES), F32)
        for name, ref in zip(_VEC_NAMES, vec_refs):
            off, size = vec_at[name]
            vec_s[:, off:off + size] = jnp.broadcast_to(ref[...], (SUBLANES, size))
        for cb in range(n_slab):
            for part, src in enumerate((glu_a, glu_g)):
                w_glu_s[:, (2 * cb + part) * LANES:(2 * cb + part + 1) * LANES] = (
                    w_in_ref[:, src + cb * LANES:src + (cb + 1) * LANES])
        w_pool_s[...] = jnp.zeros(w_pool_s.shape, BF16)
        for grp in range(n_slab):
            k, lo = grp // 2, (grp % 2) * LANES
            w_pool_s[k, lo:lo + LANES, lo:lo + LANES] = w_pool_ref[grp].astype(BF16)

    shift = mod_ref[0, :, 0:d]
    scale1 = 1.0 + mod_ref[0, :, d:2 * d]
    gate = mod_ref[0, :, 2 * d:3 * d] * (1.0 / DEEPNORM_ALPHA)
    eps_post = LN_EPS / (DEEPNORM_ALPHA * DEEPNORM_ALPHA)

    def tile(j, carry):
        r0 = pl.multiple_of(j * ts, ts)
        first = jnp.logical_and(pl.program_id(1) == 0, j == 0)

        def x_rows(r, n):
            return x_ref[0, pl.ds(r0 + r, n), :]

        u_s[:, 0:HALO, :] = jnp.where(first, 0.0, u_s[:, ts:ts + HALO, :])
        v_s[:, 0:HALO, :] = jnp.where(first, 0.0, v_s[:, ts:ts + HALO, :])

        for r in range(0, ts, LN_ROWS):
            h = _layer_norm(x_rows(r, LN_ROWS)) * _rows(scale1, LN_ROWS) + _rows(shift, LN_ROWS)
            h_s[r:r + LN_ROWS, :] = h.astype(BF16)

        h = h_s[...]

        def project(lo, hi):
            return (jnp.dot(h, w_in_ref[:, lo:hi], preferred_element_type=F32)
                    + _rows(vec("b_in", lo, hi), ts))

        u = project(0, dp)
        for g in range(n_slab):
            u_s[g, HALO:HALO + ts, :] = u[:, g * LANES:(g + 1) * LANES]
        zg_s[:, 0:dp] = _silu(project(dp, 2 * dp)) * _rows(vec("ls"), ts)
        zg_s[:, dp:] = _silu(project(z_b, z_b + dp))
        for cb in range(n_slab):
            bias = jnp.concatenate([vec("b_in", glu_a + cb * LANES, glu_a + (cb + 1) * LANES),
                                    vec("b_in", glu_g + cb * LANES, glu_g + (cb + 1) * LANES)], axis=1)
            ag = (jnp.dot(h, w_glu_s[:, 2 * cb * LANES:2 * (cb + 1) * LANES], preferred_element_type=F32)
                  + _rows(bias, ts))
            v_s[cb, HALO:HALO + ts, :] = ag[:, 0:LANES] * jax.nn.sigmoid(ag[:, LANES:])

        for t0 in range(0, ts, BLOCK_ROWS):
            for g, w in enumerate(POOL_WINDOWS):
                s = {q: _strided_rows(u_s, g, HALO + t0 + q) for q in range(1 - w, ROW_STRIDE)}
                cur = s
                span = 1
                while span < w:
                    s = {q: s[q] + s[q - span] for q in s if q - span in s}
                    span *= 2
                for q in range(ROW_STRIDE):
                    inv = jnp.where(first, invc_ref[g, q], 1.0 / w) if t0 == 0 else 1.0 / w
                    pn_s[g, pl.ds(t0 + q, SUBLANES, stride=ROW_STRIDE), :] = s[q] * inv - cur[q]

        p = jnp.concatenate([pn_s[g] for g in range(n_slab)], axis=1).astype(BF16)
        half = dp // 2
        for k in range(2):
            cols = slice(k * half, (k + 1) * half)
            ya = (jnp.dot(p[:, cols], w_pool_s[k], preferred_element_type=F32)
                  + _rows(vec("b_pool", k * half, (k + 1) * half), ts)) * zg_s[:, cols]
            cat_s[:, cols] = ya.astype(BF16)

        for cb in range(n_slab):
            lanes = slice(cb * LANES, (cb + 1) * LANES)
            taps = [w_dw_ref[k, :, lanes] for k in range(CONV_WIDTH)]
            bias = vec("b_dw", cb * LANES, (cb + 1) * LANES)
            for t0 in range(0, ts, BLOCK_ROWS):
                acc = [bias] * ROW_STRIDE
                for qi in range(1 - CONV_WIDTH, ROW_STRIDE):
                    vin = _strided_rows(v_s, cb, HALO + t0 + qi)
                    for q in range(ROW_STRIDE):
                        delay = q - qi
                        if 0 <= delay < CONV_WIDTH:
                            acc[q] = acc[q] + vin * taps[CONV_WIDTH - 1 - delay]
                for q in range(ROW_STRIDE):
                    cn_s[cb, pl.ds(t0 + q, SUBLANES, stride=ROW_STRIDE), :] = acc[q]

        for r in range(0, ts, LN_ROWS):
            cv = jnp.concatenate([cn_s[cb, r:r + LN_ROWS, :] for cb in range(n_slab)], axis=1)
            vn = _layer_norm(cv) * _rows(vec("lncg"), LN_ROWS) + _rows(vec("lncb"), LN_ROWS)
            c_s[r:r + LN_ROWS, :] = _silu(vn).astype(BF16)

        yb = (jnp.dot(c_s[...], w_pw_ref[...], preferred_element_type=F32)
              + _rows(vec("b_pw"), ts)) * zg_s[:, dp:]
        cat_s[:, dp:] = yb.astype(BF16)

        yo_s[...] = (jnp.dot(cat_s[...], w_out_ref[...], preferred_element_type=F32)
                     + _rows(vec("b_out"), ts))
        for r in range(0, ts, LN_ROWS):
            t = x_rows(r, LN_ROWS) + _rows(gate, LN_ROWS) * yo_s[r:r + LN_ROWS, :]
            o = _layer_norm(t, eps_post) * _rows(vec("lnpg"), LN_ROWS) + _rows(vec("lnpb"), LN_ROWS)
            out_ref[0, pl.ds(r0 + r, LN_ROWS), :] = o
        return carry

    lax.fori_loop(0, TILES_PER_STEP, tile, 0)


def _inv_count_table():
    q = np.arange(ROW_STRIDE, dtype=np.float64)[:, None]
    s = np.arange(SUBLANES, dtype=np.float64)[None, :]
    t1 = q + ROW_STRIDE * s + 1.0
    tab = np.stack([1.0 / np.minimum(t1, float(w)) for w in POOL_WINDOWS])
    return jnp.asarray(np.broadcast_to(tab[..., None], tab.shape + (LANES,)), dtype=F32)


def _const_spec(shape):
    return pl.BlockSpec(shape, lambda b, i: (0,) * len(shape))


@jax.jit
def kernel(x, c, w_ada, b_ada, w_in, b_in, w_pool, b_pool, ls_pool, w_dw, b_dw, ln_conv_g, ln_conv_b,
           w_pw, b_pw, w_out, b_out, ln_post_g, ln_post_b):
    bsz, seq, d = x.shape
    n_grp, gw = w_pool.shape[0], w_pool.shape[1]
    dp = n_grp * gw
    dc = w_pw.shape[0]
    d_in = w_in.shape[1]
    step_rows = SEQ_TILE * TILES_PER_STEP
    assert seq % step_rows == 0 and SEQ_TILE % BLOCK_ROWS == 0
    assert gw == LANES and n_grp == len(POOL_WINDOWS) and dp == dc
    assert d_in == 2 * dp + 3 * dc and w_dw.shape[0] == CONV_WIDTH
    assert HALO >= CONV_WIDTH - 1 and HALO >= max(POOL_WINDOWS) - 1
    n_slab = dp // LANES

    c8 = jnp.broadcast_to(c[:, None, :], (bsz, SUBLANES, d)).reshape(bsz * SUBLANES, d)
    mod = pl.pallas_call(
        _mod_kernel,
        grid=(3 * d // MOD_COLS,),
        in_specs=[pl.BlockSpec((bsz * SUBLANES, d), lambda j: (0, 0)),
                  pl.BlockSpec((d, MOD_COLS), lambda j: (0, j)),
                  pl.BlockSpec((1, MOD_COLS), lambda j: (0, j))],
        out_specs=pl.BlockSpec((bsz * SUBLANES, MOD_COLS), lambda j: (0, j)),
        out_shape=jax.ShapeDtypeStruct((bsz * SUBLANES, 3 * d), F32),
        name="adaln_mod",
    )(c8, w_ada, b_ada[None, :])
    mod = mod.reshape(bsz, SUBLANES, 3 * d)

    w_dw8 = jnp.broadcast_to(w_dw.reshape(CONV_WIDTH, 1, dc), (CONV_WIDTH, SUBLANES, dc))
    vectors = (b_in, b_pool.reshape(-1), ls_pool, b_dw, ln_conv_g, ln_conv_b, b_pw, b_out,
               ln_post_g, ln_post_b)
    vec_at, off = {}, 0
    for name, v in zip(_VEC_NAMES, vectors):
        assert v.shape[0] % LANES == 0
        vec_at[name] = (off, v.shape[0])
        off += v.shape[0]

    operands = ((x, mod) + tuple(v.astype(F32).reshape(1, -1) for v in vectors)
                + (w_in.astype(BF16), w_pool, _inv_count_table(), w_dw8, w_pw.astype(BF16),
                   w_out.astype(BF16)))
    in_specs = [
        pl.BlockSpec((1, step_rows, d), lambda b, i: (b, i, 0)),
        pl.BlockSpec((1, SUBLANES, 3 * d), lambda b, i: (b, 0, 0)),
    ] + [_const_spec(op.shape) for op in operands[2:]]

    return pl.pallas_call(
        functools.partial(_layer_kernel, vec_at),
        grid=(bsz, seq // step_rows),
        in_specs=in_specs,
        out_specs=pl.BlockSpec((1, step_rows, d), lambda b, i: (b, i, 0)),
        out_shape=jax.ShapeDtypeStruct((bsz, seq, d), x.dtype),
        scratch_shapes=[
            pltpu.VMEM((SEQ_TILE, d), BF16),
            pltpu.VMEM((n_slab, HALO + SEQ_TILE, LANES), F32),
            pltpu.VMEM((SEQ_TILE, dp + dc), F32),
            pltpu.VMEM((n_slab, SEQ_TILE, LANES), F32),
            pltpu.VMEM((n_slab, HALO + SEQ_TILE, LANES), F32),
            pltpu.VMEM((n_slab, SEQ_TILE, LANES), F32),
            pltpu.VMEM((SEQ_TILE, dc), BF16),
            pltpu.VMEM((SEQ_TILE, dp + dc), BF16),
            pltpu.VMEM((SEQ_TILE, d), F32),
            pltpu.VMEM((SUBLANES, off), F32),
            pltpu.VMEM((d, 2 * dc), BF16),
            pltpu.VMEM((n_grp // 2, 2 * gw, 2 * gw), BF16),
        ],
        compiler_params=pltpu.CompilerParams(
            dimension_semantics=("arbitrary", "arbitrary"),
            vmem_limit_bytes=VMEM_LIMIT_BYTES,
        ),
        name="hybrid_layer",
    )(*operands)
```

```python
import functools

import numpy as np
import jax
import jax.numpy as jnp
from jax import lax
from jax.experimental import pallas as pl
from jax.experimental.pallas import tpu as pltpu

F32 = jnp.float32
BF16 = jnp.bfloat16

POOL_WINDOWS = (2, 4, 8, 16)
CONV_WIDTH = 31
LN_EPS = 1e-5
DEEPNORM_ALPHA = 2.0 ** 0.25

SUBLANES = 8
LANES = 128
SEQ_TILE = 512
TILES_PER_STEP = 2
HALO = 32
ROW_STRIDE = 4
BLOCK_ROWS = SUBLANES * ROW_STRIDE
LN_ROWS = 16
MOD_COLS = 768
VMEM_LIMIT_BYTES = 48 * 1024 * 1024

_VEC_NAMES = ("b_in", "b_pool", "ls", "b_dw", "lncg", "lncb", "b_pw", "b_out", "lnpg", "lnpb")


def _rows(p8, n):
    return p8 if n == SUBLANES else jnp.concatenate([p8] * (n // SUBLANES), axis=0)


def _silu(z):
    return z * jax.nn.sigmoid(z)


def _layer_norm(t, eps=LN_EPS):
    mu = jnp.mean(t, axis=-1, keepdims=True)
    tc = t - mu
    var = jnp.mean(tc * tc, axis=-1, keepdims=True)
    return tc * lax.rsqrt(var + eps)


def _strided_rows(ref, slab, start):
    return ref[slab, pl.ds(start, SUBLANES, stride=ROW_STRIDE), :]


def _mod_kernel(vec_at, c_ref, w_ref, b_ref, w_dw_ref, *refs):
    vec_refs, (o_ref, vec8_ref, w_dw8_ref) = refs[:-3], refs[-3:]
    bsz = c_ref.shape[0]
    c = jnp.concatenate([jnp.broadcast_to(c_ref[b:b + 1, :], (SUBLANES, c_ref.shape[1]))
                         for b in range(bsz)], axis=0)
    s = _silu(c).astype(BF16)
    o_ref[...] = jnp.dot(s, w_ref[...].astype(BF16), preferred_element_type=F32) + b_ref[...]

    @pl.when(pl.program_id(0) == 0)
    def _():
        for name, ref in zip(_VEC_NAMES, vec_refs):
            off, size = vec_at[name]
            vec8_ref[:, off:off + size] = jnp.broadcast_to(ref[...], (SUBLANES, size))
        w_dw8_ref[...] = jnp.broadcast_to(w_dw_ref[...], w_dw8_ref.shape)


def _layer_kernel(vec_at, x_ref, mod_ref, vec_ref, w_in_ref, w_pool_ref, invc_ref, w_dw_ref, w_pw_ref,
                  w_out_ref, out_ref, h_s, u_s, zg_s, pn_s, v_s, cn_s, c_s, cat_s, yo_s, w_glu_s, w_pool_s):
    ts = SEQ_TILE
    d = x_ref.shape[-1]
    n_slab = u_s.shape[0]
    dp = n_slab * LANES
    glu_a, glu_g, z_b = 2 * dp, 3 * dp, 4 * dp

    def vec(name, lo=0, hi=None):
        off, size = vec_at[name]
        return vec_ref[:, off + lo:off + (size if hi is None else hi)]

    @pl.when(jnp.logical_and(pl.program_id(0) == 0, pl.program_id(1) == 0))
    def _():
        u_s[:, ts:ts + HALO, :] = jnp.zeros((n_slab, HALO, LANES), F32)
        v_s[:, ts:ts + HALO, :] = jnp.zeros((n_slab, HALO, LANES), F32)
        for cb in range(n_slab):
            for part, src in enumerate((glu_a, glu_g)):
                w_glu_s[:, (2 * cb + part) * LANES:(2 * cb + part + 1) * LANES] = (
                    w_in_ref[:, src + cb * LANES:src + (cb + 1) * LANES])
        w_pool_s[...] = jnp.zeros(w_pool_s.shape, BF16)
        for grp in range(n_slab):
            k, lo = grp // 2, (grp % 2) * LANES
            w_pool_s[k, lo:lo + LANES, lo:lo + LANES] = w_pool_ref[grp].astype(BF16)

    shift = mod_ref[0, :, 0:d]
    scale1 = 1.0 + mod_ref[0, :, d:2 * d]
    gate = mod_ref[0, :, 2 * d:3 * d] * (1.0 / DEEPNORM_ALPHA)
    eps_post = LN_EPS / (DEEPNORM_ALPHA * DEEPNORM_ALPHA)

    def tile(j, carry):
        r0 = pl.multiple_of(j * ts, ts)
        first = jnp.logical_and(pl.program_id(1) == 0, j == 0)

        def x_rows(r, n):
            return x_ref[0, pl.ds(r0 + r, n), :]

        u_s[:, 0:HALO, :] = jnp.where(first, 0.0, u_s[:, ts:ts + HALO, :])
        v_s[:, 0:HALO, :] = jnp.where(first, 0.0, v_s[:, ts:ts + HALO, :])

        for r in range(0, ts, LN_ROWS):
            h = _layer_norm(x_rows(r, LN_ROWS)) * _rows(scale1, LN_ROWS) + _rows(shift, LN_ROWS)
            h_s[r:r + LN_ROWS, :] = h.astype(BF16)

        h = h_s[...]

        def project(lo, hi):
            return (jnp.dot(h, w_in_ref[:, lo:hi], preferred_element_type=F32)
                    + _rows(vec("b_in", lo, hi), ts))

        u = project(0, dp)
        for g in range(n_slab):
            u_s[g, HALO:HALO + ts, :] = u[:, g * LANES:(g + 1) * LANES]
        zg_s[:, 0:dp] = _silu(project(dp, 2 * dp)) * _rows(vec("ls"), ts)
        zg_s[:, dp:] = _silu(project(z_b, z_b + dp))
        for cb in range(n_slab):
            bias = jnp.concatenate([vec("b_in", glu_a + cb * LANES, glu_a + (cb + 1) * LANES),
                                    vec("b_in", glu_g + cb * LANES, glu_g + (cb + 1) * LANES)], axis=1)
            ag = (jnp.dot(h, w_glu_s[:, 2 * cb * LANES:2 * (cb + 1) * LANES], preferred_element_type=F32)
                  + _rows(bias, ts))
            v_s[cb, HALO:HALO + ts, :] = ag[:, 0:LANES] * jax.nn.sigmoid(ag[:, LANES:])

        for t0 in range(0, ts, BLOCK_ROWS):
            for g, w in enumerate(POOL_WINDOWS):
                s = {q: _strided_rows(u_s, g, HALO + t0 + q) for q in range(1 - w, ROW_STRIDE)}
                cur = s
                span = 1
                while span < w:
                    s = {q: s[q] + s[q - span] for q in s if q - span in s}
                    span *= 2
                for q in range(ROW_STRIDE):
                    inv = jnp.where(first, invc_ref[g, q], 1.0 / w) if t0 == 0 else 1.0 / w
                    pn_s[g, pl.ds(t0 + q, SUBLANES, stride=ROW_STRIDE), :] = s[q] * inv - cur[q]

        p = jnp.concatenate([pn_s[g] for g in range(n_slab)], axis=1).astype(BF16)
        half = dp // 2
        for k in range(2):
            cols = slice(k * half, (k + 1) * half)
            ya = (jnp.dot(p[:, cols], w_pool_s[k], preferred_element_type=F32)
                  + _rows(vec("b_pool", k * half, (k + 1) * half), ts)) * zg_s[:, cols]
            cat_s[:, cols] = ya.astype(BF16)

        for cb in range(n_slab):
            lanes = slice(cb * LANES, (cb + 1) * LANES)
            taps = [w_dw_ref[k, :, lanes] for k in range(CONV_WIDTH)]
            bias = vec("b_dw", cb * LANES, (cb + 1) * LANES)
            for t0 in range(0, ts, BLOCK_ROWS):
                acc = [bias] * ROW_STRIDE
                for qi in range(1 - CONV_WIDTH, ROW_STRIDE):
                    vin = _strided_rows(v_s, cb, HALO + t0 + qi)
                    for q in range(ROW_STRIDE):
                        delay = q - qi
                        if 0 <= delay < CONV_WIDTH:
                            acc[q] = acc[q] + vin * taps[CONV_WIDTH - 1 - delay]
                for q in range(ROW_STRIDE):
                    cn_s[cb, pl.ds(t0 + q, SUBLANES, stride=ROW_STRIDE), :] = acc[q]

        for r in range(0, ts, LN_ROWS):
            cv = jnp.concatenate([cn_s[cb, r:r + LN_ROWS, :] for cb in range(n_slab)], axis=1)
            vn = _layer_norm(cv) * _rows(vec("lncg"), LN_ROWS) + _rows(vec("lncb"), LN_ROWS)
            c_s[r:r + LN_ROWS, :] = _silu(vn).astype(BF16)

        yb = (jnp.dot(c_s[...], w_pw_ref[...], preferred_element_type=F32)
              + _rows(vec("b_pw"), ts)) * zg_s[:, dp:]
        cat_s[:, dp:] = yb.astype(BF16)

        yo_s[...] = (jnp.dot(cat_s[...], w_out_ref[...], preferred_element_type=F32)
                     + _rows(vec("b_out"), ts))
        for r in range(0, ts, LN_ROWS):
            t = x_rows(r, LN_ROWS) + _rows(gate, LN_ROWS) * yo_s[r:r + LN_ROWS, :]
            o = _layer_norm(t, eps_post) * _rows(vec("lnpg"), LN_ROWS) + _rows(vec("lnpb"), LN_ROWS)
            out_ref[0, pl.ds(r0 + r, LN_ROWS), :] = o
        return carry

    lax.fori_loop(0, TILES_PER_STEP, tile, 0)


def _inv_count_table():
    q = np.arange(ROW_STRIDE, dtype=np.float64)[:, None]
    s = np.arange(SUBLANES, dtype=np.float64)[None, :]
    t1 = q + ROW_STRIDE * s + 1.0
    tab = np.stack([1.0 / np.minimum(t1, float(w)) for w in POOL_WINDOWS])
    return jnp.asarray(np.broadcast_to(tab[..., None], tab.shape + (LANES,)), dtype=F32)


def _const_spec(shape):
    return pl.BlockSpec(shape, lambda b, i: (0,) * len(shape))


@jax.jit
def kernel(x, c, w_ada, b_ada, w_in, b_in, w_pool, b_pool, ls_pool, w_dw, b_dw, ln_conv_g, ln_conv_b,
           w_pw, b_pw, w_out, b_out, ln_post_g, ln_post_b):
    bsz, seq, d = x.shape
    n_grp, gw = w_pool.shape[0], w_pool.shape[1]
    dp = n_grp * gw
    dc = w_pw.shape[0]
    d_in = w_in.shape[1]
    step_rows = SEQ_TILE * TILES_PER_STEP
    assert seq % step_rows == 0 and SEQ_TILE % BLOCK_ROWS == 0
    assert gw == LANES and n_grp == len(POOL_WINDOWS) and dp == dc
    assert d_in == 2 * dp + 3 * dc and w_dw.shape[0] == CONV_WIDTH
    assert HALO >= CONV_WIDTH - 1 and HALO >= max(POOL_WINDOWS) - 1
    n_slab = dp // LANES

    vectors = (b_in, b_pool.reshape(-1), ls_pool, b_dw, ln_conv_g, ln_conv_b, b_pw, b_out,
               ln_post_g, ln_post_b)
    vec_at, off = {}, 0
    for name, v in zip(_VEC_NAMES, vectors):
        assert v.shape[0] % LANES == 0
        vec_at[name] = (off, v.shape[0])
        off += v.shape[0]

    def whole(shape):
        return pl.BlockSpec(shape, lambda j: (0,) * len(shape))

    mod, vec8, w_dw8 = pl.pallas_call(
        functools.partial(_mod_kernel, vec_at),
        grid=(3 * d // MOD_COLS,),
        in_specs=[whole((bsz, d)),
                  pl.BlockSpec((d, MOD_COLS), lambda j: (0, j)),
                  pl.BlockSpec((1, MOD_COLS), lambda j: (0, j)),
                  whole(w_dw.shape)] + [whole((1, v.shape[0])) for v in vectors],
        out_specs=[pl.BlockSpec((bsz * SUBLANES, MOD_COLS), lambda j: (0, j)),
                   whole((SUBLANES, off)), whole((CONV_WIDTH, SUBLANES, dc))],
        out_shape=[jax.ShapeDtypeStruct((bsz * SUBLANES, 3 * d), F32),
                   jax.ShapeDtypeStruct((SUBLANES, off), F32),
                   jax.ShapeDtypeStruct((CONV_WIDTH, SUBLANES, dc), F32)],
        compiler_params=pltpu.CompilerParams(dimension_semantics=("arbitrary",)),
        name="adaln_mod",
    )(c, w_ada, b_ada[None, :], w_dw, *(v.astype(F32).reshape(1, -1) for v in vectors))
    mod = mod.reshape(bsz, SUBLANES, 3 * d)

    operands = (x, mod, vec8, w_in.astype(BF16), w_pool, _inv_count_table(), w_dw8,
                w_pw.astype(BF16), w_out.astype(BF16))
    in_specs = [
        pl.BlockSpec((1, step_rows, d), lambda b, i: (b, i, 0)),
        pl.BlockSpec((1, SUBLANES, 3 * d), lambda b, i: (b, 0, 0)),
    ] + [_const_spec(op.shape) for op in operands[2:]]

    return pl.pallas_call(
        functools.partial(_layer_kernel, vec_at),
        grid=(bsz, seq // step_rows),
        in_specs=in_specs,
        out_specs=pl.BlockSpec((1, step_rows, d), lambda b, i: (b, i, 0)),
        out_shape=jax.ShapeDtypeStruct((bsz, seq, d), x.dtype),
        scratch_shapes=[
            pltpu.VMEM((SEQ_TILE, d), BF16),
            pltpu.VMEM((n_slab, HALO + SEQ_TILE, LANES), F32),
            pltpu.VMEM((SEQ_TILE, dp + dc), F32),
            pltpu.VMEM((n_slab, SEQ_TILE, LANES), F32),
            pltpu.VMEM((n_slab, HALO + SEQ_TILE, LANES), F32),
            pltpu.VMEM((n_slab, SEQ_TILE, LANES), F32),
            pltpu.VMEM((SEQ_TILE, dc), BF16),
            pltpu.VMEM((SEQ_TILE, dp + dc), BF16),
            pltpu.VMEM((SEQ_TILE, d), F32),
            pltpu.VMEM((d, 2 * dc), BF16),
            pltpu.VMEM((n_grp // 2, 2 * gw, 2 * gw), BF16),
        ],
        compiler_params=pltpu.CompilerParams(
            dimension_semantics=("arbitrary", "arbitrary"),
            vmem_limit_bytes=VMEM_LIMIT_BYTES,
        ),
        name="hybrid_layer",
    )(*operands)
```

```python
import functools

import numpy as np
import jax
import jax.numpy as jnp
from jax import lax
from jax.experimental import pallas as pl
from jax.experimental.pallas import tpu as pltpu

F32 = jnp.float32
BF16 = jnp.bfloat16

POOL_WINDOWS = (2, 4, 8, 16)
CONV_WIDTH = 31
LN_EPS = 1e-5
DEEPNORM_ALPHA = 2.0 ** 0.25

SUBLANES = 8
LANES = 128
SEQ_TILE = 512
TILES_PER_STEP = 2
HALO = 32
ROW_STRIDE = 4
BLOCK_ROWS = SUBLANES * ROW_STRIDE
LN_ROWS = 16
MOD_COLS = 768
VMEM_LIMIT_BYTES = 48 * 1024 * 1024

_VEC_NAMES = ("b_in", "b_pool", "ls", "b_dw", "lncg", "lncb", "b_pw", "b_out", "lnpg", "lnpb")


def _rows(p8, n):
    return p8 if n == SUBLANES else jnp.concatenate([p8] * (n // SUBLANES), axis=0)


def _silu(z):
    return z * jax.nn.sigmoid(z)


def _layer_norm(t, eps=LN_EPS):
    mu = jnp.mean(t, axis=-1, keepdims=True)
    tc = t - mu
    var = jnp.mean(tc * tc, axis=-1, keepdims=True)
    return tc * lax.rsqrt(var + eps)


def _strided_rows(ref, slab, start):
    return ref[slab, pl.ds(start, SUBLANES, stride=ROW_STRIDE), :]


def _mod_kernel(vec_at, c_ref, w_ref, b_ref, w_dw_ref, *refs):
    vec_refs, (o_ref, vec8_ref, w_dw8_ref) = refs[:-3], refs[-3:]
    bsz = c_ref.shape[0]
    c = jnp.concatenate([jnp.broadcast_to(c_ref[b:b + 1, :], (SUBLANES, c_ref.shape[1]))
                         for b in range(bsz)], axis=0)
    s = _silu(c).astype(BF16)
    o_ref[...] = jnp.dot(s, w_ref[...].astype(BF16), preferred_element_type=F32) + b_ref[...]

    @pl.when(pl.program_id(0) == 0)
    def _():
        for name, ref in zip(_VEC_NAMES, vec_refs):
            off, size = vec_at[name]
            vec8_ref[:, off:off + size] = jnp.broadcast_to(ref[...], (SUBLANES, size))
        w_dw8_ref[...] = jnp.broadcast_to(w_dw_ref[...], w_dw8_ref.shape)


def _layer_kernel(vec_at, x_ref, mod_ref, vec_ref, w_in_ref, w_pool_ref, invc_ref, w_dw_ref, w_pw_ref,
                  w_out_ref, out_ref, h_s, u_s, zg_s, pn_s, v_s, cn_s, c_s, cat_s, yo_s, w_glu_s, w_pool_s):
    ts = SEQ_TILE
    d = x_ref.shape[-1]
    n_slab = u_s.shape[0]
    dp = n_slab * LANES
    glu_a, glu_g, z_b = 2 * dp, 3 * dp, 4 * dp

    def vec(name, lo=0, hi=None):
        off, size = vec_at[name]
        return vec_ref[:, off + lo:off + (size if hi is None else hi)]

    @pl.when(jnp.logical_and(pl.program_id(0) == 0, pl.program_id(1) == 0))
    def _():
        u_s[:, ts:ts + HALO, :] = jnp.zeros((n_slab, HALO, LANES), F32)
        v_s[:, ts:ts + HALO, :] = jnp.zeros((n_slab, HALO, LANES), F32)
        for cb in range(n_slab):
            for part, src in enumerate((glu_a, glu_g)):
                w_glu_s[:, (2 * cb + part) * LANES:(2 * cb + part + 1) * LANES] = (
                    w_in_ref[:, src + cb * LANES:src + (cb + 1) * LANES])
        w_pool_s[...] = jnp.zeros(w_pool_s.shape, BF16)
        for grp in range(n_slab):
            k, lo = grp // 2, (grp % 2) * LANES
            w_pool_s[k, lo:lo + LANES, lo:lo + LANES] = w_pool_ref[grp].astype(BF16)

    shift = mod_ref[0, :, 0:d]
    scale1 = 1.0 + mod_ref[0, :, d:2 * d]
    gate = mod_ref[0, :, 2 * d:3 * d] * (1.0 / DEEPNORM_ALPHA)
    eps_post = LN_EPS / (DEEPNORM_ALPHA * DEEPNORM_ALPHA)
    never = pl.program_id(0) < 0

    def tile(j, carry):
        r0 = pl.multiple_of(j * ts, ts)
        first = jnp.logical_and(pl.program_id(1) == 0, j == 0)

        def x_rows(r, n):
            return x_ref[0, pl.ds(r0 + r, n), :]

        u_s[:, 0:HALO, :] = jnp.where(first, 0.0, u_s[:, ts:ts + HALO, :])
        v_s[:, 0:HALO, :] = jnp.where(first, 0.0, v_s[:, ts:ts + HALO, :])

        for r in range(0, ts, LN_ROWS):
            h = _layer_norm(x_rows(r, LN_ROWS)) * _rows(scale1, LN_ROWS) + _rows(shift, LN_ROWS)
            h_s[r:r + LN_ROWS, :] = h.astype(BF16)

        h = h_s[...]

        def project(lo, hi):
            return (jnp.dot(h, w_in_ref[:, lo:hi], preferred_element_type=F32)
                    + _rows(vec("b_in", lo, hi), ts))

        u = project(0, dp)
        for g in range(n_slab):
            u_s[g, HALO:HALO + ts, :] = u[:, g * LANES:(g + 1) * LANES]

        def glu(cb):
            bias = jnp.concatenate([vec("b_in", glu_a + cb * LANES, glu_a + (cb + 1) * LANES),
                                    vec("b_in", glu_g + cb * LANES, glu_g + (cb + 1) * LANES)], axis=1)
            ag = (jnp.dot(h, w_glu_s[:, 2 * cb * LANES:2 * (cb + 1) * LANES], preferred_element_type=F32)
                  + _rows(bias, ts))
            v_s[cb, HALO:HALO + ts, :] = ag[:, 0:LANES] * jax.nn.sigmoid(ag[:, LANES:])

        def conv(cb, after=None):
            lanes = slice(cb * LANES, (cb + 1) * LANES)
            taps = [w_dw_ref[k, :, lanes] for k in range(CONV_WIDTH)]
            bias = vec("b_dw", cb * LANES, (cb + 1) * LANES)
            for t0 in range(0, ts, BLOCK_ROWS):
                acc = [bias] * ROW_STRIDE
                for qi in range(1 - CONV_WIDTH, ROW_STRIDE):
                    vin = _strided_rows(v_s, cb, HALO + t0 + qi)
                    if after is not None and qi == 1 - CONV_WIDTH:
                        vin = jnp.where(never, after, vin)
                    for q in range(ROW_STRIDE):
                        delay = q - qi
                        if 0 <= delay < CONV_WIDTH:
                            acc[q] = acc[q] + vin * taps[CONV_WIDTH - 1 - delay]
                for q in range(ROW_STRIDE):
                    cn_s[cb, pl.ds(t0 + q, SUBLANES, stride=ROW_STRIDE), :] = acc[q]

        for cb in range(n_slab):
            glu(cb)
        conv(0)
        za = _silu(project(dp, 2 * dp)) * _rows(vec("ls"), ts)
        zg_s[:, 0:dp] = za
        conv(1)
        zb = _silu(project(z_b, z_b + dp))
        zg_s[:, dp:] = zb
        for cb in range(2, n_slab):
            conv(cb, after=(za if cb % 2 == 0 else zb)[-SUBLANES:, -LANES:])

        for t0 in range(0, ts, BLOCK_ROWS):
            for g, w in enumerate(POOL_WINDOWS):
                s = {q: _strided_rows(u_s, g, HALO + t0 + q) for q in range(1 - w, ROW_STRIDE)}
                cur = s
                span = 1
                while span < w:
                    s = {q: s[q] + s[q - span] for q in s if q - span in s}
                    span *= 2
                for q in range(ROW_STRIDE):
                    inv = jnp.where(first, invc_ref[g, q], 1.0 / w) if t0 == 0 else 1.0 / w
                    pn_s[g, pl.ds(t0 + q, SUBLANES, stride=ROW_STRIDE), :] = s[q] * inv - cur[q]

        p = jnp.concatenate([pn_s[g] for g in range(n_slab)], axis=1).astype(BF16)
        half = dp // 2
        for k in range(2):
            cols = slice(k * half, (k + 1) * half)
            ya = (jnp.dot(p[:, cols], w_pool_s[k], preferred_element_type=F32)
                  + _rows(vec("b_pool", k * half, (k + 1) * half), ts)) * zg_s[:, cols]
            cat_s[:, cols] = ya.astype(BF16)

        for r in range(0, ts, LN_ROWS):
            cv = jnp.concatenate([cn_s[cb, r:r + LN_ROWS, :] for cb in range(n_slab)], axis=1)
            vn = _layer_norm(cv) * _rows(vec("lncg"), LN_ROWS) + _rows(vec("lncb"), LN_ROWS)
            c_s[r:r + LN_ROWS, :] = _silu(vn).astype(BF16)

        yb = (jnp.dot(c_s[...], w_pw_ref[...], preferred_element_type=F32)
              + _rows(vec("b_pw"), ts)) * zg_s[:, dp:]
        cat_s[:, dp:] = yb.astype(BF16)

        y = (jnp.dot(cat_s[...], w_out_ref[...], preferred_element_type=F32)
             + _rows(vec("b_out"), ts))
        yo_s[...] = x_rows(0, ts) + _rows(gate, ts) * y
        for r in range(0, ts, LN_ROWS):
            t = yo_s[r:r + LN_ROWS, :]
            o = _layer_norm(t, eps_post) * _rows(vec("lnpg"), LN_ROWS) + _rows(vec("lnpb"), LN_ROWS)
            out_ref[0, pl.ds(r0 + r, LN_ROWS), :] = o
        return carry

    lax.fori_loop(0, TILES_PER_STEP, tile, 0)


def _inv_count_table():
    q = np.arange(ROW_STRIDE, dtype=np.float64)[:, None]
    s = np.arange(SUBLANES, dtype=np.float64)[None, :]
    t1 = q + ROW_STRIDE * s + 1.0
    tab = np.stack([1.0 / np.minimum(t1, float(w)) for w in POOL_WINDOWS])
    return jnp.asarray(np.broadcast_to(tab[..., None], tab.shape + (LANES,)), dtype=F32)


def _const_spec(shape):
    return pl.BlockSpec(shape, lambda b, i: (0,) * len(shape))


@jax.jit
def kernel(x, c, w_ada, b_ada, w_in, b_in, w_pool, b_pool, ls_pool, w_dw, b_dw, ln_conv_g, ln_conv_b,
           w_pw, b_pw, w_out, b_out, ln_post_g, ln_post_b):
    bsz, seq, d = x.shape
    n_grp, gw = w_pool.shape[0], w_pool.shape[1]
    dp = n_grp * gw
    dc = w_pw.shape[0]
    d_in = w_in.shape[1]
    step_rows = SEQ_TILE * TILES_PER_STEP
    assert seq % step_rows == 0 and SEQ_TILE % BLOCK_ROWS == 0
    assert gw == LANES and n_grp == len(POOL_WINDOWS) and dp == dc
    assert d_in == 2 * dp + 3 * dc and w_dw.shape[0] == CONV_WIDTH
    assert HALO >= CONV_WIDTH - 1 and HALO >= max(POOL_WINDOWS) - 1
    n_slab = dp // LANES

    vectors = (b_in, b_pool.reshape(-1), ls_pool, b_dw, ln_conv_g, ln_conv_b, b_pw, b_out,
               ln_post_g, ln_post_b)
    vec_at, off = {}, 0
    for name, v in zip(_VEC_NAMES, vectors):
        assert v.shape[0] % LANES == 0
        vec_at[name] = (off, v.shape[0])
        off += v.shape[0]

    def whole(shape):
        return pl.BlockSpec(shape, lambda j: (0,) * len(shape))

    mod, vec8, w_dw8 = pl.pallas_call(
        functools.partial(_mod_kernel, vec_at),
        grid=(3 * d // MOD_COLS,),
        in_specs=[whole((bsz, d)),
                  pl.BlockSpec((d, MOD_COLS), lambda j: (0, j)),
                  pl.BlockSpec((1, MOD_COLS), lambda j: (0, j)),
                  whole(w_dw.shape)] + [whole((1, v.shape[0])) for v in vectors],
        out_specs=[pl.BlockSpec((bsz * SUBLANES, MOD_COLS), lambda j: (0, j)),
                   whole((SUBLANES, off)), whole((CONV_WIDTH, SUBLANES, dc))],
        out_shape=[jax.ShapeDtypeStruct((bsz * SUBLANES, 3 * d), F32),
                   jax.ShapeDtypeStruct((SUBLANES, off), F32),
                   jax.ShapeDtypeStruct((CONV_WIDTH, SUBLANES, dc), F32)],
        compiler_params=pltpu.CompilerParams(dimension_semantics=("arbitrary",)),
        name="adaln_mod",
    )(c, w_ada, b_ada[None, :], w_dw, *(v.astype(F32).reshape(1, -1) for v in vectors))
    mod = mod.reshape(bsz, SUBLANES, 3 * d)

    operands = (x, mod, vec8, w_in.astype(BF16), w_pool, _inv_count_table(), w_dw8,
                w_pw.astype(BF16), w_out.astype(BF16))
    in_specs = [
        pl.BlockSpec((1, step_rows, d), lambda b, i: (b, i, 0)),
        pl.BlockSpec((1, SUBLANES, 3 * d), lambda b, i: (b, 0, 0)),
    ] + [_const_spec(op.shape) for op in operands[2:]]

    return pl.pallas_call(
        functools.partial(_layer_kernel, vec_at),
        grid=(bsz, seq // step_rows),
        in_specs=in_specs,
        out_specs=pl.BlockSpec((1, step_rows, d), lambda b, i: (b, i, 0)),
        out_shape=jax.ShapeDtypeStruct((bsz, seq, d), x.dtype),
        scratch_shapes=[
            pltpu.VMEM((SEQ_TILE, d), BF16),
            pltpu.VMEM((n_slab, HALO + SEQ_TILE, LANES), F32),
            pltpu.VMEM((SEQ_TILE, dp + dc), F32),
            pltpu.VMEM((n_slab, SEQ_TILE, LANES), F32),
            pltpu.VMEM((n_slab, HALO + SEQ_TILE, LANES), F32),
            pltpu.VMEM((n_slab, SEQ_TILE, LANES), F32),
            pltpu.VMEM((SEQ_TILE, dc), BF16),
            pltpu.VMEM((SEQ_TILE, dp + dc), BF16),
            pltpu.VMEM((SEQ_TILE, d), F32),
            pltpu.VMEM((d, 2 * dc), BF16),
            pltpu.VMEM((n_grp // 2, 2 * gw, 2 * gw), BF16),
        ],
        compiler_params=pltpu.CompilerParams(
            dimension_semantics=("arbitrary", "arbitrary"),
            vmem_limit_bytes=VMEM_LIMIT_BYTES,
        ),
        name="hybrid_layer",
    )(*operands)
```

```python
import functools

import numpy as np
import jax
import jax.numpy as jnp
from jax import lax
from jax.experimental import pallas as pl
from jax.experimental.pallas import tpu as pltpu

F32 = jnp.float32
BF16 = jnp.bfloat16

POOL_WINDOWS = (2, 4, 8, 16)
CONV_WIDTH = 31
LN_EPS = 1e-5
DEEPNORM_ALPHA = 2.0 ** 0.25

SUBLANES = 8
LANES = 128
SEQ_TILE = 512
TILES_PER_STEP = 2
HALO = 32
ROW_STRIDE = 4
BLOCK_ROWS = SUBLANES * ROW_STRIDE
LN_ROWS = 16
MOD_COLS = 768
VMEM_LIMIT_BYTES = 48 * 1024 * 1024

_VEC_NAMES = ("b_in", "b_pool", "ls", "b_dw", "lncg", "lncb", "b_pw", "b_out", "lnpg", "lnpb")


def _rows(p8, n):
    return p8 if n == SUBLANES else jnp.concatenate([p8] * (n // SUBLANES), axis=0)


def _silu(z):
    return z * jax.nn.sigmoid(z)


def _layer_norm(t, eps=LN_EPS):
    mu = jnp.mean(t, axis=-1, keepdims=True)
    tc = t - mu
    var = jnp.mean(tc * tc, axis=-1, keepdims=True)
    return tc * lax.rsqrt(var + eps)


def _strided_rows(ref, slab, start):
    return ref[slab, pl.ds(start, SUBLANES, stride=ROW_STRIDE), :]


def _mod_kernel(vec_at, c_ref, w_top_ref, w_bot_ref, b_ref, w_dw_ref, *refs):
    vec_refs, (o_ref, vec8_ref, w_dw8_ref) = refs[:-3], refs[-3:]
    bsz = c_ref.shape[0]
    half = w_top_ref.shape[0]
    c = jnp.concatenate([jnp.broadcast_to(c_ref[b:b + 1, :], (SUBLANES, c_ref.shape[1]))
                         for b in range(bsz)], axis=0)
    s = _silu(c).astype(BF16)
    o_ref[...] = (jnp.dot(s[:, 0:half], w_top_ref[...].astype(BF16), preferred_element_type=F32)
                  + jnp.dot(s[:, half:], w_bot_ref[...].astype(BF16), preferred_element_type=F32)
                  + b_ref[...])

    @pl.when(pl.program_id(0) == 0)
    def _():
        for name, ref in zip(_VEC_NAMES, vec_refs):
            off, size = vec_at[name]
            vec8_ref[:, off:off + size] = jnp.broadcast_to(ref[...], (SUBLANES, size))
        w_dw8_ref[...] = jnp.broadcast_to(w_dw_ref[...], w_dw8_ref.shape)


def _layer_kernel(vec_at, x_ref, mod_ref, vec_ref, w_in_ref, w_pool_ref, invc_ref, w_dw_ref, w_pw_ref,
                  w_out_ref, out_ref, h_s, u_s, zg_s, pn_s, v_s, cn_s, c_s, cat_s, yo_s, w_glu_s, w_pool_s):
    ts = SEQ_TILE
    d = x_ref.shape[-1]
    n_slab = u_s.shape[0]
    dp = n_slab * LANES
    glu_a, glu_g, z_b = 2 * dp, 3 * dp, 4 * dp

    def vec(name, lo=0, hi=None):
        off, size = vec_at[name]
        return vec_ref[:, off + lo:off + (size if hi is None else hi)]

    @pl.when(jnp.logical_and(pl.program_id(0) == 0, pl.program_id(1) == 0))
    def _():
        u_s[:, ts:ts + HALO, :] = jnp.zeros((n_slab, HALO, LANES), F32)
        v_s[:, ts:ts + HALO, :] = jnp.zeros((n_slab, HALO, LANES), F32)
        for cb in range(n_slab):
            for part, src in enumerate((glu_a, glu_g)):
                w_glu_s[:, (2 * cb + part) * LANES:(2 * cb + part + 1) * LANES] = (
                    w_in_ref[:, src + cb * LANES:src + (cb + 1) * LANES])
        w_pool_s[...] = jnp.zeros(w_pool_s.shape, BF16)
        for grp in range(n_slab):
            k, lo = grp // 2, (grp % 2) * LANES
            w_pool_s[k, lo:lo + LANES, lo:lo + LANES] = w_pool_ref[grp].astype(BF16)

    shift = mod_ref[:, 0:d]
    scale1 = 1.0 + mod_ref[:, d:2 * d]
    gate = mod_ref[:, 2 * d:3 * d] * (1.0 / DEEPNORM_ALPHA)
    eps_post = LN_EPS / (DEEPNORM_ALPHA * DEEPNORM_ALPHA)
    never = pl.program_id(0) < 0

    def tile(j, carry):
        r0 = pl.multiple_of(j * ts, ts)
        first = jnp.logical_and(pl.program_id(1) == 0, j == 0)

        def x_rows(r, n):
            return x_ref[0, pl.ds(r0 + r, n), :]

        u_s[:, 0:HALO, :] = jnp.where(first, 0.0, u_s[:, ts:ts + HALO, :])
        v_s[:, 0:HALO, :] = jnp.where(first, 0.0, v_s[:, ts:ts + HALO, :])

        for r in range(0, ts, LN_ROWS):
            h = _layer_norm(x_rows(r, LN_ROWS)) * _rows(scale1, LN_ROWS) + _rows(shift, LN_ROWS)
            h_s[r:r + LN_ROWS, :] = h.astype(BF16)

        h = h_s[...]

        def project(lo, hi):
            return (jnp.dot(h, w_in_ref[:, lo:hi], preferred_element_type=F32)
                    + _rows(vec("b_in", lo, hi), ts))

        u = project(0, dp)
        for g in range(n_slab):
            u_s[g, HALO:HALO + ts, :] = u[:, g * LANES:(g + 1) * LANES]

        def glu(cb):
            bias = jnp.concatenate([vec("b_in", glu_a + cb * LANES, glu_a + (cb + 1) * LANES),
                                    vec("b_in", glu_g + cb * LANES, glu_g + (cb + 1) * LANES)], axis=1)
            ag = (jnp.dot(h, w_glu_s[:, 2 * cb * LANES:2 * (cb + 1) * LANES], preferred_element_type=F32)
                  + _rows(bias, ts))
            v_s[cb, HALO:HALO + ts, :] = ag[:, 0:LANES] * jax.nn.sigmoid(ag[:, LANES:])

        def conv(cb, after=None):
            lanes = slice(cb * LANES, (cb + 1) * LANES)
            taps = [w_dw_ref[k, :, lanes] for k in range(CONV_WIDTH)]
            bias = vec("b_dw", cb * LANES, (cb + 1) * LANES)
            for t0 in range(0, ts, BLOCK_ROWS):
                acc = [bias] * ROW_STRIDE
                for qi in range(1 - CONV_WIDTH, ROW_STRIDE):
                    vin = _strided_rows(v_s, cb, HALO + t0 + qi)
                    if after is not None and qi == 1 - CONV_WIDTH:
                        vin = jnp.where(never, after, vin)
                    for q in range(ROW_STRIDE):
                        delay = q - qi
                        if 0 <= delay < CONV_WIDTH:
                            acc[q] = acc[q] + vin * taps[CONV_WIDTH - 1 - delay]
                for q in range(ROW_STRIDE):
                    cn_s[cb, pl.ds(t0 + q, SUBLANES, stride=ROW_STRIDE), :] = acc[q]

        for cb in range(n_slab):
            glu(cb)
        conv(0)
        za = _silu(project(dp, 2 * dp)) * _rows(vec("ls"), ts)
        zg_s[:, 0:dp] = za
        conv(1)
        zb = _silu(project(z_b, z_b + dp))
        zg_s[:, dp:] = zb
        for cb in range(2, n_slab):
            conv(cb, after=(za if cb % 2 == 0 else zb)[-SUBLANES:, -LANES:])

        for t0 in range(0, ts, BLOCK_ROWS):
            for g, w in enumerate(POOL_WINDOWS):
                s = {q: _strided_rows(u_s, g, HALO + t0 + q) for q in range(1 - w, ROW_STRIDE)}
                cur = s
                span = 1
                while span < w:
                    s = {q: s[q] + s[q - span] for q in s if q - span in s}
                    span *= 2
                for q in range(ROW_STRIDE):
                    inv = jnp.where(first, invc_ref[g, q], 1.0 / w) if t0 == 0 else 1.0 / w
                    pn_s[g, pl.ds(t0 + q, SUBLANES, stride=ROW_STRIDE), :] = s[q] * inv - cur[q]

        p = jnp.concatenate([pn_s[g] for g in range(n_slab)], axis=1).astype(BF16)
        half = dp // 2
        for k in range(2):
            cols = slice(k * half, (k + 1) * half)
            ya = (jnp.dot(p[:, cols], w_pool_s[k], preferred_element_type=F32)
                  + _rows(vec("b_pool", k * half, (k + 1) * half), ts)) * zg_s[:, cols]
            cat_s[:, cols] = ya.astype(BF16)

        for r in range(0, ts, LN_ROWS):
            cv = jnp.concatenate([cn_s[cb, r:r + LN_ROWS, :] for cb in range(n_slab)], axis=1)
            vn = _layer_norm(cv) * _rows(vec("lncg"), LN_ROWS) + _rows(vec("lncb"), LN_ROWS)
            c_s[r:r + LN_ROWS, :] = _silu(vn).astype(BF16)

        yb = (jnp.dot(c_s[...], w_pw_ref[...], preferred_element_type=F32)
              + _rows(vec("b_pw"), ts)) * zg_s[:, dp:]
        cat_s[:, dp:] = yb.astype(BF16)

        y = (jnp.dot(cat_s[...], w_out_ref[...], preferred_element_type=F32)
             + _rows(vec("b_out"), ts))
        yo_s[...] = x_rows(0, ts) + _rows(gate, ts) * y
        for r in range(0, ts, LN_ROWS):
            t = yo_s[r:r + LN_ROWS, :]
            o = _layer_norm(t, eps_post) * _rows(vec("lnpg"), LN_ROWS) + _rows(vec("lnpb"), LN_ROWS)
            out_ref[0, pl.ds(r0 + r, LN_ROWS), :] = o
        return carry

    lax.fori_loop(0, TILES_PER_STEP, tile, 0)


def _inv_count_table():
    q = np.arange(ROW_STRIDE, dtype=np.float64)[:, None]
    s = np.arange(SUBLANES, dtype=np.float64)[None, :]
    t1 = q + ROW_STRIDE * s + 1.0
    tab = np.stack([1.0 / np.minimum(t1, float(w)) for w in POOL_WINDOWS])
    return jnp.asarray(np.broadcast_to(tab[..., None], tab.shape + (LANES,)), dtype=F32)


def _const_spec(shape):
    return pl.BlockSpec(shape, lambda b, i: (0,) * len(shape))


@jax.jit
def kernel(x, c, w_ada, b_ada, w_in, b_in, w_pool, b_pool, ls_pool, w_dw, b_dw, ln_conv_g, ln_conv_b,
           w_pw, b_pw, w_out, b_out, ln_post_g, ln_post_b):
    bsz, seq, d = x.shape
    n_grp, gw = w_pool.shape[0], w_pool.shape[1]
    dp = n_grp * gw
    dc = w_pw.shape[0]
    d_in = w_in.shape[1]
    step_rows = SEQ_TILE * TILES_PER_STEP
    assert seq % step_rows == 0 and SEQ_TILE % BLOCK_ROWS == 0
    assert gw == LANES and n_grp == len(POOL_WINDOWS) and dp == dc
    assert d_in == 2 * dp + 3 * dc and w_dw.shape[0] == CONV_WIDTH
    assert HALO >= CONV_WIDTH - 1 and HALO >= max(POOL_WINDOWS) - 1
    n_slab = dp // LANES

    vectors = (b_in, b_pool.reshape(-1), ls_pool, b_dw, ln_conv_g, ln_conv_b, b_pw, b_out,
               ln_post_g, ln_post_b)
    vec_at, off = {}, 0
    for name, v in zip(_VEC_NAMES, vectors):
        assert v.shape[0] % LANES == 0
        vec_at[name] = (off, v.shape[0])
        off += v.shape[0]

    def whole(shape):
        return pl.BlockSpec(shape, lambda j: (0,) * len(shape))

    mod, vec8, w_dw8 = pl.pallas_call(
        functools.partial(_mod_kernel, vec_at),
        grid=(3 * d // MOD_COLS,),
        in_specs=[whole((bsz, d)),
                  pl.BlockSpec((d // 2, MOD_COLS), lambda j: (0, j)),
                  pl.BlockSpec((d // 2, MOD_COLS), lambda j: (1, j)),
                  pl.BlockSpec((1, MOD_COLS), lambda j: (0, j)),
                  whole(w_dw.shape)] + [whole((1, v.shape[0])) for v in vectors],
        out_specs=[pl.BlockSpec((bsz * SUBLANES, MOD_COLS), lambda j: (0, j)),
                   whole((SUBLANES, off)), whole((CONV_WIDTH, SUBLANES, dc))],
        out_shape=[jax.ShapeDtypeStruct((bsz * SUBLANES, 3 * d), F32),
                   jax.ShapeDtypeStruct((SUBLANES, off), F32),
                   jax.ShapeDtypeStruct((CONV_WIDTH, SUBLANES, dc), F32)],
        compiler_params=pltpu.CompilerParams(dimension_semantics=("arbitrary",)),
        name="adaln_mod",
    )(c, w_ada, w_ada, b_ada[None, :], w_dw, *(v.astype(F32).reshape(1, -1) for v in vectors))

    operands = (x, mod, vec8, w_in.astype(BF16), w_pool, _inv_count_table(), w_dw8,
                w_pw.astype(BF16), w_out.astype(BF16))
    in_specs = [
        pl.BlockSpec((1, step_rows, d), lambda b, i: (b, i, 0)),
        pl.BlockSpec((SUBLANES, 3 * d), lambda b, i: (b, 0)),
    ] + [_const_spec(op.shape) for op in operands[2:]]

    return pl.pallas_call(
        functools.partial(_layer_kernel, vec_at),
        grid=(bsz, seq // step_rows),
        in_specs=in_specs,
        out_specs=pl.BlockSpec((1, step_rows, d), lambda b, i: (b, i, 0)),
        out_shape=jax.ShapeDtypeStruct((bsz, seq, d), x.dtype),
        scratch_shapes=[
            pltpu.VMEM((SEQ_TILE, d), BF16),
            pltpu.VMEM((n_slab, HALO + SEQ_TILE, LANES), F32),
            pltpu.VMEM((SEQ_TILE, dp + dc), F32),
            pltpu.VMEM((n_slab, SEQ_TILE, LANES), F32),
            pltpu.VMEM((n_slab, HALO + SEQ_TILE, LANES), F32),
            pltpu.VMEM((n_slab, SEQ_TILE, LANES), F32),
            pltpu.VMEM((SEQ_TILE, dc), BF16),
            pltpu.VMEM((SEQ_TILE, dp + dc), BF16),
            pltpu.VMEM((SEQ_TILE, d), F32),
            pltpu.VMEM((d, 2 * dc), BF16),
            pltpu.VMEM((n_grp // 2, 2 * gw, 2 * gw), BF16),
        ],
        compiler_params=pltpu.CompilerParams(
            dimension_semantics=("arbitrary", "arbitrary"),
            vmem_limit_bytes=VMEM_LIMIT_BYTES,
        ),
        name="hybrid_layer",
    )(*operands)
```

```python
import functools

import numpy as np
import jax
import jax.numpy as jnp
from jax import lax
from jax.experimental import pallas as pl
from jax.experimental.pallas import tpu as pltpu

F32 = jnp.float32
BF16 = jnp.bfloat16

POOL_WINDOWS = (2, 4, 8, 16)
CONV_WIDTH = 31
LN_EPS = 1e-5
DEEPNORM_ALPHA = 2.0 ** 0.25

SUBLANES = 8
LANES = 128
SEQ_TILE = 512
TILES_PER_STEP = 2
HALO = 32
ROW_STRIDE = 4
BLOCK_ROWS = SUBLANES * ROW_STRIDE
LN_ROWS = 16
MOD_COLS = 768
VMEM_LIMIT_BYTES = 48 * 1024 * 1024

_VEC_NAMES = ("b_in", "b_pool", "ls", "b_dw", "lncg", "lncb", "b_pw", "b_out", "lnpg", "lnpb")


def _rows(p8, n):
    return p8 if n == SUBLANES else jnp.concatenate([p8] * (n // SUBLANES), axis=0)


def _silu(z):
    return z * jax.nn.sigmoid(z)


def _layer_norm(t, eps=LN_EPS):
    mu = jnp.mean(t, axis=-1, keepdims=True)
    tc = t - mu
    var = jnp.mean(tc * tc, axis=-1, keepdims=True)
    return tc * lax.rsqrt(var + eps)


def _strided_rows(ref, slab, start):
    return ref[slab, pl.ds(start, SUBLANES, stride=ROW_STRIDE), :]


def _mod_kernel(vec_at, c_ref, w_ref, b_ref, w_dw_ref, *refs):
    vec_refs, (o_ref, vec8_ref, w_dw8_ref) = refs[:-3], refs[-3:]
    bsz = c_ref.shape[0]
    c = jnp.concatenate([jnp.broadcast_to(c_ref[b:b + 1, :], (SUBLANES, c_ref.shape[1]))
                         for b in range(bsz)], axis=0)
    s = _silu(c).astype(BF16)
    o_ref[...] = jnp.dot(s, w_ref[...].astype(BF16), preferred_element_type=F32) + b_ref[...]

    @pl.when(pl.program_id(0) == 0)
    def _():
        for name, ref in zip(_VEC_NAMES, vec_refs):
            off, size = vec_at[name]
            vec8_ref[:, off:off + size] = jnp.broadcast_to(ref[...], (SUBLANES, size))
        w_dw8_ref[...] = jnp.broadcast_to(w_dw_ref[...], w_dw8_ref.shape)


def _layer_kernel(vec_at, x_ref, mod_ref, vec_ref, w_in_ref, w_pool_ref, invc_ref, w_dw_ref, w_pw_ref,
                  w_out_ref, out_ref, h_s, u_s, zg_s, pn_s, v_s, cn_s, c_s, cat_s, yo_s, w_glu_s, w_pool_s):
    ts = SEQ_TILE
    d = x_ref.shape[-1]
    n_slab = u_s.shape[0]
    dp = n_slab * LANES
    glu_a, glu_g, z_b = 2 * dp, 3 * dp, 4 * dp

    def vec(name, lo=0, hi=None):
        off, size = vec_at[name]
        return vec_ref[:, off + lo:off + (size if hi is None else hi)]

    @pl.when(jnp.logical_and(pl.program_id(0) == 0, pl.program_id(1) == 0))
    def _():
        u_s[:, ts:ts + HALO, :] = jnp.zeros((n_slab, HALO, LANES), F32)
        v_s[:, ts:ts + HALO, :] = jnp.zeros((n_slab, HALO, LANES), F32)
        for cb in range(n_slab):
            for part, src in enumerate((glu_a, glu_g)):
                w_glu_s[:, (2 * cb + part) * LANES:(2 * cb + part + 1) * LANES] = (
                    w_in_ref[:, src + cb * LANES:src + (cb + 1) * LANES])
        w_pool_s[...] = jnp.zeros(w_pool_s.shape, BF16)
        for grp in range(n_slab):
            k, lo = grp // 2, (grp % 2) * LANES
            w_pool_s[k, lo:lo + LANES, lo:lo + LANES] = w_pool_ref[grp].astype(BF16)

    shift = mod_ref[0, :, 0:d]
    scale1 = 1.0 + mod_ref[0, :, d:2 * d]
    gate = mod_ref[0, :, 2 * d:3 * d] * (1.0 / DEEPNORM_ALPHA)
    eps_post = LN_EPS / (DEEPNORM_ALPHA * DEEPNORM_ALPHA)
    never = pl.program_id(0) < 0

    def tile(j, carry):
        r0 = pl.multiple_of(j * ts, ts)
        first = jnp.logical_and(pl.program_id(1) == 0, j == 0)

        def x_rows(r, n):
            return x_ref[0, pl.ds(r0 + r, n), :]

        u_s[:, 0:HALO, :] = jnp.where(first, 0.0, u_s[:, ts:ts + HALO, :])
        v_s[:, 0:HALO, :] = jnp.where(first, 0.0, v_s[:, ts:ts + HALO, :])

        for r in range(0, ts, LN_ROWS):
            h = _layer_norm(x_rows(r, LN_ROWS)) * _rows(scale1, LN_ROWS) + _rows(shift, LN_ROWS)
            h_s[r:r + LN_ROWS, :] = h.astype(BF16)

        h = h_s[...]

        def project(lo, hi):
            return (jnp.dot(h, w_in_ref[:, lo:hi], preferred_element_type=F32)
                    + _rows(vec("b_in", lo, hi), ts))

        u = project(0, dp)
        for g in range(n_slab):
            u_s[g, HALO:HALO + ts, :] = u[:, g * LANES:(g + 1) * LANES]

        def glu(cb):
            bias = jnp.concatenate([vec("b_in", glu_a + cb * LANES, glu_a + (cb + 1) * LANES),
                                    vec("b_in", glu_g + cb * LANES, glu_g + (cb + 1) * LANES)], axis=1)
            ag = (jnp.dot(h, w_glu_s[:, 2 * cb * LANES:2 * (cb + 1) * LANES], preferred_element_type=F32)
                  + _rows(bias, ts))
            v_s[cb, HALO:HALO + ts, :] = ag[:, 0:LANES] * jax.nn.sigmoid(ag[:, LANES:])

        def conv(cb, rows, after=None):
            lanes = slice(cb * LANES, (cb + 1) * LANES)
            taps = [w_dw_ref[k, :, lanes] for k in range(CONV_WIDTH)]
            bias = vec("b_dw", cb * LANES, (cb + 1) * LANES)
            for t0 in range(rows.start, rows.stop, BLOCK_ROWS):
                acc = [bias] * ROW_STRIDE
                for qi in range(1 - CONV_WIDTH, ROW_STRIDE):
                    vin = _strided_rows(v_s, cb, HALO + t0 + qi)
                    if after is not None and qi == 1 - CONV_WIDTH:
                        vin = jnp.where(never, after, vin)
                    for q in range(ROW_STRIDE):
                        delay = q - qi
                        if 0 <= delay < CONV_WIDTH:
                            acc[q] = acc[q] + vin * taps[CONV_WIDTH - 1 - delay]
                for q in range(ROW_STRIDE):
                    cn_s[cb, pl.ds(t0 + q, SUBLANES, stride=ROW_STRIDE), :] = acc[q]

        def tail_matmuls(rows):
            n = rows.stop - rows.start
            for r in range(rows.start, rows.stop, LN_ROWS):
                cv = jnp.concatenate([cn_s[cb, r:r + LN_ROWS, :] for cb in range(n_slab)], axis=1)
                vn = _layer_norm(cv) * _rows(vec("lncg"), LN_ROWS) + _rows(vec("lncb"), LN_ROWS)
                c_s[r:r + LN_ROWS, :] = _silu(vn).astype(BF16)
            yb = (jnp.dot(c_s[rows, :], w_pw_ref[...], preferred_element_type=F32)
                  + _rows(vec("b_pw"), n)) * zg_s[rows, dp:]
            cat_s[rows, dp:] = yb.astype(BF16)
            y = (jnp.dot(cat_s[rows, :], w_out_ref[...], preferred_element_type=F32)
                 + _rows(vec("b_out"), n))
            yo_s[rows, :] = x_rows(rows.start, n) + _rows(gate, n) * y
            return yb[-SUBLANES:, -LANES:], y[-SUBLANES:, -LANES:]

        def norm_post(rows):
            for r in range(rows.start, rows.stop, LN_ROWS):
                t = yo_s[r:r + LN_ROWS, :]
                o = _layer_norm(t, eps_post) * _rows(vec("lnpg"), LN_ROWS) + _rows(vec("lnpb"), LN_ROWS)
                out_ref[0, pl.ds(r0 + r, LN_ROWS), :] = o

        top, bottom = slice(0, ts // 2), slice(ts // 2, ts)
        for cb in range(n_slab):
            glu(cb)
        conv(0, top)
        za = _silu(project(dp, 2 * dp)) * _rows(vec("ls"), ts)
        zg_s[:, 0:dp] = za
        conv(1, top)
        zb = _silu(project(z_b, z_b + dp))
        zg_s[:, dp:] = zb
        conv(2, top, after=za[-SUBLANES:, -LANES:])
        conv(3, top, after=zb[-SUBLANES:, -LANES:])

        for t0 in range(0, ts, BLOCK_ROWS):
            for g, w in enumerate(POOL_WINDOWS):
                s = {q: _strided_rows(u_s, g, HALO + t0 + q) for q in range(1 - w, ROW_STRIDE)}
                cur = s
                span = 1
                while span < w:
                    s = {q: s[q] + s[q - span] for q in s if q - span in s}
                    span *= 2
                for q in range(ROW_STRIDE):
                    inv = jnp.where(first, invc_ref[g, q], 1.0 / w) if t0 == 0 else 1.0 / w
                    pn_s[g, pl.ds(t0 + q, SUBLANES, stride=ROW_STRIDE), :] = s[q] * inv - cur[q]

        p = jnp.concatenate([pn_s[g] for g in range(n_slab)], axis=1).astype(BF16)
        half = dp // 2
        for k in range(2):
            cols = slice(k * half, (k + 1) * half)
            ya = (jnp.dot(p[:, cols], w_pool_s[k], preferred_element_type=F32)
                  + _rows(vec("b_pool", k * half, (k + 1) * half), ts)) * zg_s[:, cols]
            cat_s[:, cols] = ya.astype(BF16)

        pw_top, out_top = tail_matmuls(top)
        conv(0, bottom)
        conv(1, bottom)
        conv(2, bottom, after=pw_top)
        conv(3, bottom, after=out_top)
        norm_post(top)
        tail_matmuls(bottom)
        norm_post(bottom)
        return carry

    lax.fori_loop(0, TILES_PER_STEP, tile, 0)


def _inv_count_table():
    q = np.arange(ROW_STRIDE, dtype=np.float64)[:, None]
    s = np.arange(SUBLANES, dtype=np.float64)[None, :]
    t1 = q + ROW_STRIDE * s + 1.0
    tab = np.stack([1.0 / np.minimum(t1, float(w)) for w in POOL_WINDOWS])
    return jnp.asarray(np.broadcast_to(tab[..., None], tab.shape + (LANES,)), dtype=F32)


def _const_spec(shape):
    return pl.BlockSpec(shape, lambda b, i: (0,) * len(shape))


@jax.jit
def kernel(x, c, w_ada, b_ada, w_in, b_in, w_pool, b_pool, ls_pool, w_dw, b_dw, ln_conv_g, ln_conv_b,
           w_pw, b_pw, w_out, b_out, ln_post_g, ln_post_b):
    bsz, seq, d = x.shape
    n_grp, gw = w_pool.shape[0], w_pool.shape[1]
    dp = n_grp * gw
    dc = w_pw.shape[0]
    d_in = w_in.shape[1]
    step_rows = SEQ_TILE * TILES_PER_STEP
    assert seq % step_rows == 0 and SEQ_TILE % BLOCK_ROWS == 0
    assert gw == LANES and n_grp == len(POOL_WINDOWS) and dp == dc
    assert d_in == 2 * dp + 3 * dc and w_dw.shape[0] == CONV_WIDTH
    assert HALO >= CONV_WIDTH - 1 and HALO >= max(POOL_WINDOWS) - 1
    n_slab = dp // LANES

    vectors = (b_in, b_pool.reshape(-1), ls_pool, b_dw, ln_conv_g, ln_conv_b, b_pw, b_out,
               ln_post_g, ln_post_b)
    vec_at, off = {}, 0
    for name, v in zip(_VEC_NAMES, vectors):
        assert v.shape[0] % LANES == 0
        vec_at[name] = (off, v.shape[0])
        off += v.shape[0]

    def whole(shape):
        return pl.BlockSpec(shape, lambda j: (0,) * len(shape))

    mod, vec8, w_dw8 = pl.pallas_call(
        functools.partial(_mod_kernel, vec_at),
        grid=(3 * d // MOD_COLS,),
        in_specs=[whole((bsz, d)),
                  pl.BlockSpec((d, MOD_COLS), lambda j: (0, j)),
                  pl.BlockSpec((1, MOD_COLS), lambda j: (0, j)),
                  whole(w_dw.shape)] + [whole((1, v.shape[0])) for v in vectors],
        out_specs=[pl.BlockSpec((bsz * SUBLANES, MOD_COLS), lambda j: (0, j)),
                   whole((SUBLANES, off)), whole((CONV_WIDTH, SUBLANES, dc))],
        out_shape=[jax.ShapeDtypeStruct((bsz * SUBLANES, 3 * d), F32),
                   jax.ShapeDtypeStruct((SUBLANES, off), F32),
                   jax.ShapeDtypeStruct((CONV_WIDTH, SUBLANES, dc), F32)],
        compiler_params=pltpu.CompilerParams(dimension_semantics=("arbitrary",)),
        name="adaln_mod",
    )(c, w_ada, b_ada[None, :], w_dw, *(v.astype(F32).reshape(1, -1) for v in vectors))
    mod = mod.reshape(bsz, SUBLANES, 3 * d)

    operands = (x, mod, vec8, w_in.astype(BF16), w_pool, _inv_count_table(), w_dw8,
                w_pw.astype(BF16), w_out.astype(BF16))
    in_specs = [
        pl.BlockSpec((1, step_rows, d), lambda b, i: (b, i, 0)),
        pl.BlockSpec((1, SUBLANES, 3 * d), lambda b, i: (b, 0, 0)),
    ] + [_const_spec(op.shape) for op in operands[2:]]

    return pl.pallas_call(
        functools.partial(_layer_kernel, vec_at),
        grid=(bsz, seq // step_rows),
        in_specs=in_specs,
        out_specs=pl.BlockSpec((1, step_rows, d), lambda b, i: (b, i, 0)),
        out_shape=jax.ShapeDtypeStruct((bsz, seq, d), x.dtype),
        scratch_shapes=[
            pltpu.VMEM((SEQ_TILE, d), BF16),
            pltpu.VMEM((n_slab, HALO + SEQ_TILE, LANES), F32),
            pltpu.VMEM((SEQ_TILE, dp + dc), F32),
            pltpu.VMEM((n_slab, SEQ_TILE, LANES), F32),
            pltpu.VMEM((n_slab, HALO + SEQ_TILE, LANES), F32),
            pltpu.VMEM((n_slab, SEQ_TILE, LANES), F32),
            pltpu.VMEM((SEQ_TILE, dc), BF16),
            pltpu.VMEM((SEQ_TILE, dp + dc), BF16),
            pltpu.VMEM((SEQ_TILE, d), F32),
            pltpu.VMEM((d, 2 * dc), BF16),
            pltpu.VMEM((n_grp // 2, 2 * gw, 2 * gw), BF16),
        ],
        compiler_params=pltpu.CompilerParams(
            dimension_semantics=("arbitrary", "arbitrary"),
            vmem_limit_bytes=VMEM_LIMIT_BYTES,
        ),
        name="hybrid_layer",
    )(*operands)
```

```python
import functools

import numpy as np
import jax
import jax.numpy as jnp
from jax import lax
from jax.experimental import pallas as pl
from jax.experimental.pallas import tpu as pltpu

F32 = jnp.float32
BF16 = jnp.bfloat16

POOL_WINDOWS = (2, 4, 8, 16)
CONV_WIDTH = 31
LN_EPS = 1e-5
DEEPNORM_ALPHA = 2.0 ** 0.25

SUBLANES = 8
LANES = 128
SEQ_TILE = 512
TILES_PER_STEP = 2
HALO = 32
ROW_STRIDE = 4
BLOCK_ROWS = SUBLANES * ROW_STRIDE
LN_ROWS = 16
MOD_COLS = 768
VMEM_LIMIT_BYTES = 48 * 1024 * 1024

_VEC_NAMES = ("b_in", "b_pool", "ls", "b_dw", "lncg", "lncb", "b_pw", "b_out", "lnpg", "lnpb")


def _rows(p8, n):
    return p8 if n == SUBLANES else jnp.concatenate([p8] * (n // SUBLANES), axis=0)


def _silu(z):
    return z * jax.nn.sigmoid(z)


def _layer_norm(t, eps=LN_EPS):
    mu = jnp.mean(t, axis=-1, keepdims=True)
    tc = t - mu
    var = jnp.mean(tc * tc, axis=-1, keepdims=True)
    return tc * lax.rsqrt(var + eps)


def _strided_rows(ref, slab, start):
    return ref[slab, pl.ds(start, SUBLANES, stride=ROW_STRIDE), :]


def _mod_kernel(vec_at, c_ref, w_ref, b_ref, w_dw_ref, *refs):
    vec_refs, (o_ref, vec8_ref, w_dw8_ref) = refs[:-3], refs[-3:]
    bsz = c_ref.shape[0]
    c = jnp.concatenate([jnp.broadcast_to(c_ref[b:b + 1, :], (SUBLANES, c_ref.shape[1]))
                         for b in range(bsz)], axis=0)
    s = _silu(c).astype(BF16)
    o_ref[...] = jnp.dot(s, w_ref[...].astype(BF16), preferred_element_type=F32) + b_ref[...]

    @pl.when(pl.program_id(0) == 0)
    def _():
        for name, ref in zip(_VEC_NAMES, vec_refs):
            off, size = vec_at[name]
            vec8_ref[:, off:off + size] = jnp.broadcast_to(ref[...], (SUBLANES, size))
        w_dw8_ref[...] = jnp.broadcast_to(w_dw_ref[...], w_dw8_ref.shape)


def _layer_kernel(vec_at, x_ref, mod_ref, vec_ref, w_in_ref, w_pool_ref, invc_ref, w_dw_ref, w_pw_ref,
                  w_out_ref, out_ref, h_s, u_s, zg_s, pn_s, v_s, cn_s, c_s, cat_s, yo_s, w_glu_s, w_pool_s):
    ts = SEQ_TILE
    d = x_ref.shape[-1]
    n_slab = u_s.shape[0]
    dp = n_slab * LANES
    glu_a, glu_g, z_b = 2 * dp, 3 * dp, 4 * dp

    def vec(name, lo=0, hi=None):
        off, size = vec_at[name]
        return vec_ref[:, off + lo:off + (size if hi is None else hi)]

    @pl.when(jnp.logical_and(pl.program_id(0) == 0, pl.program_id(1) == 0))
    def _():
        u_s[:, ts:ts + HALO, :] = jnp.zeros((n_slab, HALO, LANES), F32)
        v_s[:, ts:ts + HALO, :] = jnp.zeros((n_slab, HALO, LANES), F32)
        for cb in range(n_slab):
            for part, src in enumerate((glu_a, glu_g)):
                w_glu_s[:, (2 * cb + part) * LANES:(2 * cb + part + 1) * LANES] = (
                    w_in_ref[:, src + cb * LANES:src + (cb + 1) * LANES])
        w_pool_s[...] = jnp.zeros(w_pool_s.shape, BF16)
        for grp in range(n_slab):
            k, lo = grp // 2, (grp % 2) * LANES
            w_pool_s[k, lo:lo + LANES, lo:lo + LANES] = w_pool_ref[grp].astype(BF16)

    shift = mod_ref[0, :, 0:d]
    scale1 = 1.0 + mod_ref[0, :, d:2 * d]
    gate = mod_ref[0, :, 2 * d:3 * d] * (1.0 / DEEPNORM_ALPHA)
    eps_post = LN_EPS / (DEEPNORM_ALPHA * DEEPNORM_ALPHA)
    never = pl.program_id(0) < 0

    def tile(j, carry):
        r0 = pl.multiple_of(j * ts, ts)
        first = jnp.logical_and(pl.program_id(1) == 0, j == 0)

        def x_rows(r, n):
            return x_ref[0, pl.ds(r0 + r, n), :]

        u_s[:, 0:HALO, :] = jnp.where(first, 0.0, u_s[:, ts:ts + HALO, :])
        v_s[:, 0:HALO, :] = jnp.where(first, 0.0, v_s[:, ts:ts + HALO, :])

        for r in range(0, ts, LN_ROWS):
            h = _layer_norm(x_rows(r, LN_ROWS)) * _rows(scale1, LN_ROWS) + _rows(shift, LN_ROWS)
            h_s[r:r + LN_ROWS, :] = h.astype(BF16)

        h = h_s[...]

        def project(lo, hi):
            return (jnp.dot(h, w_in_ref[:, lo:hi], preferred_element_type=F32)
                    + _rows(vec("b_in", lo, hi), ts))

        u = project(0, dp)
        for g in range(n_slab):
            u_s[g, HALO:HALO + ts, :] = u[:, g * LANES:(g + 1) * LANES]

        def glu(cb):
            bias = jnp.concatenate([vec("b_in", glu_a + cb * LANES, glu_a + (cb + 1) * LANES),
                                    vec("b_in", glu_g + cb * LANES, glu_g + (cb + 1) * LANES)], axis=1)
            ag = (jnp.dot(h, w_glu_s[:, 2 * cb * LANES:2 * (cb + 1) * LANES], preferred_element_type=F32)
                  + _rows(bias, ts))
            v_s[cb, HALO:HALO + ts, :] = ag[:, 0:LANES] * jax.nn.sigmoid(ag[:, LANES:])

        def conv(cb, after=None):
            lanes = slice(cb * LANES, (cb + 1) * LANES)
            taps = [w_dw_ref[k, :, lanes] for k in range(CONV_WIDTH)]
            bias = vec("b_dw", cb * LANES, (cb + 1) * LANES)
            for t0 in range(0, ts, BLOCK_ROWS):
                acc = [bias] * ROW_STRIDE
                for qi in range(1 - CONV_WIDTH, ROW_STRIDE):
                    vin = _strided_rows(v_s, cb, HALO + t0 + qi)
                    if after is not None and qi == 1 - CONV_WIDTH:
                        vin = jnp.where(never, after, vin)
                    for q in range(ROW_STRIDE):
                        delay = q - qi
                        if 0 <= delay < CONV_WIDTH:
                            acc[q] = acc[q] + vin * taps[CONV_WIDTH - 1 - delay]
                for q in range(ROW_STRIDE):
                    cn_s[cb, pl.ds(t0 + q, SUBLANES, stride=ROW_STRIDE), :] = acc[q]

        for cb in range(n_slab):
            glu(cb)
        conv(0)
        za = _silu(project(dp, 2 * dp)) * _rows(vec("ls"), ts)
        zg_s[:, 0:dp] = za
        conv(1)
        zb = _silu(project(z_b, z_b + dp))
        zg_s[:, dp:] = zb
        for cb in range(2, n_slab):
            conv(cb, after=(za if cb % 2 == 0 else zb)[-SUBLANES:, -LANES:])

        for t0 in range(0, ts, BLOCK_ROWS):
            for g, w in enumerate(POOL_WINDOWS):
                s = {q: _strided_rows(u_s, g, HALO + t0 + q) for q in range(1 - w, ROW_STRIDE)}
                cur = s
                span = 1
                while span < w:
                    s = {q: s[q] + s[q - span] for q in s if q - span in s}
                    span *= 2
                for q in range(ROW_STRIDE):
                    inv = jnp.where(first, invc_ref[g, q], 1.0 / w) if t0 == 0 else 1.0 / w
                    pn_s[g, pl.ds(t0 + q, SUBLANES, stride=ROW_STRIDE), :] = s[q] * inv - cur[q]

        p = jnp.concatenate([pn_s[g] for g in range(n_slab)], axis=1).astype(BF16)
        half = dp // 2
        for k in range(2):
            cols = slice(k * half, (k + 1) * half)
            ya = (jnp.dot(p[:, cols], w_pool_s[k], preferred_element_type=F32)
                  + _rows(vec("b_pool", k * half, (k + 1) * half), ts)) * zg_s[:, cols]
            cat_s[:, cols] = ya.astype(BF16)

        for r in range(0, ts, LN_ROWS):
            cv = jnp.concatenate([cn_s[cb, r:r + LN_ROWS, :] for cb in range(n_slab)], axis=1)
            vn = _layer_norm(cv) * _rows(vec("lncg"), LN_ROWS) + _rows(vec("lncb"), LN_ROWS)
            c_s[r:r + LN_ROWS, :] = _silu(vn).astype(BF16)

        yb = (jnp.dot(c_s[...], w_pw_ref[...], preferred_element_type=F32)
              + _rows(vec("b_pw"), ts)) * zg_s[:, dp:]
        cat_s[:, dp:] = yb.astype(BF16)

        y = (jnp.dot(cat_s[...], w_out_ref[...], preferred_element_type=F32)
             + _rows(vec("b_out"), ts))
        yo_s[...] = x_rows(0, ts) + _rows(gate, ts) * y
        for r in range(0, ts, LN_ROWS):
            t = yo_s[r:r + LN_ROWS, :]
            o = _layer_norm(t, eps_post) * _rows(vec("lnpg"), LN_ROWS) + _rows(vec("lnpb"), LN_ROWS)
            out_ref[0, pl.ds(r0 + r, LN_ROWS), :] = o
        return carry

    lax.fori_loop(0, TILES_PER_STEP, tile, 0)


def _inv_count_table():
    q = np.arange(ROW_STRIDE, dtype=np.float64)[:, None]
    s = np.arange(SUBLANES, dtype=np.float64)[None, :]
    t1 = q + ROW_STRIDE * s + 1.0
    tab = np.stack([1.0 / np.minimum(t1, float(w)) for w in POOL_WINDOWS])
    return jnp.asarray(np.broadcast_to(tab[..., None], tab.shape + (LANES,)), dtype=F32)


def _const_spec(shape):
    return pl.BlockSpec(shape, lambda b, i: (0,) * len(shape))


@jax.jit
def kernel(x, c, w_ada, b_ada, w_in, b_in, w_pool, b_pool, ls_pool, w_dw, b_dw, ln_conv_g, ln_conv_b,
           w_pw, b_pw, w_out, b_out, ln_post_g, ln_post_b):
    bsz, seq, d = x.shape
    n_grp, gw = w_pool.shape[0], w_pool.shape[1]
    dp = n_grp * gw
    dc = w_pw.shape[0]
    d_in = w_in.shape[1]
    step_rows = SEQ_TILE * TILES_PER_STEP
    assert seq % step_rows == 0 and SEQ_TILE % BLOCK_ROWS == 0
    assert gw == LANES and n_grp == len(POOL_WINDOWS) and dp == dc
    assert d_in == 2 * dp + 3 * dc and w_dw.shape[0] == CONV_WIDTH
    assert HALO >= CONV_WIDTH - 1 and HALO >= max(POOL_WINDOWS) - 1
    n_slab = dp // LANES

    vectors = (b_in, b_pool.reshape(-1), ls_pool, b_dw, ln_conv_g, ln_conv_b, b_pw, b_out,
               ln_post_g, ln_post_b)
    vec_at, off = {}, 0
    for name, v in zip(_VEC_NAMES, vectors):
        assert v.shape[0] % LANES == 0
        vec_at[name] = (off, v.shape[0])
        off += v.shape[0]

    def whole(shape):
        return pl.BlockSpec(shape, lambda j: (0,) * len(shape))

    mod, vec8, w_dw8 = pl.pallas_call(
        functools.partial(_mod_kernel, vec_at),
        grid=(3 * d // MOD_COLS,),
        in_specs=[whole((bsz, d)),
                  pl.BlockSpec((d, MOD_COLS), lambda j: (0, j)),
                  pl.BlockSpec((1, MOD_COLS), lambda j: (0, j)),
                  whole(w_dw.shape)] + [whole((1, v.shape[0])) for v in vectors],
        out_specs=[pl.BlockSpec((bsz * SUBLANES, MOD_COLS), lambda j: (0, j)),
                   whole((SUBLANES, off)), whole((CONV_WIDTH, SUBLANES, dc))],
        out_shape=[jax.ShapeDtypeStruct((bsz * SUBLANES, 3 * d), F32),
                   jax.ShapeDtypeStruct((SUBLANES, off), F32),
                   jax.ShapeDtypeStruct((CONV_WIDTH, SUBLANES, dc), F32)],
        compiler_params=pltpu.CompilerParams(dimension_semantics=("arbitrary",)),
        name="adaln_mod",
    )(c, w_ada, b_ada[None, :], w_dw, *(v.astype(F32).reshape(1, -1) for v in vectors))
    mod = mod.reshape(bsz, SUBLANES, 3 * d)

    operands = (x, mod, vec8, w_in.astype(BF16), w_pool, _inv_count_table(), w_dw8,
                w_pw.astype(BF16), w_out.astype(BF16))
    in_specs = [
        pl.BlockSpec((1, step_rows, d), lambda b, i: (b, i, 0)),
        pl.BlockSpec((1, SUBLANES, 3 * d), lambda b, i: (b, 0, 0)),
    ] + [_const_spec(op.shape) for op in operands[2:]]

    return pl.pallas_call(
        functools.partial(_layer_kernel, vec_at),
        grid=(bsz, seq // step_rows),
        in_specs=in_specs,
        out_specs=pl.BlockSpec((1, step_rows, d), lambda b, i: (b, i, 0)),
        out_shape=jax.ShapeDtypeStruct((bsz, seq, d), x.dtype),
        scratch_shapes=[
            pltpu.VMEM((SEQ_TILE, d), BF16),
            pltpu.VMEM((n_slab, HALO + SEQ_TILE, LANES), F32),
            pltpu.VMEM((SEQ_TILE, dp + dc), F32),
            pltpu.VMEM((n_slab, SEQ_TILE, LANES), F32),
            pltpu.VMEM((n_slab, HALO + SEQ_TILE, LANES), F32),
            pltpu.VMEM((n_slab, SEQ_TILE, LANES), F32),
            pltpu.VMEM((SEQ_TILE, dc), BF16),
            pltpu.VMEM((SEQ_TILE, dp + dc), BF16),
            pltpu.VMEM((SEQ_TILE, d), F32),
            pltpu.VMEM((d, 2 * dc), BF16),
            pltpu.VMEM((n_grp // 2, 2 * gw, 2 * gw), BF16),
        ],
        compiler_params=pltpu.CompilerParams(
            dimension_semantics=("arbitrary", "arbitrary"),
            vmem_limit_bytes=VMEM_LIMIT_BYTES,
        ),
        name="hybrid_layer",
    )(*operands)
```

```python
import functools

import numpy as np
import jax
import jax.numpy as jnp
from jax import lax
from jax.experimental import pallas as pl
from jax.experimental.pallas import tpu as pltpu

F32 = jnp.float32
BF16 = jnp.bfloat16

POOL_WINDOWS = (2, 4, 8, 16)
CONV_WIDTH = 31
LN_EPS = 1e-5
DEEPNORM_ALPHA = 2.0 ** 0.25

SUBLANES = 8
LANES = 128
SEQ_TILE = 512
TILES_PER_STEP = 2
HALO = 32
ROW_STRIDE = 4
BLOCK_ROWS = SUBLANES * ROW_STRIDE
CONV_GROUP = 4
LN_ROWS = 16
MOD_COLS = 768
VMEM_LIMIT_BYTES = 48 * 1024 * 1024

_VEC_NAMES = ("b_in", "b_pool", "ls", "b_dw", "lncg", "lncb", "b_pw", "b_out", "lnpg", "lnpb")


def _rows(p8, n):
    return p8 if n == SUBLANES else jnp.concatenate([p8] * (n // SUBLANES), axis=0)


def _silu(z):
    return z * jax.nn.sigmoid(z)


def _layer_norm(t, eps=LN_EPS):
    mu = jnp.mean(t, axis=-1, keepdims=True)
    tc = t - mu
    var = jnp.mean(tc * tc, axis=-1, keepdims=True)
    return tc * lax.rsqrt(var + eps)


def _strided_rows(ref, slab, start):
    return ref[slab, pl.ds(start, SUBLANES, stride=ROW_STRIDE), :]


def _mod_kernel(vec_at, c_ref, w_ref, b_ref, w_dw_ref, *refs):
    vec_refs, (o_ref, vec8_ref, w_dw8_ref) = refs[:-3], refs[-3:]
    bsz = c_ref.shape[0]
    c = jnp.concatenate([jnp.broadcast_to(c_ref[b:b + 1, :], (SUBLANES, c_ref.shape[1]))
                         for b in range(bsz)], axis=0)
    s = _silu(c).astype(BF16)
    o_ref[...] = jnp.dot(s, w_ref[...].astype(BF16), preferred_element_type=F32) + b_ref[...]

    @pl.when(pl.program_id(0) == 0)
    def _():
        for name, ref in zip(_VEC_NAMES, vec_refs):
            off, size = vec_at[name]
            vec8_ref[:, off:off + size] = jnp.broadcast_to(ref[...], (SUBLANES, size))
        w_dw8_ref[...] = jnp.broadcast_to(w_dw_ref[...], w_dw8_ref.shape).astype(w_dw8_ref.dtype)


def _layer_kernel(vec_at, x_ref, mod_ref, vec_ref, w_in_ref, w_pool_ref, invc_ref, w_dw_ref, w_pw_ref,
                  w_out_ref, out_ref, h_s, u_s, zg_s, pn_s, v_s, cn_s, c_s, cat_s, yo_s, w_glu_s, w_pool_s):
    ts = SEQ_TILE
    d = x_ref.shape[-1]
    n_slab = u_s.shape[0]
    dp = n_slab * LANES
    glu_a, glu_g, z_b = 2 * dp, 3 * dp, 4 * dp

    def vec(name, lo=0, hi=None):
        off, size = vec_at[name]
        return vec_ref[:, off + lo:off + (size if hi is None else hi)]

    @pl.when(jnp.logical_and(pl.program_id(0) == 0, pl.program_id(1) == 0))
    def _():
        u_s[:, ts:ts + HALO, :] = jnp.zeros((n_slab, HALO, LANES), F32)
        v_s[:, ts:ts + HALO, :] = jnp.zeros((n_slab, HALO, LANES), F32)
        for cb in range(n_slab):
            for part, src in enumerate((glu_a, glu_g)):
                w_glu_s[:, (2 * cb + part) * LANES:(2 * cb + part + 1) * LANES] = (
                    w_in_ref[:, src + cb * LANES:src + (cb + 1) * LANES])
        w_pool_s[...] = jnp.zeros(w_pool_s.shape, BF16)
        for grp in range(n_slab):
            k, lo = grp // 2, (grp % 2) * LANES
            w_pool_s[k, lo:lo + LANES, lo:lo + LANES] = w_pool_ref[grp].astype(BF16)

    shift = mod_ref[0, :, 0:d]
    scale1 = 1.0 + mod_ref[0, :, d:2 * d]
    gate = mod_ref[0, :, 2 * d:3 * d] * (1.0 / DEEPNORM_ALPHA)
    eps_post = LN_EPS / (DEEPNORM_ALPHA * DEEPNORM_ALPHA)
    never = pl.program_id(0) < 0

    def tile(j, carry):
        r0 = pl.multiple_of(j * ts, ts)
        first = jnp.logical_and(pl.program_id(1) == 0, j == 0)

        def x_rows(r, n):
            return x_ref[0, pl.ds(r0 + r, n), :]

        u_s[:, 0:HALO, :] = jnp.where(first, 0.0, u_s[:, ts:ts + HALO, :])
        v_s[:, 0:HALO, :] = jnp.where(first, 0.0, v_s[:, ts:ts + HALO, :])

        for r in range(0, ts, LN_ROWS):
            h = _layer_norm(x_rows(r, LN_ROWS)) * _rows(scale1, LN_ROWS) + _rows(shift, LN_ROWS)
            h_s[r:r + LN_ROWS, :] = h.astype(BF16)

        h = h_s[...]

        def project(lo, hi):
            return (jnp.dot(h, w_in_ref[:, lo:hi], preferred_element_type=F32)
                    + _rows(vec("b_in", lo, hi), ts))

        u = project(0, dp)
        for g in range(n_slab):
            u_s[g, HALO:HALO + ts, :] = u[:, g * LANES:(g + 1) * LANES]

        def glu(cb):
            bias = jnp.concatenate([vec("b_in", glu_a + cb * LANES, glu_a + (cb + 1) * LANES),
                                    vec("b_in", glu_g + cb * LANES, glu_g + (cb + 1) * LANES)], axis=1)
            ag = (jnp.dot(h, w_glu_s[:, 2 * cb * LANES:2 * (cb + 1) * LANES], preferred_element_type=F32)
                  + _rows(bias, ts))
            v_s[cb, HALO:HALO + ts, :] = ag[:, 0:LANES] * jax.nn.sigmoid(ag[:, LANES:])

        def conv(cb, after=None):
            lanes = slice(cb * LANES, (cb + 1) * LANES)
            taps = [w_dw_ref[k, :, lanes] for k in range(CONV_WIDTH)]
            bias = _rows(vec("b_dw", cb * LANES, (cb + 1) * LANES), 2 * SUBLANES)
            for t0 in range(0, ts, 2 * BLOCK_ROWS):
                acc = [bias] * ROW_STRIDE
                part = [None] * ROW_STRIDE
                for qi in range(1 - CONV_WIDTH, ROW_STRIDE):
                    va = _strided_rows(v_s, cb, HALO + t0 + qi)
                    vb = _strided_rows(v_s, cb, HALO + t0 + BLOCK_ROWS + qi)
                    if after is not None and qi == 1 - CONV_WIDTH:
                        va = jnp.where(never, after, va)
                    vin = jnp.concatenate([va, vb], axis=0).astype(BF16)
                    for q in range(ROW_STRIDE):
                        delay = q - qi
                        if 0 <= delay < CONV_WIDTH:
                            term = vin * taps[CONV_WIDTH - 1 - delay]
                            part[q] = term if part[q] is None else part[q] + term
                            if delay % CONV_GROUP == 0:
                                acc[q] = acc[q] + part[q].astype(F32)
                                part[q] = None
                for q in range(ROW_STRIDE):
                    cn_s[cb, pl.ds(t0 + q, SUBLANES, stride=ROW_STRIDE), :] = acc[q][0:SUBLANES]
                    cn_s[cb, pl.ds(t0 + BLOCK_ROWS + q, SUBLANES, stride=ROW_STRIDE), :] = acc[q][SUBLANES:]

        for cb in range(n_slab):
            glu(cb)
        conv(0)
        za = _silu(project(dp, 2 * dp)) * _rows(vec("ls"), ts)
        zg_s[:, 0:dp] = za
        conv(1)
        zb = _silu(project(z_b, z_b + dp))
        zg_s[:, dp:] = zb
        for cb in range(2, n_slab):
            conv(cb, after=(za if cb % 2 == 0 else zb)[-SUBLANES:, -LANES:])

        for t0 in range(0, ts, BLOCK_ROWS):
            for g, w in enumerate(POOL_WINDOWS):
                s = {q: _strided_rows(u_s, g, HALO + t0 + q) for q in range(1 - w, ROW_STRIDE)}
                cur = s
                span = 1
                while span < w:
                    s = {q: s[q] + s[q - span] for q in s if q - span in s}
                    span *= 2
                for q in range(ROW_STRIDE):
                    inv = jnp.where(first, invc_ref[g, q], 1.0 / w) if t0 == 0 else 1.0 / w
                    pn_s[g, pl.ds(t0 + q, SUBLANES, stride=ROW_STRIDE), :] = s[q] * inv - cur[q]

        p = jnp.concatenate([pn_s[g] for g in range(n_slab)], axis=1).astype(BF16)
        half = dp // 2
        for k in range(2):
            cols = slice(k * half, (k + 1) * half)
            ya = (jnp.dot(p[:, cols], w_pool_s[k], preferred_element_type=F32)
                  + _rows(vec("b_pool", k * half, (k + 1) * half), ts)) * zg_s[:, cols]
            cat_s[:, cols] = ya.astype(BF16)

        for r in range(0, ts, LN_ROWS):
            cv = jnp.concatenate([cn_s[cb, r:r + LN_ROWS, :] for cb in range(n_slab)], axis=1)
            vn = _layer_norm(cv) * _rows(vec("lncg"), LN_ROWS) + _rows(vec("lncb"), LN_ROWS)
            c_s[r:r + LN_ROWS, :] = _silu(vn).astype(BF16)

        yb = (jnp.dot(c_s[...], w_pw_ref[...], preferred_element_type=F32)
              + _rows(vec("b_pw"), ts)) * zg_s[:, dp:]
        cat_s[:, dp:] = yb.astype(BF16)

        y = (jnp.dot(cat_s[...], w_out_ref[...], preferred_element_type=F32)
             + _rows(vec("b_out"), ts))
        yo_s[...] = x_rows(0, ts) + _rows(gate, ts) * y
        for r in range(0, ts, LN_ROWS):
            t = yo_s[r:r + LN_ROWS, :]
            o = _layer_norm(t, eps_post) * _rows(vec("lnpg"), LN_ROWS) + _rows(vec("lnpb"), LN_ROWS)
            out_ref[0, pl.ds(r0 + r, LN_ROWS), :] = o
        return carry

    lax.fori_loop(0, TILES_PER_STEP, tile, 0)


def _inv_count_table():
    q = np.arange(ROW_STRIDE, dtype=np.float64)[:, None]
    s = np.arange(SUBLANES, dtype=np.float64)[None, :]
    t1 = q + ROW_STRIDE * s + 1.0
    tab = np.stack([1.0 / np.minimum(t1, float(w)) for w in POOL_WINDOWS])
    return jnp.asarray(np.broadcast_to(tab[..., None], tab.shape + (LANES,)), dtype=F32)


def _const_spec(shape):
    return pl.BlockSpec(shape, lambda b, i: (0,) * len(shape))


@jax.jit
def kernel(x, c, w_ada, b_ada, w_in, b_in, w_pool, b_pool, ls_pool, w_dw, b_dw, ln_conv_g, ln_conv_b,
           w_pw, b_pw, w_out, b_out, ln_post_g, ln_post_b):
    bsz, seq, d = x.shape
    n_grp, gw = w_pool.shape[0], w_pool.shape[1]
    dp = n_grp * gw
    dc = w_pw.shape[0]
    d_in = w_in.shape[1]
    step_rows = SEQ_TILE * TILES_PER_STEP
    assert seq % step_rows == 0 and SEQ_TILE % BLOCK_ROWS == 0
    assert gw == LANES and n_grp == len(POOL_WINDOWS) and dp == dc
    assert d_in == 2 * dp + 3 * dc and w_dw.shape[0] == CONV_WIDTH
    assert HALO >= CONV_WIDTH - 1 and HALO >= max(POOL_WINDOWS) - 1
    n_slab = dp // LANES

    vectors = (b_in, b_pool.reshape(-1), ls_pool, b_dw, ln_conv_g, ln_conv_b, b_pw, b_out,
               ln_post_g, ln_post_b)
    vec_at, off = {}, 0
    for name, v in zip(_VEC_NAMES, vectors):
        assert v.shape[0] % LANES == 0
        vec_at[name] = (off, v.shape[0])
        off += v.shape[0]

    def whole(shape):
        return pl.BlockSpec(shape, lambda j: (0,) * len(shape))

    mod, vec8, w_dw8 = pl.pallas_call(
        functools.partial(_mod_kernel, vec_at),
        grid=(3 * d // MOD_COLS,),
        in_specs=[whole((bsz, d)),
                  pl.BlockSpec((d, MOD_COLS), lambda j: (0, j)),
                  pl.BlockSpec((1, MOD_COLS), lambda j: (0, j)),
                  whole(w_dw.shape)] + [whole((1, v.shape[0])) for v in vectors],
        out_specs=[pl.BlockSpec((bsz * SUBLANES, MOD_COLS), lambda j: (0, j)),
                   whole((SUBLANES, off)), whole((CONV_WIDTH, 2 * SUBLANES, dc))],
        out_shape=[jax.ShapeDtypeStruct((bsz * SUBLANES, 3 * d), F32),
                   jax.ShapeDtypeStruct((SUBLANES, off), F32),
                   jax.ShapeDtypeStruct((CONV_WIDTH, 2 * SUBLANES, dc), BF16)],
        compiler_params=pltpu.CompilerParams(dimension_semantics=("arbitrary",)),
        name="adaln_mod",
    )(c, w_ada, b_ada[None, :], w_dw, *(v.astype(F32).reshape(1, -1) for v in vectors))
    mod = mod.reshape(bsz, SUBLANES, 3 * d)

    operands = (x, mod, vec8, w_in.astype(BF16), w_pool, _inv_count_table(), w_dw8,
                w_pw.astype(BF16), w_out.astype(BF16))
    in_specs = [
        pl.BlockSpec((1, step_rows, d), lambda b, i: (b, i, 0)),
        pl.BlockSpec((1, SUBLANES, 3 * d), lambda b, i: (b, 0, 0)),
    ] + [_const_spec(op.shape) for op in operands[2:]]

    return pl.pallas_call(
        functools.partial(_layer_kernel, vec_at),
        grid=(bsz, seq // step_rows),
        in_specs=in_specs,
        out_specs=pl.BlockSpec((1, step_rows, d), lambda b, i: (b, i, 0)),
        out_shape=jax.ShapeDtypeStruct((bsz, seq, d), x.dtype),
        scratch_shapes=[
            pltpu.VMEM((SEQ_TILE, d), BF16),
            pltpu.VMEM((n_slab, HALO + SEQ_TILE, LANES), F32),
            pltpu.VMEM((SEQ_TILE, dp + dc), F32),
            pltpu.VMEM((n_slab, SEQ_TILE, LANES), F32),
            pltpu.VMEM((n_slab, HALO + SEQ_TILE, LANES), F32),
            pltpu.VMEM((n_slab, SEQ_TILE, LANES), F32),
            pltpu.VMEM((SEQ_TILE, dc), BF16),
            pltpu.VMEM((SEQ_TILE, dp + dc), BF16),
            pltpu.VMEM((SEQ_TILE, d), F32),
            pltpu.VMEM((d, 2 * dc), BF16),
            pltpu.VMEM((n_grp // 2, 2 * gw, 2 * gw), BF16),
        ],
        compiler_params=pltpu.CompilerParams(
            dimension_semantics=("arbitrary", "arbitrary"),
            vmem_limit_bytes=VMEM_LIMIT_BYTES,
        ),
        name="hybrid_layer",
    )(*operands)
```

```python
import functools

import numpy as np
import jax
import jax.numpy as jnp
from jax import lax
from jax.experimental import pallas as pl
from jax.experimental.pallas import tpu as pltpu

F32 = jnp.float32
BF16 = jnp.bfloat16

POOL_WINDOWS = (2, 4, 8, 16)
CONV_WIDTH = 31
LN_EPS = 1e-5
DEEPNORM_ALPHA = 2.0 ** 0.25

SUBLANES = 8
LANES = 128
SEQ_TILE = 512
TILES_PER_STEP = 2
HALO = 32
ROW_STRIDE = 4
BLOCK_ROWS = SUBLANES * ROW_STRIDE
CONV_GROUP = 4
LN_ROWS = 16
MOD_COLS = 768
VMEM_LIMIT_BYTES = 48 * 1024 * 1024

_VEC_NAMES = ("b_in", "b_pool", "ls", "b_dw", "lncg", "lncb", "b_pw", "b_out", "lnpg", "lnpb")


def _rows(p8, n):
    return p8 if n == SUBLANES else jnp.concatenate([p8] * (n // SUBLANES), axis=0)


def _silu(z):
    return z * jax.nn.sigmoid(z)


def _layer_norm(t, eps=LN_EPS):
    mu = jnp.mean(t, axis=-1, keepdims=True)
    tc = t - mu
    var = jnp.mean(tc * tc, axis=-1, keepdims=True)
    return tc * lax.rsqrt(var + eps)


def _strided_rows(ref, slab, start):
    return ref[slab, pl.ds(start, SUBLANES, stride=ROW_STRIDE), :]


def _mod_kernel(vec_at, c_ref, w_ref, b_ref, w_dw_ref, *refs):
    vec_refs, (o_ref, vec8_ref, w_dw8_ref) = refs[:-3], refs[-3:]
    bsz = c_ref.shape[0]
    c = jnp.concatenate([jnp.broadcast_to(c_ref[b:b + 1, :], (SUBLANES, c_ref.shape[1]))
                         for b in range(bsz)], axis=0)
    s = _silu(c).astype(BF16)
    o_ref[...] = jnp.dot(s, w_ref[...].astype(BF16), preferred_element_type=F32) + b_ref[...]

    @pl.when(pl.program_id(0) == 0)
    def _():
        for name, ref in zip(_VEC_NAMES, vec_refs):
            off, size = vec_at[name]
            vec8_ref[:, off:off + size] = jnp.broadcast_to(ref[...], (SUBLANES, size))
        w_dw8_ref[...] = jnp.broadcast_to(w_dw_ref[...], w_dw8_ref.shape).astype(w_dw8_ref.dtype)


def _layer_kernel(vec_at, x_ref, mod_ref, vec_ref, w_in_ref, w_pool_ref, invc_ref, w_dw_ref, w_pw_ref,
                  w_out_ref, out_ref, h_s, u_s, zg_s, pn_s, v_s, cn_s, c_s, cat_s, yo_s, w_glu_s, w_pool_s):
    ts = SEQ_TILE
    d = x_ref.shape[-1]
    n_slab = u_s.shape[0]
    dp = n_slab * LANES
    glu_a, glu_g, z_b = 2 * dp, 3 * dp, 4 * dp

    def vec(name, lo=0, hi=None):
        off, size = vec_at[name]
        return vec_ref[:, off + lo:off + (size if hi is None else hi)]

    @pl.when(jnp.logical_and(pl.program_id(0) == 0, pl.program_id(1) == 0))
    def _():
        u_s[:, ts:ts + HALO, :] = jnp.zeros((n_slab, HALO, LANES), F32)
        v_s[:, ts:ts + HALO, :] = jnp.zeros((n_slab, HALO, LANES), F32)
        for cb in range(n_slab):
            for part, src in enumerate((glu_a, glu_g)):
                w_glu_s[:, (2 * cb + part) * LANES:(2 * cb + part + 1) * LANES] = (
                    w_in_ref[:, src + cb * LANES:src + (cb + 1) * LANES])
        w_pool_s[...] = jnp.zeros(w_pool_s.shape, BF16)
        for grp in range(n_slab):
            k, lo = grp // 2, (grp % 2) * LANES
            w_pool_s[k, lo:lo + LANES, lo:lo + LANES] = w_pool_ref[grp].astype(BF16)

    shift = mod_ref[0, :, 0:d]
    scale1 = 1.0 + mod_ref[0, :, d:2 * d]
    gate = mod_ref[0, :, 2 * d:3 * d] * (1.0 / DEEPNORM_ALPHA)
    eps_post = LN_EPS / (DEEPNORM_ALPHA * DEEPNORM_ALPHA)
    never = pl.program_id(0) < 0

    def tile(j, carry):
        r0 = pl.multiple_of(j * ts, ts)
        first = jnp.logical_and(pl.program_id(1) == 0, j == 0)

        def x_rows(r, n):
            return x_ref[0, pl.ds(r0 + r, n), :]

        u_s[:, 0:HALO, :] = jnp.where(first, 0.0, u_s[:, ts:ts + HALO, :])
        v_s[:, 0:HALO, :] = jnp.where(first, 0.0, v_s[:, ts:ts + HALO, :])

        for r in range(0, ts, LN_ROWS):
            h = _layer_norm(x_rows(r, LN_ROWS)) * _rows(scale1, LN_ROWS) + _rows(shift, LN_ROWS)
            h_s[r:r + LN_ROWS, :] = h.astype(BF16)

        h = h_s[...]

        def project(lo, hi):
            return (jnp.dot(h, w_in_ref[:, lo:hi], preferred_element_type=F32)
                    + _rows(vec("b_in", lo, hi), ts))

        def glu(cb):
            bias = jnp.concatenate([vec("b_in", glu_a + cb * LANES, glu_a + (cb + 1) * LANES),
                                    vec("b_in", glu_g + cb * LANES, glu_g + (cb + 1) * LANES)], axis=1)
            ag = (jnp.dot(h, w_glu_s[:, 2 * cb * LANES:2 * (cb + 1) * LANES], preferred_element_type=F32)
                  + _rows(bias, ts))
            v_s[cb, HALO:HALO + ts, :] = ag[:, 0:LANES] * jax.nn.sigmoid(ag[:, LANES:])

        def conv(cb, after=None):
            lanes = slice(cb * LANES, (cb + 1) * LANES)
            taps = [w_dw_ref[k, :, lanes] for k in range(CONV_WIDTH)]
            bias = _rows(vec("b_dw", cb * LANES, (cb + 1) * LANES), 2 * SUBLANES)
            for t0 in range(0, ts, 2 * BLOCK_ROWS):
                acc = [bias] * ROW_STRIDE
                part = [None] * ROW_STRIDE
                for qi in range(1 - CONV_WIDTH, ROW_STRIDE):
                    va = _strided_rows(v_s, cb, HALO + t0 + qi)
                    vb = _strided_rows(v_s, cb, HALO + t0 + BLOCK_ROWS + qi)
                    if after is not None and qi == 1 - CONV_WIDTH:
                        va = jnp.where(never, after, va)
                    vin = jnp.concatenate([va, vb], axis=0).astype(BF16)
                    for q in range(ROW_STRIDE):
                        delay = q - qi
                        if 0 <= delay < CONV_WIDTH:
                            term = vin * taps[CONV_WIDTH - 1 - delay]
                            part[q] = term if part[q] is None else part[q] + term
                            if delay % CONV_GROUP == 0:
                                acc[q] = acc[q] + part[q].astype(F32)
                                part[q] = None
                for q in range(ROW_STRIDE):
                    cn_s[cb, pl.ds(t0 + q, SUBLANES, stride=ROW_STRIDE), :] = acc[q][0:SUBLANES]
                    cn_s[cb, pl.ds(t0 + BLOCK_ROWS + q, SUBLANES, stride=ROW_STRIDE), :] = acc[q][SUBLANES:]

        for cb in range(n_slab):
            glu(cb)
        conv(0)
        u = project(0, dp)
        for g in range(n_slab):
            u_s[g, HALO:HALO + ts, :] = u[:, g * LANES:(g + 1) * LANES]
        conv(1, after=u[-SUBLANES:, -LANES:])
        za = _silu(project(dp, 2 * dp)) * _rows(vec("ls"), ts)
        zg_s[:, 0:dp] = za
        conv(2, after=za[-SUBLANES:, -LANES:])
        zb = _silu(project(z_b, z_b + dp))
        zg_s[:, dp:] = zb
        conv(3, after=zb[-SUBLANES:, -LANES:])

        for t0 in range(0, ts, BLOCK_ROWS):
            for g, w in enumerate(POOL_WINDOWS):
                s = {q: _strided_rows(u_s, g, HALO + t0 + q) for q in range(1 - w, ROW_STRIDE)}
                cur = s
                span = 1
                while span < w:
                    s = {q: s[q] + s[q - span] for q in s if q - span in s}
                    span *= 2
                for q in range(ROW_STRIDE):
                    inv = jnp.where(first, invc_ref[g, q], 1.0 / w) if t0 == 0 else 1.0 / w
                    pn_s[g, pl.ds(t0 + q, SUBLANES, stride=ROW_STRIDE), :] = s[q] * inv - cur[q]

        p = jnp.concatenate([pn_s[g] for g in range(n_slab)], axis=1).astype(BF16)
        half = dp // 2
        for k in range(2):
            cols = slice(k * half, (k + 1) * half)
            ya = (jnp.dot(p[:, cols], w_pool_s[k], preferred_element_type=F32)
                  + _rows(vec("b_pool", k * half, (k + 1) * half), ts)) * zg_s[:, cols]
            cat_s[:, cols] = ya.astype(BF16)

        for r in range(0, ts, LN_ROWS):
            cv = jnp.concatenate([cn_s[cb, r:r + LN_ROWS, :] for cb in range(n_slab)], axis=1)
            vn = _layer_norm(cv) * _rows(vec("lncg"), LN_ROWS) + _rows(vec("lncb"), LN_ROWS)
            c_s[r:r + LN_ROWS, :] = _silu(vn).astype(BF16)

        yb = (jnp.dot(c_s[...], w_pw_ref[...], preferred_element_type=F32)
              + _rows(vec("b_pw"), ts)) * zg_s[:, dp:]
        cat_s[:, dp:] = yb.astype(BF16)

        y = (jnp.dot(cat_s[...], w_out_ref[...], preferred_element_type=F32)
             + _rows(vec("b_out"), ts))
        yo_s[...] = x_rows(0, ts) + _rows(gate, ts) * y
        for r in range(0, ts, LN_ROWS):
            t = yo_s[r:r + LN_ROWS, :]
            o = _layer_norm(t, eps_post) * _rows(vec("lnpg"), LN_ROWS) + _rows(vec("lnpb"), LN_ROWS)
            out_ref[0, pl.ds(r0 + r, LN_ROWS), :] = o
        return carry

    lax.fori_loop(0, TILES_PER_STEP, tile, 0)


def _inv_count_table():
    q = np.arange(ROW_STRIDE, dtype=np.float64)[:, None]
    s = np.arange(SUBLANES, dtype=np.float64)[None, :]
    t1 = q + ROW_STRIDE * s + 1.0
    tab = np.stack([1.0 / np.minimum(t1, float(w)) for w in POOL_WINDOWS])
    return jnp.asarray(np.broadcast_to(tab[..., None], tab.shape + (LANES,)), dtype=F32)


def _const_spec(shape):
    return pl.BlockSpec(shape, lambda b, i: (0,) * len(shape))


@jax.jit
def kernel(x, c, w_ada, b_ada, w_in, b_in, w_pool, b_pool, ls_pool, w_dw, b_dw, ln_conv_g, ln_conv_b,
           w_pw, b_pw, w_out, b_out, ln_post_g, ln_post_b):
    bsz, seq, d = x.shape
    n_grp, gw = w_pool.shape[0], w_pool.shape[1]
    dp = n_grp * gw
    dc = w_pw.shape[0]
    d_in = w_in.shape[1]
    step_rows = SEQ_TILE * TILES_PER_STEP
    assert seq % step_rows == 0 and SEQ_TILE % BLOCK_ROWS == 0
    assert gw == LANES and n_grp == len(POOL_WINDOWS) and dp == dc
    assert d_in == 2 * dp + 3 * dc and w_dw.shape[0] == CONV_WIDTH
    assert HALO >= CONV_WIDTH - 1 and HALO >= max(POOL_WINDOWS) - 1
    n_slab = dp // LANES

    vectors = (b_in, b_pool.reshape(-1), ls_pool, b_dw, ln_conv_g, ln_conv_b, b_pw, b_out,
               ln_post_g, ln_post_b)
    vec_at, off = {}, 0
    for name, v in zip(_VEC_NAMES, vectors):
        assert v.shape[0] % LANES == 0
        vec_at[name] = (off, v.shape[0])
        off += v.shape[0]

    def whole(shape):
        return pl.BlockSpec(shape, lambda j: (0,) * len(shape))

    mod, vec8, w_dw8 = pl.pallas_call(
        functools.partial(_mod_kernel, vec_at),
        grid=(3 * d // MOD_COLS,),
        in_specs=[whole((bsz, d)),
                  pl.BlockSpec((d, MOD_COLS), lambda j: (0, j)),
                  pl.BlockSpec((1, MOD_COLS), lambda j: (0, j)),
                  whole(w_dw.shape)] + [whole((1, v.shape[0])) for v in vectors],
        out_specs=[pl.BlockSpec((bsz * SUBLANES, MOD_COLS), lambda j: (0, j)),
                   whole((SUBLANES, off)), whole((CONV_WIDTH, 2 * SUBLANES, dc))],
        out_shape=[jax.ShapeDtypeStruct((bsz * SUBLANES, 3 * d), F32),
                   jax.ShapeDtypeStruct((SUBLANES, off), F32),
                   jax.ShapeDtypeStruct((CONV_WIDTH, 2 * SUBLANES, dc), BF16)],
        compiler_params=pltpu.CompilerParams(dimension_semantics=("arbitrary",)),
        name="adaln_mod",
    )(c, w_ada, b_ada[None, :], w_dw, *(v.astype(F32).reshape(1, -1) for v in vectors))
    mod = mod.reshape(bsz, SUBLANES, 3 * d)

    operands = (x, mod, vec8, w_in.astype(BF16), w_pool, _inv_count_table(), w_dw8,
                w_pw.astype(BF16), w_out.astype(BF16))
    in_specs = [
        pl.BlockSpec((1, step_rows, d), lambda b, i: (b, i, 0)),
        pl.BlockSpec((1, SUBLANES, 3 * d), lambda b, i: (b, 0, 0)),
    ] + [_const_spec(op.shape) for op in operands[2:]]

    return pl.pallas_call(
        functools.partial(_layer_kernel, vec_at),
        grid=(bsz, seq // step_rows),
        in_specs=in_specs,
        out_specs=pl.BlockSpec((1, step_rows, d), lambda b, i: (b, i, 0)),
        out_shape=jax.ShapeDtypeStruct((bsz, seq, d), x.dtype),
        scratch_shapes=[
            pltpu.VMEM((SEQ_TILE, d), BF16),
            pltpu.VMEM((n_slab, HALO + SEQ_TILE, LANES), F32),
            pltpu.VMEM((SEQ_TILE, dp + dc), F32),
            pltpu.VMEM((n_slab, SEQ_TILE, LANES), F32),
            pltpu.VMEM((n_slab, HALO + SEQ_TILE, LANES), F32),
            pltpu.VMEM((n_slab, SEQ_TILE, LANES), F32),
            pltpu.VMEM((SEQ_TILE, dc), BF16),
            pltpu.VMEM((SEQ_TILE, dp + dc), BF16),
            pltpu.VMEM((SEQ_TILE, d), F32),
            pltpu.VMEM((d, 2 * dc), BF16),
            pltpu.VMEM((n_grp // 2, 2 * gw, 2 * gw), BF16),
        ],
        compiler_params=pltpu.CompilerParams(
            dimension_semantics=("arbitrary", "arbitrary"),
            vmem_limit_bytes=VMEM_LIMIT_BYTES,
        ),
        name="hybrid_layer",
    )(*operands)
```

```python
import functools

import numpy as np
import jax
import jax.numpy as jnp
from jax import lax
from jax.experimental import pallas as pl
from jax.experimental.pallas import tpu as pltpu

F32 = jnp.float32
BF16 = jnp.bfloat16

POOL_WINDOWS = (2, 4, 8, 16)
CONV_WIDTH = 31
LN_EPS = 1e-5
DEEPNORM_ALPHA = 2.0 ** 0.25

SUBLANES = 8
LANES = 128
SEQ_TILE = 512
TILES_PER_STEP = 1
HALO = 32
ROW_STRIDE = 4
BLOCK_ROWS = SUBLANES * ROW_STRIDE
CONV_GROUP = 4
LN_ROWS = 16
MOD_COLS = 768
VMEM_LIMIT_BYTES = 48 * 1024 * 1024

_VEC_NAMES = ("b_in", "b_pool", "ls", "b_dw", "lncg", "lncb", "b_pw", "b_out", "lnpg", "lnpb")


def _rows(p8, n):
    return p8 if n == SUBLANES else jnp.concatenate([p8] * (n // SUBLANES), axis=0)


def _silu(z):
    return z * jax.nn.sigmoid(z)


def _layer_norm(t, eps=LN_EPS):
    mu = jnp.mean(t, axis=-1, keepdims=True)
    tc = t - mu
    var = jnp.mean(tc * tc, axis=-1, keepdims=True)
    return tc * lax.rsqrt(var + eps)


def _strided_rows(ref, slab, start):
    return ref[slab, pl.ds(start, SUBLANES, stride=ROW_STRIDE), :]


def _mod_kernel(vec_at, c_ref, w_ref, b_ref, w_dw_ref, *refs):
    vec_refs, (o_ref, vec8_ref, w_dw8_ref) = refs[:-3], refs[-3:]
    bsz = c_ref.shape[0]
    c = jnp.concatenate([jnp.broadcast_to(c_ref[b:b + 1, :], (SUBLANES, c_ref.shape[1]))
                         for b in range(bsz)], axis=0)
    s = _silu(c).astype(BF16)
    o_ref[...] = jnp.dot(s, w_ref[...].astype(BF16), preferred_element_type=F32) + b_ref[...]

    @pl.when(pl.program_id(0) == 0)
    def _():
        for name, ref in zip(_VEC_NAMES, vec_refs):
            off, size = vec_at[name]
            vec8_ref[:, off:off + size] = jnp.broadcast_to(ref[...], (SUBLANES, size))
        w_dw8_ref[...] = jnp.broadcast_to(w_dw_ref[...], w_dw8_ref.shape).astype(w_dw8_ref.dtype)


def _layer_kernel(vec_at, x_ref, mod_ref, vec_ref, w_in_ref, w_pool_ref, invc_ref, w_dw_ref, w_pw_ref,
                  w_out_ref, out_ref, h_s, u_s, zg_s, pn_s, v_s, cn_s, c_s, cat_s, yo_s, w_glu_s, w_pool_s):
    ts = SEQ_TILE
    d = x_ref.shape[-1]
    n_slab = u_s.shape[0]
    dp = n_slab * LANES
    glu_a, glu_g, z_b = 2 * dp, 3 * dp, 4 * dp

    def vec(name, lo=0, hi=None):
        off, size = vec_at[name]
        return vec_ref[:, off + lo:off + (size if hi is None else hi)]

    @pl.when(jnp.logical_and(pl.program_id(0) == 0, pl.program_id(1) == 0))
    def _():
        u_s[:, ts:ts + HALO, :] = jnp.zeros((n_slab, HALO, LANES), F32)
        v_s[:, ts:ts + HALO, :] = jnp.zeros((n_slab, HALO, LANES), F32)
        for cb in range(n_slab):
            for part, src in enumerate((glu_a, glu_g)):
                w_glu_s[:, (2 * cb + part) * LANES:(2 * cb + part + 1) * LANES] = (
                    w_in_ref[:, src + cb * LANES:src + (cb + 1) * LANES])
        w_pool_s[...] = jnp.zeros(w_pool_s.shape, BF16)
        for grp in range(n_slab):
            k, lo = grp // 2, (grp % 2) * LANES
            w_pool_s[k, lo:lo + LANES, lo:lo + LANES] = w_pool_ref[grp].astype(BF16)

    shift = mod_ref[0, :, 0:d]
    scale1 = 1.0 + mod_ref[0, :, d:2 * d]
    gate = mod_ref[0, :, 2 * d:3 * d] * (1.0 / DEEPNORM_ALPHA)
    eps_post = LN_EPS / (DEEPNORM_ALPHA * DEEPNORM_ALPHA)
    never = pl.program_id(0) < 0

    def tile(j, carry):
        r0 = pl.multiple_of(j * ts, ts)
        first = jnp.logical_and(pl.program_id(1) == 0, j == 0)

        def x_rows(r, n):
            return x_ref[0, pl.ds(r0 + r, n), :]

        u_s[:, 0:HALO, :] = jnp.where(first, 0.0, u_s[:, ts:ts + HALO, :])
        v_s[:, 0:HALO, :] = jnp.where(first, 0.0, v_s[:, ts:ts + HALO, :])

        for r in range(0, ts, LN_ROWS):
            h = _layer_norm(x_rows(r, LN_ROWS)) * _rows(scale1, LN_ROWS) + _rows(shift, LN_ROWS)
            h_s[r:r + LN_ROWS, :] = h.astype(BF16)

        h = h_s[...]

        def project(lo, hi):
            return (jnp.dot(h, w_in_ref[:, lo:hi], preferred_element_type=F32)
                    + _rows(vec("b_in", lo, hi), ts))

        def glu(cb):
            bias = jnp.concatenate([vec("b_in", glu_a + cb * LANES, glu_a + (cb + 1) * LANES),
                                    vec("b_in", glu_g + cb * LANES, glu_g + (cb + 1) * LANES)], axis=1)
            ag = (jnp.dot(h, w_glu_s[:, 2 * cb * LANES:2 * (cb + 1) * LANES], preferred_element_type=F32)
                  + _rows(bias, ts))
            v_s[cb, HALO:HALO + ts, :] = ag[:, 0:LANES] * jax.nn.sigmoid(ag[:, LANES:])

        def conv(cb, after=None):
            lanes = slice(cb * LANES, (cb + 1) * LANES)
            taps = [w_dw_ref[k, :, lanes] for k in range(CONV_WIDTH)]
            bias = _rows(vec("b_dw", cb * LANES, (cb + 1) * LANES), 2 * SUBLANES)
            for t0 in range(0, ts, 2 * BLOCK_ROWS):
                acc = [bias] * ROW_STRIDE
                part = [None] * ROW_STRIDE
                for qi in range(1 - CONV_WIDTH, ROW_STRIDE):
                    va = _strided_rows(v_s, cb, HALO + t0 + qi)
                    vb = _strided_rows(v_s, cb, HALO + t0 + BLOCK_ROWS + qi)
                    if after is not None and qi == 1 - CONV_WIDTH:
                        va = jnp.where(never, after, va)
                    vin = jnp.concatenate([va, vb], axis=0).astype(BF16)
                    for q in range(ROW_STRIDE):
                        delay = q - qi
                        if 0 <= delay < CONV_WIDTH:
                            term = vin * taps[CONV_WIDTH - 1 - delay]
                            part[q] = term if part[q] is None else part[q] + term
                            if delay % CONV_GROUP == 0:
                                acc[q] = acc[q] + part[q].astype(F32)
                                part[q] = None
                for q in range(ROW_STRIDE):
                    cn_s[cb, pl.ds(t0 + q, SUBLANES, stride=ROW_STRIDE), :] = acc[q][0:SUBLANES]
                    cn_s[cb, pl.ds(t0 + BLOCK_ROWS + q, SUBLANES, stride=ROW_STRIDE), :] = acc[q][SUBLANES:]

        for cb in range(n_slab):
            glu(cb)
        conv(0)
        u = project(0, dp)
        for g in range(n_slab):
            u_s[g, HALO:HALO + ts, :] = u[:, g * LANES:(g + 1) * LANES]
        conv(1, after=u[-SUBLANES:, -LANES:])
        za = _silu(project(dp, 2 * dp)) * _rows(vec("ls"), ts)
        zg_s[:, 0:dp] = za
        conv(2, after=za[-SUBLANES:, -LANES:])
        zb = _silu(project(z_b, z_b + dp))
        zg_s[:, dp:] = zb
        conv(3, after=zb[-SUBLANES:, -LANES:])

        for t0 in range(0, ts, BLOCK_ROWS):
            for g, w in enumerate(POOL_WINDOWS):
                s = {q: _strided_rows(u_s, g, HALO + t0 + q) for q in range(1 - w, ROW_STRIDE)}
                cur = s
                span = 1
                while span < w:
                    s = {q: s[q] + s[q - span] for q in s if q - span in s}
                    span *= 2
                for q in range(ROW_STRIDE):
                    inv = jnp.where(first, invc_ref[g, q], 1.0 / w) if t0 == 0 else 1.0 / w
                    pn_s[g, pl.ds(t0 + q, SUBLANES, stride=ROW_STRIDE), :] = s[q] * inv - cur[q]

        p = jnp.concatenate([pn_s[g] for g in range(n_slab)], axis=1).astype(BF16)
        half = dp // 2
        for k in range(2):
            cols = slice(k * half, (k + 1) * half)
            ya = (jnp.dot(p[:, cols], w_pool_s[k], preferred_element_type=F32)
                  + _rows(vec("b_pool", k * half, (k + 1) * half), ts)) * zg_s[:, cols]
            cat_s[:, cols] = ya.astype(BF16)

        for r in range(0, ts, LN_ROWS):
            cv = jnp.concatenate([cn_s[cb, r:r + LN_ROWS, :] for cb in range(n_slab)], axis=1)
            vn = _layer_norm(cv) * _rows(vec("lncg"), LN_ROWS) + _rows(vec("lncb"), LN_ROWS)
            c_s[r:r + LN_ROWS, :] = _silu(vn).astype(BF16)

        yb = (jnp.dot(c_s[...], w_pw_ref[...], preferred_element_type=F32)
              + _rows(vec("b_pw"), ts)) * zg_s[:, dp:]
        cat_s[:, dp:] = yb.astype(BF16)

        y = (jnp.dot(cat_s[...], w_out_ref[...], preferred_element_type=F32)
             + _rows(vec("b_out"), ts))
        yo_s[...] = x_rows(0, ts) + _rows(gate, ts) * y
        for r in range(0, ts, LN_ROWS):
            t = yo_s[r:r + LN_ROWS, :]
            o = _layer_norm(t, eps_post) * _rows(vec("lnpg"), LN_ROWS) + _rows(vec("lnpb"), LN_ROWS)
            out_ref[0, pl.ds(r0 + r, LN_ROWS), :] = o
        return carry

    lax.fori_loop(0, TILES_PER_STEP, tile, 0)


def _inv_count_table():
    q = np.arange(ROW_STRIDE, dtype=np.float64)[:, None]
    s = np.arange(SUBLANES, dtype=np.float64)[None, :]
    t1 = q + ROW_STRIDE * s + 1.0
    tab = np.stack([1.0 / np.minimum(t1, float(w)) for w in POOL_WINDOWS])
    return jnp.asarray(np.broadcast_to(tab[..., None], tab.shape + (LANES,)), dtype=F32)


def _const_spec(shape):
    return pl.BlockSpec(shape, lambda b, i: (0,) * len(shape))


@jax.jit
def kernel(x, c, w_ada, b_ada, w_in, b_in, w_pool, b_pool, ls_pool, w_dw, b_dw, ln_conv_g, ln_conv_b,
           w_pw, b_pw, w_out, b_out, ln_post_g, ln_post_b):
    bsz, seq, d = x.shape
    n_grp, gw = w_pool.shape[0], w_pool.shape[1]
    dp = n_grp * gw
    dc = w_pw.shape[0]
    d_in = w_in.shape[1]
    step_rows = SEQ_TILE * TILES_PER_STEP
    assert seq % step_rows == 0 and SEQ_TILE % BLOCK_ROWS == 0
    assert gw == LANES and n_grp == len(POOL_WINDOWS) and dp == dc
    assert d_in == 2 * dp + 3 * dc and w_dw.shape[0] == CONV_WIDTH
    assert HALO >= CONV_WIDTH - 1 and HALO >= max(POOL_WINDOWS) - 1
    n_slab = dp // LANES

    vectors = (b_in, b_pool.reshape(-1), ls_pool, b_dw, ln_conv_g, ln_conv_b, b_pw, b_out,
               ln_post_g, ln_post_b)
    vec_at, off = {}, 0
    for name, v in zip(_VEC_NAMES, vectors):
        assert v.shape[0] % LANES == 0
        vec_at[name] = (off, v.shape[0])
        off += v.shape[0]

    def whole(shape):
        return pl.BlockSpec(shape, lambda j: (0,) * len(shape))

    mod, vec8, w_dw8 = pl.pallas_call(
        functools.partial(_mod_kernel, vec_at),
        grid=(3 * d // MOD_COLS,),
        in_specs=[whole((bsz, d)),
                  pl.BlockSpec((d, MOD_COLS), lambda j: (0, j)),
                  pl.BlockSpec((1, MOD_COLS), lambda j: (0, j)),
                  whole(w_dw.shape)] + [whole((1, v.shape[0])) for v in vectors],
        out_specs=[pl.BlockSpec((bsz * SUBLANES, MOD_COLS), lambda j: (0, j)),
                   whole((SUBLANES, off)), whole((CONV_WIDTH, 2 * SUBLANES, dc))],
        out_shape=[jax.ShapeDtypeStruct((bsz * SUBLANES, 3 * d), F32),
                   jax.ShapeDtypeStruct((SUBLANES, off), F32),
                   jax.ShapeDtypeStruct((CONV_WIDTH, 2 * SUBLANES, dc), BF16)],
        compiler_params=pltpu.CompilerParams(dimension_semantics=("arbitrary",)),
        name="adaln_mod",
    )(c, w_ada, b_ada[None, :], w_dw, *(v.astype(F32).reshape(1, -1) for v in vectors))
    mod = mod.reshape(bsz, SUBLANES, 3 * d)

    operands = (x, mod, vec8, w_in.astype(BF16), w_pool, _inv_count_table(), w_dw8,
                w_pw.astype(BF16), w_out.astype(BF16))
    in_specs = [
        pl.BlockSpec((1, step_rows, d), lambda b, i: (b, i, 0)),
        pl.BlockSpec((1, SUBLANES, 3 * d), lambda b, i: (b, 0, 0)),
    ] + [_const_spec(op.shape) for op in operands[2:]]

    return pl.pallas_call(
        functools.partial(_layer_kernel, vec_at),
        grid=(bsz, seq // step_rows),
        in_specs=in_specs,
        out_specs=pl.BlockSpec((1, step_rows, d), lambda b, i: (b, i, 0)),
        out_shape=jax.ShapeDtypeStruct((bsz, seq, d), x.dtype),
        scratch_shapes=[
            pltpu.VMEM((SEQ_TILE, d), BF16),
            pltpu.VMEM((n_slab, HALO + SEQ_TILE, LANES), F32),
            pltpu.VMEM((SEQ_TILE, dp + dc), F32),
            pltpu.VMEM((n_slab, SEQ_TILE, LANES), F32),
            pltpu.VMEM((n_slab, HALO + SEQ_TILE, LANES), F32),
            pltpu.VMEM((n_slab, SEQ_TILE, LANES), F32),
            pltpu.VMEM((SEQ_TILE, dc), BF16),
            pltpu.VMEM((SEQ_TILE, dp + dc), BF16),
            pltpu.VMEM((SEQ_TILE, d), F32),
            pltpu.VMEM((d, 2 * dc), BF16),
            pltpu.VMEM((n_grp // 2, 2 * gw, 2 * gw), BF16),
        ],
        compiler_params=pltpu.CompilerParams(
            dimension_semantics=("arbitrary", "arbitrary"),
            vmem_limit_bytes=VMEM_LIMIT_BYTES,
        ),
        name="hybrid_layer",
    )(*operands)
```

```python
import functools

import numpy as np
import jax
import jax.numpy as jnp
from jax import lax
from jax.experimental import pallas as pl
from jax.experimental.pallas import tpu as pltpu

F32 = jnp.float32
BF16 = jnp.bfloat16

POOL_WINDOWS = (2, 4, 8, 16)
CONV_WIDTH = 31
LN_EPS = 1e-5
DEEPNORM_ALPHA = 2.0 ** 0.25

SUBLANES = 8
LANES = 128
SEQ_TILE = 512
TILES_PER_STEP = 1
HALO = 32
ROW_STRIDE = 4
BLOCK_ROWS = SUBLANES * ROW_STRIDE
CONV_GROUP = 4
LN_ROWS = 16
MOD_COLS = 768
VMEM_LIMIT_BYTES = 48 * 1024 * 1024

_VEC_NAMES = ("b_in", "b_pool", "ls", "b_dw", "lncg", "lncb", "b_pw", "b_out", "lnpg", "lnpb")


def _rows(p8, n):
    return p8 if n == SUBLANES else jnp.concatenate([p8] * (n // SUBLANES), axis=0)


def _silu(z):
    return z * jax.nn.sigmoid(z)


def _layer_norm(t, eps=LN_EPS):
    mu = jnp.mean(t, axis=-1, keepdims=True)
    tc = t - mu
    var = jnp.mean(tc * tc, axis=-1, keepdims=True)
    return tc * lax.rsqrt(var + eps)


def _strided_rows(ref, slab, start):
    return ref[slab, pl.ds(start, SUBLANES, stride=ROW_STRIDE), :]


def _mod_kernel(vec_at, c_ref, w_ref, b_ref, w_dw_ref, *refs):
    vec_refs, (o_ref, vec8_ref, w_dw8_ref) = refs[:-3], refs[-3:]
    bsz = c_ref.shape[0]
    c = jnp.concatenate([jnp.broadcast_to(c_ref[b:b + 1, :], (SUBLANES, c_ref.shape[1]))
                         for b in range(bsz)], axis=0)
    s = _silu(c).astype(BF16)
    o_ref[...] = jnp.dot(s, w_ref[...].astype(BF16), preferred_element_type=F32) + b_ref[...]

    @pl.when(pl.program_id(0) == 0)
    def _():
        for name, ref in zip(_VEC_NAMES, vec_refs):
            off, size = vec_at[name]
            vec8_ref[:, off:off + size] = jnp.broadcast_to(ref[...], (SUBLANES, size))
        w_dw8_ref[...] = jnp.broadcast_to(w_dw_ref[...], w_dw8_ref.shape).astype(w_dw8_ref.dtype)


def _layer_kernel(vec_at, x_ref, mod_ref, vec_ref, w_in_ref, w_pool_ref, invc_ref, w_dw_ref, w_pw_ref,
                  w_out_ref, out_ref, h_s, u_s, zg_s, pn_s, v_s, cn_s, c_s, cat_s, yo_s, w_glu_s, w_pool_s):
    ts = SEQ_TILE
    d = x_ref.shape[-1]
    n_slab = u_s.shape[0]
    dp = n_slab * LANES
    glu_a, glu_g, z_b = 2 * dp, 3 * dp, 4 * dp

    def vec(name, lo=0, hi=None):
        off, size = vec_at[name]
        return vec_ref[:, off + lo:off + (size if hi is None else hi)]

    @pl.when(jnp.logical_and(pl.program_id(0) == 0, pl.program_id(1) == 0))
    def _():
        u_s[:, ts:ts + HALO, :] = jnp.zeros((n_slab, HALO, LANES), F32)
        v_s[:, ts:ts + HALO, :] = jnp.zeros((n_slab, HALO, LANES), F32)
        for cb in range(n_slab):
            for part, src in enumerate((glu_a, glu_g)):
                w_glu_s[:, (2 * cb + part) * LANES:(2 * cb + part + 1) * LANES] = (
                    w_in_ref[:, src + cb * LANES:src + (cb + 1) * LANES])
        w_pool_s[...] = jnp.zeros(w_pool_s.shape, BF16)
        for grp in range(n_slab):
            k, lo = grp // 2, (grp % 2) * LANES
            w_pool_s[k, lo:lo + LANES, lo:lo + LANES] = w_pool_ref[grp].astype(BF16)

    shift = mod_ref[0, :, 0:d]
    scale1 = 1.0 + mod_ref[0, :, d:2 * d]
    gate = mod_ref[0, :, 2 * d:3 * d] * (1.0 / DEEPNORM_ALPHA)
    eps_post = LN_EPS / (DEEPNORM_ALPHA * DEEPNORM_ALPHA)
    never = pl.program_id(0) < 0

    def tile(j, carry):
        r0 = pl.multiple_of(j * ts, ts)
        first = jnp.logical_and(pl.program_id(1) == 0, j == 0)

        def x_rows(r, n):
            return x_ref[0, pl.ds(r0 + r, n), :]

        u_s[:, 0:HALO, :] = jnp.where(first, 0.0, u_s[:, ts:ts + HALO, :])
        v_s[:, 0:HALO, :] = jnp.where(first, 0.0, v_s[:, ts:ts + HALO, :])

        scale_b = _rows(scale1, LN_ROWS).astype(BF16)
        shift_b = _rows(shift, LN_ROWS).astype(BF16)
        for r in range(0, ts, LN_ROWS):
            h_s[r:r + LN_ROWS, :] = _layer_norm(x_rows(r, LN_ROWS)).astype(BF16) * scale_b + shift_b

        h = h_s[...]

        def project(lo, hi):
            return (jnp.dot(h, w_in_ref[:, lo:hi], preferred_element_type=F32)
                    + _rows(vec("b_in", lo, hi), ts))

        def glu(cb):
            bias = jnp.concatenate([vec("b_in", glu_a + cb * LANES, glu_a + (cb + 1) * LANES),
                                    vec("b_in", glu_g + cb * LANES, glu_g + (cb + 1) * LANES)], axis=1)
            ag = (jnp.dot(h, w_glu_s[:, 2 * cb * LANES:2 * (cb + 1) * LANES], preferred_element_type=F32)
                  + _rows(bias, ts))
            v_s[cb, HALO:HALO + ts, :] = ag[:, 0:LANES] * jax.nn.sigmoid(ag[:, LANES:])

        def conv(cb, after=None):
            lanes = slice(cb * LANES, (cb + 1) * LANES)
            taps = [w_dw_ref[k, :, lanes] for k in range(CONV_WIDTH)]
            bias = _rows(vec("b_dw", cb * LANES, (cb + 1) * LANES), 2 * SUBLANES)
            for t0 in range(0, ts, 2 * BLOCK_ROWS):
                acc = [bias] * ROW_STRIDE
                part = [None] * ROW_STRIDE
                for qi in range(1 - CONV_WIDTH, ROW_STRIDE):
                    va = _strided_rows(v_s, cb, HALO + t0 + qi)
                    vb = _strided_rows(v_s, cb, HALO + t0 + BLOCK_ROWS + qi)
                    if after is not None and qi == 1 - CONV_WIDTH:
                        va = jnp.where(never, after, va)
                    vin = jnp.concatenate([va, vb], axis=0).astype(BF16)
                    for q in range(ROW_STRIDE):
                        delay = q - qi
                        if 0 <= delay < CONV_WIDTH:
                            term = vin * taps[CONV_WIDTH - 1 - delay]
                            part[q] = term if part[q] is None else part[q] + term
                            if delay % CONV_GROUP == 0:
                                acc[q] = acc[q] + part[q].astype(F32)
                                part[q] = None
                for q in range(ROW_STRIDE):
                    cn_s[cb, pl.ds(t0 + q, SUBLANES, stride=ROW_STRIDE), :] = acc[q][0:SUBLANES]
                    cn_s[cb, pl.ds(t0 + BLOCK_ROWS + q, SUBLANES, stride=ROW_STRIDE), :] = acc[q][SUBLANES:]

        for cb in range(n_slab):
            glu(cb)
        conv(0)
        u = project(0, dp)
        for g in range(n_slab):
            u_s[g, HALO:HALO + ts, :] = u[:, g * LANES:(g + 1) * LANES]
        conv(1, after=u[-SUBLANES:, -LANES:])
        za = _silu(project(dp, 2 * dp)) * _rows(vec("ls"), ts)
        zg_s[:, 0:dp] = za
        conv(2, after=za[-SUBLANES:, -LANES:])
        zb = _silu(project(z_b, z_b + dp))
        zg_s[:, dp:] = zb
        conv(3, after=zb[-SUBLANES:, -LANES:])

        for t0 in range(0, ts, BLOCK_ROWS):
            for g, w in enumerate(POOL_WINDOWS):
                s = {q: _strided_rows(u_s, g, HALO + t0 + q) for q in range(1 - w, ROW_STRIDE)}
                cur = s
                span = 1
                while span < w:
                    s = {q: s[q] + s[q - span] for q in s if q - span in s}
                    span *= 2
                for q in range(ROW_STRIDE):
                    inv = jnp.where(first, invc_ref[g, q], 1.0 / w) if t0 == 0 else 1.0 / w
                    pn_s[g, pl.ds(t0 + q, SUBLANES, stride=ROW_STRIDE), :] = s[q] * inv - cur[q]

        p = jnp.concatenate([pn_s[g] for g in range(n_slab)], axis=1).astype(BF16)
        half = dp // 2
        for k in range(2):
            cols = slice(k * half, (k + 1) * half)
            ya = (jnp.dot(p[:, cols], w_pool_s[k], preferred_element_type=F32)
                  + _rows(vec("b_pool", k * half, (k + 1) * half), ts)) * zg_s[:, cols]
            cat_s[:, cols] = ya.astype(BF16)

        for r in range(0, ts, LN_ROWS):
            cv = jnp.concatenate([cn_s[cb, r:r + LN_ROWS, :] for cb in range(n_slab)], axis=1)
            vn = _layer_norm(cv) * _rows(vec("lncg"), LN_ROWS) + _rows(vec("lncb"), LN_ROWS)
            c_s[r:r + LN_ROWS, :] = _silu(vn).astype(BF16)

        yb = (jnp.dot(c_s[...], w_pw_ref[...], preferred_element_type=F32)
              + _rows(vec("b_pw"), ts)) * zg_s[:, dp:]
        cat_s[:, dp:] = yb.astype(BF16)

        y = (jnp.dot(cat_s[...], w_out_ref[...], preferred_element_type=F32)
             + _rows(vec("b_out"), ts))
        yo_s[...] = x_rows(0, ts) + _rows(gate, ts) * y
        for r in range(0, ts, LN_ROWS):
            t = yo_s[r:r + LN_ROWS, :]
            o = _layer_norm(t, eps_post) * _rows(vec("lnpg"), LN_ROWS) + _rows(vec("lnpb"), LN_ROWS)
            out_ref[0, pl.ds(r0 + r, LN_ROWS), :] = o
        return carry

    lax.fori_loop(0, TILES_PER_STEP, tile, 0)


def _inv_count_table():
    q = np.arange(ROW_STRIDE, dtype=np.float64)[:, None]
    s = np.arange(SUBLANES, dtype=np.float64)[None, :]
    t1 = q + ROW_STRIDE * s + 1.0
    tab = np.stack([1.0 / np.minimum(t1, float(w)) for w in POOL_WINDOWS])
    return jnp.asarray(np.broadcast_to(tab[..., None], tab.shape + (LANES,)), dtype=F32)


def _const_spec(shape):
    return pl.BlockSpec(shape, lambda b, i: (0,) * len(shape))


@jax.jit
def kernel(x, c, w_ada, b_ada, w_in, b_in, w_pool, b_pool, ls_pool, w_dw, b_dw, ln_conv_g, ln_conv_b,
           w_pw, b_pw, w_out, b_out, ln_post_g, ln_post_b):
    bsz, seq, d = x.shape
    n_grp, gw = w_pool.shape[0], w_pool.shape[1]
    dp = n_grp * gw
    dc = w_pw.shape[0]
    d_in = w_in.shape[1]
    step_rows = SEQ_TILE * TILES_PER_STEP
    assert seq % step_rows == 0 and SEQ_TILE % BLOCK_ROWS == 0
    assert gw == LANES and n_grp == len(POOL_WINDOWS) and dp == dc
    assert d_in == 2 * dp + 3 * dc and w_dw.shape[0] == CONV_WIDTH
    assert HALO >= CONV_WIDTH - 1 and HALO >= max(POOL_WINDOWS) - 1
    n_slab = dp // LANES

    vectors = (b_in, b_pool.reshape(-1), ls_pool, b_dw, ln_conv_g, ln_conv_b, b_pw, b_out,
               ln_post_g, ln_post_b)
    vec_at, off = {}, 0
    for name, v in zip(_VEC_NAMES, vectors):
        assert v.shape[0] % LANES == 0
        vec_at[name] = (off, v.shape[0])
        off += v.shape[0]

    def whole(shape):
        return pl.BlockSpec(shape, lambda j: (0,) * len(shape))

    mod, vec8, w_dw8 = pl.pallas_call(
        functools.partial(_mod_kernel, vec_at),
        grid=(3 * d // MOD_COLS,),
        in_specs=[whole((bsz, d)),
                  pl.BlockSpec((d, MOD_COLS), lambda j: (0, j)),
                  pl.BlockSpec((1, MOD_COLS), lambda j: (0, j)),
                  whole(w_dw.shape)] + [whole((1, v.shape[0])) for v in vectors],
        out_specs=[pl.BlockSpec((bsz * SUBLANES, MOD_COLS), lambda j: (0, j)),
                   whole((SUBLANES, off)), whole((CONV_WIDTH, 2 * SUBLANES, dc))],
        out_shape=[jax.ShapeDtypeStruct((bsz * SUBLANES, 3 * d), F32),
                   jax.ShapeDtypeStruct((SUBLANES, off), F32),
                   jax.ShapeDtypeStruct((CONV_WIDTH, 2 * SUBLANES, dc), BF16)],
        compiler_params=pltpu.CompilerParams(dimension_semantics=("arbitrary",)),
        name="adaln_mod",
    )(c, w_ada, b_ada[None, :], w_dw, *(v.astype(F32).reshape(1, -1) for v in vectors))
    mod = mod.reshape(bsz, SUBLANES, 3 * d)

    operands = (x, mod, vec8, w_in.astype(BF16), w_pool, _inv_count_table(), w_dw8,
                w_pw.astype(BF16), w_out.astype(BF16))
    in_specs = [
        pl.BlockSpec((1, step_rows, d), lambda b, i: (b, i, 0)),
        pl.BlockSpec((1, SUBLANES, 3 * d), lambda b, i: (b, 0, 0)),
    ] + [_const_spec(op.shape) for op in operands[2:]]

    return pl.pallas_call(
        functools.partial(_layer_kernel, vec_at),
        grid=(bsz, seq // step_rows),
        in_specs=in_specs,
        out_specs=pl.BlockSpec((1, step_rows, d), lambda b, i: (b, i, 0)),
        out_shape=jax.ShapeDtypeStruct((bsz, seq, d), x.dtype),
        scratch_shapes=[
            pltpu.VMEM((SEQ_TILE, d), BF16),
            pltpu.VMEM((n_slab, HALO + SEQ_TILE, LANES), F32),
            pltpu.VMEM((SEQ_TILE, dp + dc), F32),
            pltpu.VMEM((n_slab, SEQ_TILE, LANES), F32),
            pltpu.VMEM((n_slab, HALO + SEQ_TILE, LANES), F32),
            pltpu.VMEM((n_slab, SEQ_TILE, LANES), F32),
            pltpu.VMEM((SEQ_TILE, dc), BF16),
            pltpu.VMEM((SEQ_TILE, dp + dc), BF16),
            pltpu.VMEM((SEQ_TILE, d), F32),
            pltpu.VMEM((d, 2 * dc), BF16),
            pltpu.VMEM((n_grp // 2, 2 * gw, 2 * gw), BF16),
        ],
        compiler_params=pltpu.CompilerParams(
            dimension_semantics=("arbitrary", "arbitrary"),
            vmem_limit_bytes=VMEM_LIMIT_BYTES,
        ),
        name="hybrid_layer",
    )(*operands)
```

```python
import functools

import numpy as np
import jax
import jax.numpy as jnp
from jax import lax
from jax.experimental import pallas as pl
from jax.experimental.pallas import tpu as pltpu

F32 = jnp.float32
BF16 = jnp.bfloat16

POOL_WINDOWS = (2, 4, 8, 16)
CONV_WIDTH = 31
LN_EPS = 1e-5
DEEPNORM_ALPHA = 2.0 ** 0.25

SUBLANES = 8
LANES = 128
SEQ_TILE = 512
TILES_PER_STEP = 1
HALO = 32
ROW_STRIDE = 4
BLOCK_ROWS = SUBLANES * ROW_STRIDE
CONV_GROUP = 4
LN_ROWS = 16
MOD_COLS = 768
VMEM_LIMIT_BYTES = 48 * 1024 * 1024

_VEC_NAMES = ("b_in", "b_pool", "ls", "b_dw", "lncg", "lncb", "b_pw", "b_out", "lnpg", "lnpb")


def _rows(p8, n):
    return p8 if n == SUBLANES else jnp.concatenate([p8] * (n // SUBLANES), axis=0)


def _silu(z):
    return z * jax.nn.sigmoid(z)


def _layer_norm(t, eps=LN_EPS):
    mu = jnp.mean(t, axis=-1, keepdims=True)
    tc = t - mu
    var = jnp.mean(tc * tc, axis=-1, keepdims=True)
    return tc * lax.rsqrt(var + eps)


def _strided_rows(ref, slab, start):
    return ref[slab, pl.ds(start, SUBLANES, stride=ROW_STRIDE), :]


def _mod_kernel(vec_at, c_ref, w_ref, b_ref, w_dw_ref, *refs):
    vec_refs, (o_ref, vec8_ref, w_dw8_ref) = refs[:-3], refs[-3:]
    bsz = c_ref.shape[0]
    c = jnp.concatenate([jnp.broadcast_to(c_ref[b:b + 1, :], (SUBLANES, c_ref.shape[1]))
                         for b in range(bsz)], axis=0)
    s = _silu(c).astype(BF16)
    o_ref[...] = jnp.dot(s, w_ref[...].astype(BF16), preferred_element_type=F32) + b_ref[...]

    @pl.when(pl.program_id(0) == 0)
    def _():
        for name, ref in zip(_VEC_NAMES, vec_refs):
            off, size = vec_at[name]
            vec8_ref[:, off:off + size] = jnp.broadcast_to(ref[...], (SUBLANES, size))
        w_dw8_ref[...] = jnp.broadcast_to(w_dw_ref[...], w_dw8_ref.shape).astype(w_dw8_ref.dtype)


def _layer_kernel(vec_at, x_ref, mod_ref, vec_ref, w_in_ref, w_pool_ref, invc_ref, w_dw_ref, w_pw_ref,
                  w_out_ref, out_ref, h_s, u_s, zg_s, pn_s, v_s, cn_s, c_s, cat_s, yo_s, w_pool_s):
    ts = SEQ_TILE
    d = x_ref.shape[-1]
    n_slab = u_s.shape[0]
    dp = n_slab * LANES
    glu_a, glu_g, z_b = 2 * dp, 3 * dp, 4 * dp

    def vec(name, lo=0, hi=None):
        off, size = vec_at[name]
        return vec_ref[:, off + lo:off + (size if hi is None else hi)]

    @pl.when(jnp.logical_and(pl.program_id(0) == 0, pl.program_id(1) == 0))
    def _():
        u_s[:, ts:ts + HALO, :] = jnp.zeros((n_slab, HALO, LANES), F32)
        v_s[:, ts:ts + HALO, :] = jnp.zeros((n_slab, HALO, LANES), F32)
        w_pool_s[...] = jnp.zeros(w_pool_s.shape, BF16)
        for grp in range(n_slab):
            k, lo = grp // 2, (grp % 2) * LANES
            w_pool_s[k, lo:lo + LANES, lo:lo + LANES] = w_pool_ref[grp].astype(BF16)

    shift = mod_ref[0, :, 0:d]
    scale1 = 1.0 + mod_ref[0, :, d:2 * d]
    gate = mod_ref[0, :, 2 * d:3 * d] * (1.0 / DEEPNORM_ALPHA)
    eps_post = LN_EPS / (DEEPNORM_ALPHA * DEEPNORM_ALPHA)
    never = pl.program_id(0) < 0

    def tile(j, carry):
        r0 = pl.multiple_of(j * ts, ts)
        first = jnp.logical_and(pl.program_id(1) == 0, j == 0)

        def x_rows(r, n):
            return x_ref[0, pl.ds(r0 + r, n), :]

        u_s[:, 0:HALO, :] = jnp.where(first, 0.0, u_s[:, ts:ts + HALO, :])
        v_s[:, 0:HALO, :] = jnp.where(first, 0.0, v_s[:, ts:ts + HALO, :])

        for r in range(0, ts, LN_ROWS):
            h = _layer_norm(x_rows(r, LN_ROWS)) * _rows(scale1, LN_ROWS) + _rows(shift, LN_ROWS)
            h_s[r:r + LN_ROWS, :] = h.astype(BF16)

        h = h_s[...]

        def project(lo, hi):
            return (jnp.dot(h, w_in_ref[:, lo:hi], preferred_element_type=F32)
                    + _rows(vec("b_in", lo, hi), ts))

        def glu(cb):
            bias = jnp.concatenate([vec("b_in", glu_a + cb * LANES, glu_a + (cb + 1) * LANES),
                                    vec("b_in", glu_g + cb * LANES, glu_g + (cb + 1) * LANES)], axis=1)
            w_ag = jnp.concatenate([w_in_ref[:, glu_a + cb * LANES:glu_a + (cb + 1) * LANES],
                                    w_in_ref[:, glu_g + cb * LANES:glu_g + (cb + 1) * LANES]], axis=1)
            ag = (jnp.dot(h, w_ag, preferred_element_type=F32)
                  + _rows(bias, ts))
            v_s[cb, HALO:HALO + ts, :] = ag[:, 0:LANES] * jax.nn.sigmoid(ag[:, LANES:])

        def conv(cb, after=None):
            lanes = slice(cb * LANES, (cb + 1) * LANES)
            taps = [w_dw_ref[k, :, lanes] for k in range(CONV_WIDTH)]
            bias = _rows(vec("b_dw", cb * LANES, (cb + 1) * LANES), 2 * SUBLANES)
            for t0 in range(0, ts, 2 * BLOCK_ROWS):
                acc = [bias] * ROW_STRIDE
                part = [None] * ROW_STRIDE
                for qi in range(1 - CONV_WIDTH, ROW_STRIDE):
                    va = _strided_rows(v_s, cb, HALO + t0 + qi)
                    vb = _strided_rows(v_s, cb, HALO + t0 + BLOCK_ROWS + qi)
                    if after is not None and qi == 1 - CONV_WIDTH:
                        va = jnp.where(never, after, va)
                    vin = jnp.concatenate([va, vb], axis=0).astype(BF16)
                    for q in range(ROW_STRIDE):
                        delay = q - qi
                        if 0 <= delay < CONV_WIDTH:
                            term = vin * taps[CONV_WIDTH - 1 - delay]
                            part[q] = term if part[q] is None else part[q] + term
                            if delay % CONV_GROUP == 0:
                                acc[q] = acc[q] + part[q].astype(F32)
                                part[q] = None
                for q in range(ROW_STRIDE):
                    cn_s[cb, pl.ds(t0 + q, SUBLANES, stride=ROW_STRIDE), :] = acc[q][0:SUBLANES]
                    cn_s[cb, pl.ds(t0 + BLOCK_ROWS + q, SUBLANES, stride=ROW_STRIDE), :] = acc[q][SUBLANES:]

        for cb in range(n_slab):
            glu(cb)
        conv(0)
        u = project(0, dp)
        for g in range(n_slab):
            u_s[g, HALO:HALO + ts, :] = u[:, g * LANES:(g + 1) * LANES]
        conv(1, after=u[-SUBLANES:, -LANES:])
        za = _silu(project(dp, 2 * dp)) * _rows(vec("ls"), ts)
        zg_s[:, 0:dp] = za
        conv(2, after=za[-SUBLANES:, -LANES:])
        zb = _silu(project(z_b, z_b + dp))
        zg_s[:, dp:] = zb
        conv(3, after=zb[-SUBLANES:, -LANES:])

        for t0 in range(0, ts, BLOCK_ROWS):
            for g, w in enumerate(POOL_WINDOWS):
                s = {q: _strided_rows(u_s, g, HALO + t0 + q) for q in range(1 - w, ROW_STRIDE)}
                cur = s
                span = 1
                while span < w:
                    s = {q: s[q] + s[q - span] for q in s if q - span in s}
                    span *= 2
                for q in range(ROW_STRIDE):
                    inv = jnp.where(first, invc_ref[g, q], 1.0 / w) if t0 == 0 else 1.0 / w
                    pn_s[g, pl.ds(t0 + q, SUBLANES, stride=ROW_STRIDE), :] = s[q] * inv - cur[q]

        p = jnp.concatenate([pn_s[g] for g in range(n_slab)], axis=1).astype(BF16)
        half = dp // 2
        for k in range(2):
            cols = slice(k * half, (k + 1) * half)
            ya = (jnp.dot(p[:, cols], w_pool_s[k], preferred_element_type=F32)
                  + _rows(vec("b_pool", k * half, (k + 1) * half), ts)) * zg_s[:, cols]
            cat_s[:, cols] = ya.astype(BF16)

        for r in range(0, ts, LN_ROWS):
            cv = jnp.concatenate([cn_s[cb, r:r + LN_ROWS, :] for cb in range(n_slab)], axis=1)
            vn = _layer_norm(cv) * _rows(vec("lncg"), LN_ROWS) + _rows(vec("lncb"), LN_ROWS)
            c_s[r:r + LN_ROWS, :] = _silu(vn).astype(BF16)

        yb = (jnp.dot(c_s[...], w_pw_ref[...], preferred_element_type=F32)
              + _rows(vec("b_pw"), ts)) * zg_s[:, dp:]
        cat_s[:, dp:] = yb.astype(BF16)

        y = (jnp.dot(cat_s[...], w_out_ref[...], preferred_element_type=F32)
             + _rows(vec("b_out"), ts))
        yo_s[...] = x_rows(0, ts) + _rows(gate, ts) * y
        for r in range(0, ts, LN_ROWS):
            t = yo_s[r:r + LN_ROWS, :]
            o = _layer_norm(t, eps_post) * _rows(vec("lnpg"), LN_ROWS) + _rows(vec("lnpb"), LN_ROWS)
            out_ref[0, pl.ds(r0 + r, LN_ROWS), :] = o
        return carry

    lax.fori_loop(0, TILES_PER_STEP, tile, 0)


def _inv_count_table():
    q = np.arange(ROW_STRIDE, dtype=np.float64)[:, None]
    s = np.arange(SUBLANES, dtype=np.float64)[None, :]
    t1 = q + ROW_STRIDE * s + 1.0
    tab = np.stack([1.0 / np.minimum(t1, float(w)) for w in POOL_WINDOWS])
    return jnp.asarray(np.broadcast_to(tab[..., None], tab.shape + (LANES,)), dtype=F32)


def _const_spec(shape):
    return pl.BlockSpec(shape, lambda b, i: (0,) * len(shape))


@jax.jit
def kernel(x, c, w_ada, b_ada, w_in, b_in, w_pool, b_pool, ls_pool, w_dw, b_dw, ln_conv_g, ln_conv_b,
           w_pw, b_pw, w_out, b_out, ln_post_g, ln_post_b):
    bsz, seq, d = x.shape
    n_grp, gw = w_pool.shape[0], w_pool.shape[1]
    dp = n_grp * gw
    dc = w_pw.shape[0]
    d_in = w_in.shape[1]
    step_rows = SEQ_TILE * TILES_PER_STEP
    assert seq % step_rows == 0 and SEQ_TILE % BLOCK_ROWS == 0
    assert gw == LANES and n_grp == len(POOL_WINDOWS) and dp == dc
    assert d_in == 2 * dp + 3 * dc and w_dw.shape[0] == CONV_WIDTH
    assert HALO >= CONV_WIDTH - 1 and HALO >= max(POOL_WINDOWS) - 1
    n_slab = dp // LANES

    vectors = (b_in, b_pool.reshape(-1), ls_pool, b_dw, ln_conv_g, ln_conv_b, b_pw, b_out,
               ln_post_g, ln_post_b)
    vec_at, off = {}, 0
    for name, v in zip(_VEC_NAMES, vectors):
        assert v.shape[0] % LANES == 0
        vec_at[name] = (off, v.shape[0])
        off += v.shape[0]

    def whole(shape):
        return pl.BlockSpec(shape, lambda j: (0,) * len(shape))

    mod, vec8, w_dw8 = pl.pallas_call(
        functools.partial(_mod_kernel, vec_at),
        grid=(3 * d // MOD_COLS,),
        in_specs=[whole((bsz, d)),
                  pl.BlockSpec((d, MOD_COLS), lambda j: (0, j)),
                  pl.BlockSpec((1, MOD_COLS), lambda j: (0, j)),
                  whole(w_dw.shape)] + [whole((1, v.shape[0])) for v in vectors],
        out_specs=[pl.BlockSpec((bsz * SUBLANES, MOD_COLS), lambda j: (0, j)),
                   whole((SUBLANES, off)), whole((CONV_WIDTH, 2 * SUBLANES, dc))],
        out_shape=[jax.ShapeDtypeStruct((bsz * SUBLANES, 3 * d), F32),
                   jax.ShapeDtypeStruct((SUBLANES, off), F32),
                   jax.ShapeDtypeStruct((CONV_WIDTH, 2 * SUBLANES, dc), BF16)],
        compiler_params=pltpu.CompilerParams(dimension_semantics=("arbitrary",)),
        name="adaln_mod",
    )(c, w_ada, b_ada[None, :], w_dw, *(v.astype(F32).reshape(1, -1) for v in vectors))
    mod = mod.reshape(bsz, SUBLANES, 3 * d)

    operands = (x, mod, vec8, w_in.astype(BF16), w_pool, _inv_count_table(), w_dw8,
                w_pw.astype(BF16), w_out.astype(BF16))
    in_specs = [
        pl.BlockSpec((1, step_rows, d), lambda b, i: (b, i, 0)),
        pl.BlockSpec((1, SUBLANES, 3 * d), lambda b, i: (b, 0, 0)),
    ] + [_const_spec(op.shape) for op in operands[2:]]

    return pl.pallas_call(
        functools.partial(_layer_kernel, vec_at),
        grid=(bsz, seq // step_rows),
        in_specs=in_specs,
        out_specs=pl.BlockSpec((1, step_rows, d), lambda b, i: (b, i, 0)),
        out_shape=jax.ShapeDtypeStruct((bsz, seq, d), x.dtype),
        scratch_shapes=[
            pltpu.VMEM((SEQ_TILE, d), BF16),
            pltpu.VMEM((n_slab, HALO + SEQ_TILE, LANES), F32),
            pltpu.VMEM((SEQ_TILE, dp + dc), F32),
            pltpu.VMEM((n_slab, SEQ_TILE, LANES), F32),
            pltpu.VMEM((n_slab, HALO + SEQ_TILE, LANES), F32),
            pltpu.VMEM((n_slab, SEQ_TILE, LANES), F32),
            pltpu.VMEM((SEQ_TILE, dc), BF16),
            pltpu.VMEM((SEQ_TILE, dp + dc), BF16),
            pltpu.VMEM((SEQ_TILE, d), F32),
            pltpu.VMEM((n_grp // 2, 2 * gw, 2 * gw), BF16),
        ],
        compiler_params=pltpu.CompilerParams(
            dimension_semantics=("arbitrary", "arbitrary"),
            vmem_limit_bytes=VMEM_LIMIT_BYTES,
        ),
        name="hybrid_layer",
    )(*operands)
```

```python
import functools

import numpy as np
import jax
import jax.numpy as jnp
from jax import lax
from jax.experimental import pallas as pl
from jax.experimental.pallas import tpu as pltpu

F32 = jnp.float32
BF16 = jnp.bfloat16

POOL_WINDOWS = (2, 4, 8, 16)
CONV_WIDTH = 31
LN_EPS = 1e-5
DEEPNORM_ALPHA = 2.0 ** 0.25

SUBLANES = 8
LANES = 128
SEQ_TILE = 512
TILES_PER_STEP = 1
HALO = 32
ROW_STRIDE = 4
BLOCK_ROWS = SUBLANES * ROW_STRIDE
CONV_GROUP = 4
LN_ROWS = 16
MOD_COLS = 768
VMEM_LIMIT_BYTES = 48 * 1024 * 1024

_VEC_NAMES = ("b_in", "b_pool", "ls", "b_dw", "lncg", "lncb", "b_pw", "b_out", "lnpg", "lnpb")


def _rows(p8, n):
    return p8 if n == SUBLANES else jnp.concatenate([p8] * (n // SUBLANES), axis=0)


def _silu(z):
    return z * jax.nn.sigmoid(z)


def _layer_norm(t, eps=LN_EPS):
    mu = jnp.mean(t, axis=-1, keepdims=True)
    tc = t - mu
    var = jnp.mean(tc * tc, axis=-1, keepdims=True)
    return tc * lax.rsqrt(var + eps)


def _strided_rows(ref, slab, start):
    return ref[slab, pl.ds(start, SUBLANES, stride=ROW_STRIDE), :]


def _mod_kernel(vec_at, c_ref, w_ref, b_ref, w_dw_ref, *refs):
    vec_refs, (o_ref, vec8_ref, w_dw8_ref) = refs[:-3], refs[-3:]
    bsz = c_ref.shape[0]
    c = jnp.concatenate([jnp.broadcast_to(c_ref[b:b + 1, :], (SUBLANES, c_ref.shape[1]))
                         for b in range(bsz)], axis=0)
    s = _silu(c).astype(BF16)
    o_ref[...] = jnp.dot(s, w_ref[...].astype(BF16), preferred_element_type=F32) + b_ref[...]

    @pl.when(pl.program_id(0) == 0)
    def _():
        for name, ref in zip(_VEC_NAMES, vec_refs):
            off, size = vec_at[name]
            vec8_ref[:, off:off + size] = jnp.broadcast_to(ref[...], (SUBLANES, size))
        w_dw8_ref[...] = jnp.broadcast_to(w_dw_ref[...], w_dw8_ref.shape).astype(w_dw8_ref.dtype)


def _layer_kernel(vec_at, x_ref, mod_ref, vec_ref, w_in_ref, w_pool_ref, invc_ref, w_dw_ref, w_pw_ref,
                  w_out_ref, out_ref, h_s, u_s, zg_s, pn_s, v_s, cn_s, c_s, cat_s, yo_s, w_pool_s):
    ts = SEQ_TILE
    d = x_ref.shape[-1]
    n_slab = u_s.shape[0]
    dp = n_slab * LANES
    glu_a, glu_g, z_b = 2 * dp, 3 * dp, 4 * dp

    def vec(name, lo=0, hi=None):
        off, size = vec_at[name]
        return vec_ref[:, off + lo:off + (size if hi is None else hi)]

    @pl.when(jnp.logical_and(pl.program_id(0) == 0, pl.program_id(1) == 0))
    def _():
        u_s[:, ts:ts + HALO, :] = jnp.zeros((n_slab, HALO, LANES), F32)
        v_s[:, ts:ts + HALO, :] = jnp.zeros((n_slab, HALO, LANES), F32)
        w_pool_s[...] = jnp.zeros(w_pool_s.shape, BF16)
        for grp in range(n_slab):
            k, lo = grp // 2, (grp % 2) * LANES
            w_pool_s[k, lo:lo + LANES, lo:lo + LANES] = w_pool_ref[grp].astype(BF16)

    shift = mod_ref[0, :, 0:d]
    scale1 = 1.0 + mod_ref[0, :, d:2 * d]
    gate = mod_ref[0, :, 2 * d:3 * d] * (1.0 / DEEPNORM_ALPHA)
    eps_post = LN_EPS / (DEEPNORM_ALPHA * DEEPNORM_ALPHA)
    never = pl.program_id(0) < 0

    def tile(j, carry):
        r0 = pl.multiple_of(j * ts, ts)
        first = jnp.logical_and(pl.program_id(1) == 0, j == 0)

        def x_rows(r, n):
            return x_ref[0, pl.ds(r0 + r, n), :]

        u_s[:, 0:HALO, :] = jnp.where(first, 0.0, u_s[:, ts:ts + HALO, :])
        v_s[:, 0:HALO, :] = jnp.where(first, 0.0, v_s[:, ts:ts + HALO, :])

        for r in range(0, ts, LN_ROWS):
            h = _layer_norm(x_rows(r, LN_ROWS)) * _rows(scale1, LN_ROWS) + _rows(shift, LN_ROWS)
            h_s[r:r + LN_ROWS, :] = h.astype(BF16)

        h = h_s[...]

        def project(lo, hi):
            return (jnp.dot(h, w_in_ref[:, lo:hi], preferred_element_type=F32)
                    + _rows(vec("b_in", lo, hi), ts))

        def glu(cb):
            bias = jnp.concatenate([vec("b_in", glu_a + cb * LANES, glu_a + (cb + 1) * LANES),
                                    vec("b_in", glu_g + cb * LANES, glu_g + (cb + 1) * LANES)], axis=1)
            ag = (jnp.dot(h, w_in_ref[:, glu_a + 2 * cb * LANES:glu_a + 2 * (cb + 1) * LANES],
                          preferred_element_type=F32)
                  + _rows(bias, ts))
            v_s[cb, HALO:HALO + ts, :] = ag[:, 0:LANES] * jax.nn.sigmoid(ag[:, LANES:])

        def conv(cb, after=None):
            lanes = slice(cb * LANES, (cb + 1) * LANES)
            taps = [w_dw_ref[k, :, lanes] for k in range(CONV_WIDTH)]
            bias = _rows(vec("b_dw", cb * LANES, (cb + 1) * LANES), 2 * SUBLANES)
            for t0 in range(0, ts, 2 * BLOCK_ROWS):
                acc = [bias] * ROW_STRIDE
                part = [None] * ROW_STRIDE
                for qi in range(1 - CONV_WIDTH, ROW_STRIDE):
                    va = _strided_rows(v_s, cb, HALO + t0 + qi)
                    vb = _strided_rows(v_s, cb, HALO + t0 + BLOCK_ROWS + qi)
                    if after is not None and qi == 1 - CONV_WIDTH:
                        va = jnp.where(never, after, va)
                    vin = jnp.concatenate([va, vb], axis=0).astype(BF16)
                    for q in range(ROW_STRIDE):
                        delay = q - qi
                        if 0 <= delay < CONV_WIDTH:
                            term = vin * taps[CONV_WIDTH - 1 - delay]
                            part[q] = term if part[q] is None else part[q] + term
                            if delay % CONV_GROUP == 0:
                                acc[q] = acc[q] + part[q].astype(F32)
                                part[q] = None
                for q in range(ROW_STRIDE):
                    cn_s[cb, pl.ds(t0 + q, SUBLANES, stride=ROW_STRIDE), :] = acc[q][0:SUBLANES]
                    cn_s[cb, pl.ds(t0 + BLOCK_ROWS + q, SUBLANES, stride=ROW_STRIDE), :] = acc[q][SUBLANES:]

        for cb in range(n_slab):
            glu(cb)
        conv(0)
        u = project(0, dp)
        for g in range(n_slab):
            u_s[g, HALO:HALO + ts, :] = u[:, g * LANES:(g + 1) * LANES]
        conv(1, after=u[-SUBLANES:, -LANES:])
        za = _silu(project(dp, 2 * dp)) * _rows(vec("ls"), ts)
        zg_s[:, 0:dp] = za
        conv(2, after=za[-SUBLANES:, -LANES:])
        zb = _silu(project(z_b, z_b + dp))
        zg_s[:, dp:] = zb
        conv(3, after=zb[-SUBLANES:, -LANES:])

        for t0 in range(0, ts, BLOCK_ROWS):
            for g, w in enumerate(POOL_WINDOWS):
                s = {q: _strided_rows(u_s, g, HALO + t0 + q) for q in range(1 - w, ROW_STRIDE)}
                cur = s
                span = 1
                while span < w:
                    s = {q: s[q] + s[q - span] for q in s if q - span in s}
                    span *= 2
                for q in range(ROW_STRIDE):
                    inv = jnp.where(first, invc_ref[g, q], 1.0 / w) if t0 == 0 else 1.0 / w
                    pn_s[g, pl.ds(t0 + q, SUBLANES, stride=ROW_STRIDE), :] = s[q] * inv - cur[q]

        p = jnp.concatenate([pn_s[g] for g in range(n_slab)], axis=1).astype(BF16)
        half = dp // 2
        for k in range(2):
            cols = slice(k * half, (k + 1) * half)
            ya = (jnp.dot(p[:, cols], w_pool_s[k], preferred_element_type=F32)
                  + _rows(vec("b_pool", k * half, (k + 1) * half), ts)) * zg_s[:, cols]
            cat_s[:, cols] = ya.astype(BF16)

        for r in range(0, ts, LN_ROWS):
            cv = jnp.concatenate([cn_s[cb, r:r + LN_ROWS, :] for cb in range(n_slab)], axis=1)
            vn = _layer_norm(cv) * _rows(vec("lncg"), LN_ROWS) + _rows(vec("lncb"), LN_ROWS)
            c_s[r:r + LN_ROWS, :] = _silu(vn).astype(BF16)

        yb = (jnp.dot(c_s[...], w_pw_ref[...], preferred_element_type=F32)
              + _rows(vec("b_pw"), ts)) * zg_s[:, dp:]
        cat_s[:, dp:] = yb.astype(BF16)

        y = (jnp.dot(cat_s[...], w_out_ref[...], preferred_element_type=F32)
             + _rows(vec("b_out"), ts))
        yo_s[...] = x_rows(0, ts) + _rows(gate, ts) * y
        for r in range(0, ts, LN_ROWS):
            t = yo_s[r:r + LN_ROWS, :]
            o = _layer_norm(t, eps_post) * _rows(vec("lnpg"), LN_ROWS) + _rows(vec("lnpb"), LN_ROWS)
            out_ref[0, pl.ds(r0 + r, LN_ROWS), :] = o
        return carry

    lax.fori_loop(0, TILES_PER_STEP, tile, 0)


def _inv_count_table():
    q = np.arange(ROW_STRIDE, dtype=np.float64)[:, None]
    s = np.arange(SUBLANES, dtype=np.float64)[None, :]
    t1 = q + ROW_STRIDE * s + 1.0
    tab = np.stack([1.0 / np.minimum(t1, float(w)) for w in POOL_WINDOWS])
    return jnp.asarray(np.broadcast_to(tab[..., None], tab.shape + (LANES,)), dtype=F32)


def _const_spec(shape):
    return pl.BlockSpec(shape, lambda b, i: (0,) * len(shape))


@jax.jit
def kernel(x, c, w_ada, b_ada, w_in, b_in, w_pool, b_pool, ls_pool, w_dw, b_dw, ln_conv_g, ln_conv_b,
           w_pw, b_pw, w_out, b_out, ln_post_g, ln_post_b):
    bsz, seq, d = x.shape
    n_grp, gw = w_pool.shape[0], w_pool.shape[1]
    dp = n_grp * gw
    dc = w_pw.shape[0]
    d_in = w_in.shape[1]
    step_rows = SEQ_TILE * TILES_PER_STEP
    assert seq % step_rows == 0 and SEQ_TILE % BLOCK_ROWS == 0
    assert gw == LANES and n_grp == len(POOL_WINDOWS) and dp == dc
    assert d_in == 2 * dp + 3 * dc and w_dw.shape[0] == CONV_WIDTH
    assert HALO >= CONV_WIDTH - 1 and HALO >= max(POOL_WINDOWS) - 1
    n_slab = dp // LANES

    vectors = (b_in, b_pool.reshape(-1), ls_pool, b_dw, ln_conv_g, ln_conv_b, b_pw, b_out,
               ln_post_g, ln_post_b)
    vec_at, off = {}, 0
    for name, v in zip(_VEC_NAMES, vectors):
        assert v.shape[0] % LANES == 0
        vec_at[name] = (off, v.shape[0])
        off += v.shape[0]

    def whole(shape):
        return pl.BlockSpec(shape, lambda j: (0,) * len(shape))

    mod, vec8, w_dw8 = pl.pallas_call(
        functools.partial(_mod_kernel, vec_at),
        grid=(3 * d // MOD_COLS,),
        in_specs=[whole((bsz, d)),
                  pl.BlockSpec((d, MOD_COLS), lambda j: (0, j)),
                  pl.BlockSpec((1, MOD_COLS), lambda j: (0, j)),
                  whole(w_dw.shape)] + [whole((1, v.shape[0])) for v in vectors],
        out_specs=[pl.BlockSpec((bsz * SUBLANES, MOD_COLS), lambda j: (0, j)),
                   whole((SUBLANES, off)), whole((CONV_WIDTH, 2 * SUBLANES, dc))],
        out_shape=[jax.ShapeDtypeStruct((bsz * SUBLANES, 3 * d), F32),
                   jax.ShapeDtypeStruct((SUBLANES, off), F32),
                   jax.ShapeDtypeStruct((CONV_WIDTH, 2 * SUBLANES, dc), BF16)],
        compiler_params=pltpu.CompilerParams(dimension_semantics=("arbitrary",)),
        name="adaln_mod",
    )(c, w_ada, b_ada[None, :], w_dw, *(v.astype(F32).reshape(1, -1) for v in vectors))
    mod = mod.reshape(bsz, SUBLANES, 3 * d)

    glu_cols = w_in[:, 2 * dp:2 * dp + 2 * dc].reshape(d, 2, dc // LANES, LANES)
    glu_cols = glu_cols.transpose(0, 2, 1, 3).reshape(d, 2 * dc)
    w_in_bf = jnp.concatenate([w_in[:, :2 * dp], glu_cols, w_in[:, 2 * dp + 2 * dc:]], axis=1).astype(BF16)

    operands = (x, mod, vec8, w_in_bf, w_pool, _inv_count_table(), w_dw8,
                w_pw.astype(BF16), w_out.astype(BF16))
    in_specs = [
        pl.BlockSpec((1, step_rows, d), lambda b, i: (b, i, 0)),
        pl.BlockSpec((1, SUBLANES, 3 * d), lambda b, i: (b, 0, 0)),
    ] + [_const_spec(op.shape) for op in operands[2:]]

    return pl.pallas_call(
        functools.partial(_layer_kernel, vec_at),
        grid=(bsz, seq // step_rows),
        in_specs=in_specs,
        out_specs=pl.BlockSpec((1, step_rows, d), lambda b, i: (b, i, 0)),
        out_shape=jax.ShapeDtypeStruct((bsz, seq, d), x.dtype),
        scratch_shapes=[
            pltpu.VMEM((SEQ_TILE, d), BF16),
            pltpu.VMEM((n_slab, HALO + SEQ_TILE, LANES), F32),
            pltpu.VMEM((SEQ_TILE, dp + dc), F32),
            pltpu.VMEM((n_slab, SEQ_TILE, LANES), F32),
            pltpu.VMEM((n_slab, HALO + SEQ_TILE, LANES), F32),
            pltpu.VMEM((n_slab, SEQ_TILE, LANES), F32),
            pltpu.VMEM((SEQ_TILE, dc), BF16),
            pltpu.VMEM((SEQ_TILE, dp + dc), BF16),
            pltpu.VMEM((SEQ_TILE, d), F32),
            pltpu.VMEM((n_grp // 2, 2 * gw, 2 * gw), BF16),
        ],
        compiler_params=pltpu.CompilerParams(
            dimension_semantics=("arbitrary", "arbitrary"),
            vmem_limit_bytes=VMEM_LIMIT_BYTES,
        ),
        name="hybrid_layer",
    )(*operands)
```

```python
import functools

import numpy as np
import jax
import jax.numpy as jnp
from jax import lax
from jax.experimental import pallas as pl
from jax.experimental.pallas import tpu as pltpu

F32 = jnp.float32
BF16 = jnp.bfloat16

POOL_WINDOWS = (2, 4, 8, 16)
CONV_WIDTH = 31
LN_EPS = 1e-5
DEEPNORM_ALPHA = 2.0 ** 0.25

SUBLANES = 8
LANES = 128
SEQ_TILE = 512
TILES_PER_STEP = 1
HALO = 32
ROW_STRIDE = 4
BLOCK_ROWS = SUBLANES * ROW_STRIDE
CONV_GROUP = 4
LN_ROWS = 16
MOD_COLS = 768
VMEM_LIMIT_BYTES = 48 * 1024 * 1024
MOD_VMEM_LIMIT_BYTES = 32 * 1024 * 1024

_VEC_NAMES = ("b_in", "b_pool", "ls", "b_dw", "lncg", "lncb", "b_pw", "b_out", "lnpg", "lnpb")


def _rows(p8, n):
    return p8 if n == SUBLANES else jnp.concatenate([p8] * (n // SUBLANES), axis=0)


def _silu(z):
    return z * jax.nn.sigmoid(z)


def _layer_norm(t, eps=LN_EPS):
    mu = jnp.mean(t, axis=-1, keepdims=True)
    tc = t - mu
    var = jnp.mean(tc * tc, axis=-1, keepdims=True)
    return tc * lax.rsqrt(var + eps)


def _strided_rows(ref, slab, start):
    return ref[slab, pl.ds(start, SUBLANES, stride=ROW_STRIDE), :]


def _mod_kernel(vec_at, glu_at, c_ref, w_ref, b_ref, w_dw_ref, w_in_ref, *refs):
    vec_refs, (o_ref, vec8_ref, w_dw8_ref, w_in_bf_ref) = refs[:-4], refs[-4:]
    bsz = c_ref.shape[0]
    glu_a, glu_g, n_glu = glu_at
    w_in_bf_ref[:, 0:glu_a] = w_in_ref[:, 0:glu_a].astype(BF16)
    for cb in range(n_glu):
        for part, src in enumerate((glu_a, glu_g)):
            dst = glu_a + (2 * cb + part) * LANES
            w_in_bf_ref[:, dst:dst + LANES] = w_in_ref[:, src + cb * LANES:src + (cb + 1) * LANES].astype(BF16)
    tail = glu_a + 2 * n_glu * LANES
    w_in_bf_ref[:, tail:] = w_in_ref[:, tail:].astype(BF16)
    c = jnp.concatenate([jnp.broadcast_to(c_ref[b:b + 1, :], (SUBLANES, c_ref.shape[1]))
                         for b in range(bsz)], axis=0)
    s = _silu(c).astype(BF16)
    o_ref[...] = jnp.dot(s, w_ref[...].astype(BF16), preferred_element_type=F32) + b_ref[...]

    @pl.when(pl.program_id(0) == 0)
    def _():
        for name, ref in zip(_VEC_NAMES, vec_refs):
            off, size = vec_at[name]
            vec8_ref[:, off:off + size] = jnp.broadcast_to(ref[...], (SUBLANES, size))
        w_dw8_ref[...] = jnp.broadcast_to(w_dw_ref[...], w_dw8_ref.shape).astype(w_dw8_ref.dtype)


def _layer_kernel(vec_at, x_ref, mod_ref, vec_ref, w_in_ref, w_pool_ref, invc_ref, w_dw_ref, w_pw_ref,
                  w_out_ref, out_ref, h_s, u_s, zg_s, pn_s, v_s, cn_s, c_s, cat_s, yo_s, w_pool_s):
    ts = SEQ_TILE
    d = x_ref.shape[-1]
    n_slab = u_s.shape[0]
    dp = n_slab * LANES
    glu_a, glu_g, z_b = 2 * dp, 3 * dp, 4 * dp

    def vec(name, lo=0, hi=None):
        off, size = vec_at[name]
        return vec_ref[:, off + lo:off + (size if hi is None else hi)]

    @pl.when(jnp.logical_and(pl.program_id(0) == 0, pl.program_id(1) == 0))
    def _():
        u_s[:, ts:ts + HALO, :] = jnp.zeros((n_slab, HALO, LANES), F32)
        v_s[:, ts:ts + HALO, :] = jnp.zeros((n_slab, HALO, LANES), F32)
        w_pool_s[...] = jnp.zeros(w_pool_s.shape, BF16)
        for grp in range(n_slab):
            k, lo = grp // 2, (grp % 2) * LANES
            w_pool_s[k, lo:lo + LANES, lo:lo + LANES] = w_pool_ref[grp].astype(BF16)

    shift = mod_ref[0, :, 0:d]
    scale1 = 1.0 + mod_ref[0, :, d:2 * d]
    gate = mod_ref[0, :, 2 * d:3 * d] * (1.0 / DEEPNORM_ALPHA)
    eps_post = LN_EPS / (DEEPNORM_ALPHA * DEEPNORM_ALPHA)
    never = pl.program_id(0) < 0

    def tile(j, carry):
        r0 = pl.multiple_of(j * ts, ts)
        first = jnp.logical_and(pl.program_id(1) == 0, j == 0)

        def x_rows(r, n):
            return x_ref[0, pl.ds(r0 + r, n), :]

        u_s[:, 0:HALO, :] = jnp.where(first, 0.0, u_s[:, ts:ts + HALO, :])
        v_s[:, 0:HALO, :] = jnp.where(first, 0.0, v_s[:, ts:ts + HALO, :])

        for r in range(0, ts, LN_ROWS):
            h = _layer_norm(x_rows(r, LN_ROWS)) * _rows(scale1, LN_ROWS) + _rows(shift, LN_ROWS)
            h_s[r:r + LN_ROWS, :] = h.astype(BF16)

        h = h_s[...]

        def project(lo, hi):
            return (jnp.dot(h, w_in_ref[:, lo:hi], preferred_element_type=F32)
                    + _rows(vec("b_in", lo, hi), ts))

        def glu(cb):
            bias = jnp.concatenate([vec("b_in", glu_a + cb * LANES, glu_a + (cb + 1) * LANES),
                                    vec("b_in", glu_g + cb * LANES, glu_g + (cb + 1) * LANES)], axis=1)
            ag = (jnp.dot(h, w_in_ref[:, glu_a + 2 * cb * LANES:glu_a + 2 * (cb + 1) * LANES],
                          preferred_element_type=F32)
                  + _rows(bias, ts))
            v_s[cb, HALO:HALO + ts, :] = ag[:, 0:LANES] * jax.nn.sigmoid(ag[:, LANES:])

        def conv(cb, after=None):
            lanes = slice(cb * LANES, (cb + 1) * LANES)
            taps = [w_dw_ref[k, :, lanes] for k in range(CONV_WIDTH)]
            bias = _rows(vec("b_dw", cb * LANES, (cb + 1) * LANES), 2 * SUBLANES)
            for t0 in range(0, ts, 2 * BLOCK_ROWS):
                acc = [bias] * ROW_STRIDE
                part = [None] * ROW_STRIDE
                for qi in range(1 - CONV_WIDTH, ROW_STRIDE):
                    va = _strided_rows(v_s, cb, HALO + t0 + qi)
                    vb = _strided_rows(v_s, cb, HALO + t0 + BLOCK_ROWS + qi)
                    if after is not None and qi == 1 - CONV_WIDTH:
                        va = jnp.where(never, after, va)
                    vin = jnp.concatenate([va, vb], axis=0).astype(BF16)
                    for q in range(ROW_STRIDE):
                        delay = q - qi
                        if 0 <= delay < CONV_WIDTH:
                            term = vin * taps[CONV_WIDTH - 1 - delay]
                            part[q] = term if part[q] is None else part[q] + term
                            if delay % CONV_GROUP == 0:
                                acc[q] = acc[q] + part[q].astype(F32)
                                part[q] = None
                for q in range(ROW_STRIDE):
                    cn_s[cb, pl.ds(t0 + q, SUBLANES, stride=ROW_STRIDE), :] = acc[q][0:SUBLANES]
                    cn_s[cb, pl.ds(t0 + BLOCK_ROWS + q, SUBLANES, stride=ROW_STRIDE), :] = acc[q][SUBLANES:]

        for cb in range(n_slab):
            glu(cb)
        conv(0)
        u = project(0, dp)
        for g in range(n_slab):
            u_s[g, HALO:HALO + ts, :] = u[:, g * LANES:(g + 1) * LANES]
        conv(1, after=u[-SUBLANES:, -LANES:])
        za = _silu(project(dp, 2 * dp)) * _rows(vec("ls"), ts)
        zg_s[:, 0:dp] = za
        conv(2, after=za[-SUBLANES:, -LANES:])
        zb = _silu(project(z_b, z_b + dp))
        zg_s[:, dp:] = zb
        conv(3, after=zb[-SUBLANES:, -LANES:])

        for t0 in range(0, ts, BLOCK_ROWS):
            for g, w in enumerate(POOL_WINDOWS):
                s = {q: _strided_rows(u_s, g, HALO + t0 + q) for q in range(1 - w, ROW_STRIDE)}
                cur = s
                span = 1
                while span < w:
                    s = {q: s[q] + s[q - span] for q in s if q - span in s}
                    span *= 2
                for q in range(ROW_STRIDE):
                    inv = jnp.where(first, invc_ref[g, q], 1.0 / w) if t0 == 0 else 1.0 / w
                    pn_s[g, pl.ds(t0 + q, SUBLANES, stride=ROW_STRIDE), :] = s[q] * inv - cur[q]

        p = jnp.concatenate([pn_s[g] for g in range(n_slab)], axis=1).astype(BF16)
        half = dp // 2
        for k in range(2):
            cols = slice(k * half, (k + 1) * half)
            ya = (jnp.dot(p[:, cols], w_pool_s[k], preferred_element_type=F32)
                  + _rows(vec("b_pool", k * half, (k + 1) * half), ts)) * zg_s[:, cols]
            cat_s[:, cols] = ya.astype(BF16)

        for r in range(0, ts, LN_ROWS):
            cv = jnp.concatenate([cn_s[cb, r:r + LN_ROWS, :] for cb in range(n_slab)], axis=1)
            vn = _layer_norm(cv) * _rows(vec("lncg"), LN_ROWS) + _rows(vec("lncb"), LN_ROWS)
            c_s[r:r + LN_ROWS, :] = _silu(vn).astype(BF16)

        yb = (jnp.dot(c_s[...], w_pw_ref[...], preferred_element_type=F32)
              + _rows(vec("b_pw"), ts)) * zg_s[:, dp:]
        cat_s[:, dp:] = yb.astype(BF16)

        y = (jnp.dot(cat_s[...], w_out_ref[...], preferred_element_type=F32)
             + _rows(vec("b_out"), ts))
        yo_s[...] = x_rows(0, ts) + _rows(gate, ts) * y
        for r in range(0, ts, LN_ROWS):
            t = yo_s[r:r + LN_ROWS, :]
            o = _layer_norm(t, eps_post) * _rows(vec("lnpg"), LN_ROWS) + _rows(vec("lnpb"), LN_ROWS)
            out_ref[0, pl.ds(r0 + r, LN_ROWS), :] = o
        return carry

    lax.fori_loop(0, TILES_PER_STEP, tile, 0)


def _inv_count_table():
    q = np.arange(ROW_STRIDE, dtype=np.float64)[:, None]
    s = np.arange(SUBLANES, dtype=np.float64)[None, :]
    t1 = q + ROW_STRIDE * s + 1.0
    tab = np.stack([1.0 / np.minimum(t1, float(w)) for w in POOL_WINDOWS])
    return jnp.asarray(np.broadcast_to(tab[..., None], tab.shape + (LANES,)), dtype=F32)


def _const_spec(shape):
    return pl.BlockSpec(shape, lambda b, i: (0,) * len(shape))


@jax.jit
def kernel(x, c, w_ada, b_ada, w_in, b_in, w_pool, b_pool, ls_pool, w_dw, b_dw, ln_conv_g, ln_conv_b,
           w_pw, b_pw, w_out, b_out, ln_post_g, ln_post_b):
    bsz, seq, d = x.shape
    n_grp, gw = w_pool.shape[0], w_pool.shape[1]
    dp = n_grp * gw
    dc = w_pw.shape[0]
    d_in = w_in.shape[1]
    step_rows = SEQ_TILE * TILES_PER_STEP
    assert seq % step_rows == 0 and SEQ_TILE % BLOCK_ROWS == 0
    assert gw == LANES and n_grp == len(POOL_WINDOWS) and dp == dc
    assert d_in == 2 * dp + 3 * dc and w_dw.shape[0] == CONV_WIDTH
    assert HALO >= CONV_WIDTH - 1 and HALO >= max(POOL_WINDOWS) - 1
    n_slab = dp // LANES

    vectors = (b_in, b_pool.reshape(-1), ls_pool, b_dw, ln_conv_g, ln_conv_b, b_pw, b_out,
               ln_post_g, ln_post_b)
    vec_at, off = {}, 0
    for name, v in zip(_VEC_NAMES, vectors):
        assert v.shape[0] % LANES == 0
        vec_at[name] = (off, v.shape[0])
        off += v.shape[0]

    def whole(shape):
        return pl.BlockSpec(shape, lambda j: (0,) * len(shape))

    mod_steps = 3 * d // MOD_COLS
    assert d % mod_steps == 0 and (d // mod_steps) % (2 * SUBLANES) == 0
    w_in_rows = d // mod_steps
    mod, vec8, w_dw8, w_in_bf = pl.pallas_call(
        functools.partial(_mod_kernel, vec_at, (2 * dp, 2 * dp + dc, dc // LANES)),
        grid=(mod_steps,),
        in_specs=[whole((bsz, d)),
                  pl.BlockSpec((d, MOD_COLS), lambda j: (0, j)),
                  pl.BlockSpec((1, MOD_COLS), lambda j: (0, j)),
                  whole(w_dw.shape),
                  pl.BlockSpec((w_in_rows, d_in), lambda j: (j, 0))]
                 + [whole((1, v.shape[0])) for v in vectors],
        out_specs=[pl.BlockSpec((bsz * SUBLANES, MOD_COLS), lambda j: (0, j)),
                   whole((SUBLANES, off)), whole((CONV_WIDTH, 2 * SUBLANES, dc)),
                   pl.BlockSpec((w_in_rows, d_in), lambda j: (j, 0))],
        out_shape=[jax.ShapeDtypeStruct((bsz * SUBLANES, 3 * d), F32),
                   jax.ShapeDtypeStruct((SUBLANES, off), F32),
                   jax.ShapeDtypeStruct((CONV_WIDTH, 2 * SUBLANES, dc), BF16),
                   jax.ShapeDtypeStruct((d, d_in), BF16)],
        compiler_params=pltpu.CompilerParams(dimension_semantics=("arbitrary",),
                                             vmem_limit_bytes=MOD_VMEM_LIMIT_BYTES),
        name="adaln_mod",
    )(c, w_ada, b_ada[None, :], w_dw, w_in, *(v.astype(F32).reshape(1, -1) for v in vectors))
    mod = mod.reshape(bsz, SUBLANES, 3 * d)

    operands = (x, mod, vec8, w_in_bf, w_pool, _inv_count_table(), w_dw8,
                w_pw.astype(BF16), w_out.astype(BF16))
    in_specs = [
        pl.BlockSpec((1, step_rows, d), lambda b, i: (b, i, 0)),
        pl.BlockSpec((1, SUBLANES, 3 * d), lambda b, i: (b, 0, 0)),
    ] + [_const_spec(op.shape) for op in operands[2:]]

    return pl.pallas_call(
        functools.partial(_layer_kernel, vec_at),
        grid=(bsz, seq // step_rows),
        in_specs=in_specs,
        out_specs=pl.BlockSpec((1, step_rows, d), lambda b, i: (b, i, 0)),
        out_shape=jax.ShapeDtypeStruct((bsz, seq, d), x.dtype),
        scratch_shapes=[
            pltpu.VMEM((SEQ_TILE, d), BF16),
            pltpu.VMEM((n_slab, HALO + SEQ_TILE, LANES), F32),
            pltpu.VMEM((SEQ_TILE, dp + dc), F32),
            pltpu.VMEM((n_slab, SEQ_TILE, LANES), F32),
            pltpu.VMEM((n_slab, HALO + SEQ_TILE, LANES), F32),
            pltpu.VMEM((n_slab, SEQ_TILE, LANES), F32),
            pltpu.VMEM((SEQ_TILE, dc), BF16),
            pltpu.VMEM((SEQ_TILE, dp + dc), BF16),
            pltpu.VMEM((SEQ_TILE, d), F32),
            pltpu.VMEM((n_grp // 2, 2 * gw, 2 * gw), BF16),
        ],
        compiler_params=pltpu.CompilerParams(
            dimension_semantics=("arbitrary", "arbitrary"),
            vmem_limit_bytes=VMEM_LIMIT_BYTES,
        ),
        name="hybrid_layer",
    )(*operands)
```

```python
import functools

import numpy as np
import jax
import jax.numpy as jnp
from jax import lax
from jax.experimental import pallas as pl
from jax.experimental.pallas import tpu as pltpu

F32 = jnp.float32
BF16 = jnp.bfloat16

POOL_WINDOWS = (2, 4, 8, 16)
CONV_WIDTH = 31
LN_EPS = 1e-5
DEEPNORM_ALPHA = 2.0 ** 0.25

SUBLANES = 8
LANES = 128
SEQ_TILE = 512
TILES_PER_STEP = 1
HALO = 32
ROW_STRIDE = 4
BLOCK_ROWS = SUBLANES * ROW_STRIDE
CONV_GROUP = 4
LN_ROWS = 16
MOD_COLS = 768
VMEM_LIMIT_BYTES = 48 * 1024 * 1024

_VEC_NAMES = ("b_in", "b_pool", "ls", "b_dw", "lncg", "lncb", "b_pw", "b_out", "lnpg", "lnpb")


def _rows(p8, n):
    return p8 if n == SUBLANES else jnp.concatenate([p8] * (n // SUBLANES), axis=0)


def _silu(z):
    return z * jax.nn.sigmoid(z)


def _layer_norm(t, eps=LN_EPS):
    mu = jnp.mean(t, axis=-1, keepdims=True)
    tc = t - mu
    var = jnp.mean(tc * tc, axis=-1, keepdims=True)
    return tc * lax.rsqrt(var + eps)


def _strided_rows(ref, slab, start):
    return ref[slab, pl.ds(start, SUBLANES, stride=ROW_STRIDE), :]


def _mod_kernel(vec_at, glu_at, c_ref, w_ref, b_ref, w_dw_ref, w_in_ref, *refs):
    vec_refs, (o_ref, vec8_ref, w_dw8_ref, w_in_bf_ref) = refs[:-4], refs[-4:]
    bsz = c_ref.shape[0]
    glu_a, glu_g, n_glu = glu_at
    w_in_bf_ref[:, 0:glu_a] = w_in_ref[:, 0:glu_a].astype(BF16)
    for cb in range(n_glu):
        for part, src in enumerate((glu_a, glu_g)):
            dst = glu_a + (2 * cb + part) * LANES
            w_in_bf_ref[:, dst:dst + LANES] = w_in_ref[:, src + cb * LANES:src + (cb + 1) * LANES].astype(BF16)
    tail = glu_a + 2 * n_glu * LANES
    w_in_bf_ref[:, tail:] = w_in_ref[:, tail:].astype(BF16)
    c = jnp.concatenate([jnp.broadcast_to(c_ref[b:b + 1, :], (SUBLANES, c_ref.shape[1]))
                         for b in range(bsz)], axis=0)
    s = _silu(c).astype(BF16)
    o_ref[...] = jnp.dot(s, w_ref[...].astype(BF16), preferred_element_type=F32) + b_ref[...]

    @pl.when(pl.program_id(0) == 0)
    def _():
        for name, ref in zip(_VEC_NAMES, vec_refs):
            off, size = vec_at[name]
            vec8_ref[:, off:off + size] = jnp.broadcast_to(ref[...], (SUBLANES, size))
        w_dw8_ref[...] = jnp.broadcast_to(w_dw_ref[...], w_dw8_ref.shape).astype(w_dw8_ref.dtype)


def _layer_kernel(vec_at, x_ref, mod_ref, vec_ref, w_in_ref, w_pool_ref, invc_ref, w_dw_ref, w_pw_ref,
                  w_out_ref, out_ref, h_s, u_s, zg_s, pn_s, v_s, cn_s, c_s, cat_s, yo_s, w_pool_s):
    ts = SEQ_TILE
    d = x_ref.shape[-1]
    n_slab = u_s.shape[0]
    dp = n_slab * LANES
    glu_a, glu_g, z_b = 2 * dp, 3 * dp, 4 * dp

    def vec(name, lo=0, hi=None):
        off, size = vec_at[name]
        return vec_ref[:, off + lo:off + (size if hi is None else hi)]

    @pl.when(jnp.logical_and(pl.program_id(0) == 0, pl.program_id(1) == 0))
    def _():
        u_s[:, ts:ts + HALO, :] = jnp.zeros((n_slab, HALO, LANES), F32)
        v_s[:, ts:ts + HALO, :] = jnp.zeros((n_slab, HALO, LANES), F32)
        w_pool_s[...] = jnp.zeros(w_pool_s.shape, BF16)
        for grp in range(n_slab):
            k, lo = grp // 2, (grp % 2) * LANES
            w_pool_s[k, lo:lo + LANES, lo:lo + LANES] = w_pool_ref[grp].astype(BF16)

    shift = mod_ref[0, :, 0:d]
    scale1 = 1.0 + mod_ref[0, :, d:2 * d]
    gate = mod_ref[0, :, 2 * d:3 * d] * (1.0 / DEEPNORM_ALPHA)
    eps_post = LN_EPS / (DEEPNORM_ALPHA * DEEPNORM_ALPHA)
    never = pl.program_id(0) < 0

    def tile(j, carry):
        r0 = pl.multiple_of(j * ts, ts)
        first = jnp.logical_and(pl.program_id(1) == 0, j == 0)

        def x_rows(r, n):
            return x_ref[0, pl.ds(r0 + r, n), :]

        u_s[:, 0:HALO, :] = jnp.where(first, 0.0, u_s[:, ts:ts + HALO, :])
        v_s[:, 0:HALO, :] = jnp.where(first, 0.0, v_s[:, ts:ts + HALO, :])

        for r in range(0, ts, LN_ROWS):
            h = _layer_norm(x_rows(r, LN_ROWS)) * _rows(scale1, LN_ROWS) + _rows(shift, LN_ROWS)
            h_s[r:r + LN_ROWS, :] = h.astype(BF16)

        h = h_s[...]

        def project(lo, hi):
            return (jnp.dot(h, w_in_ref[:, lo:hi], preferred_element_type=F32)
                    + _rows(vec("b_in", lo, hi), ts))

        def glu(cb):
            bias = jnp.concatenate([vec("b_in", glu_a + cb * LANES, glu_a + (cb + 1) * LANES),
                                    vec("b_in", glu_g + cb * LANES, glu_g + (cb + 1) * LANES)], axis=1)
            ag = (jnp.dot(h, w_in_ref[:, glu_a + 2 * cb * LANES:glu_a + 2 * (cb + 1) * LANES],
                          preferred_element_type=F32)
                  + _rows(bias, ts))
            v_s[cb, HALO:HALO + ts, :] = ag[:, 0:LANES] * jax.nn.sigmoid(ag[:, LANES:])

        def conv(cb, after=None):
            lanes = slice(cb * LANES, (cb + 1) * LANES)
            taps = [w_dw_ref[k, :, lanes] for k in range(CONV_WIDTH)]
            bias = _rows(vec("b_dw", cb * LANES, (cb + 1) * LANES), 2 * SUBLANES)
            for t0 in range(0, ts, 2 * BLOCK_ROWS):
                acc = [bias] * ROW_STRIDE
                part = [None] * ROW_STRIDE
                for qi in range(1 - CONV_WIDTH, ROW_STRIDE):
                    va = _strided_rows(v_s, cb, HALO + t0 + qi)
                    vb = _strided_rows(v_s, cb, HALO + t0 + BLOCK_ROWS + qi)
                    if after is not None and qi == 1 - CONV_WIDTH:
                        va = jnp.where(never, after, va)
                    vin = jnp.concatenate([va, vb], axis=0).astype(BF16)
                    for q in range(ROW_STRIDE):
                        delay = q - qi
                        if 0 <= delay < CONV_WIDTH:
                            term = vin * taps[CONV_WIDTH - 1 - delay]
                            part[q] = term if part[q] is None else part[q] + term
                            if delay % CONV_GROUP == 0:
                                acc[q] = acc[q] + part[q].astype(F32)
                                part[q] = None
                for q in range(ROW_STRIDE):
                    cn_s[cb, pl.ds(t0 + q, SUBLANES, stride=ROW_STRIDE), :] = acc[q][0:SUBLANES]
                    cn_s[cb, pl.ds(t0 + BLOCK_ROWS + q, SUBLANES, stride=ROW_STRIDE), :] = acc[q][SUBLANES:]

        for cb in range(n_slab):
            glu(cb)
        conv(0)
        u = project(0, dp)
        for g in range(n_slab):
            u_s[g, HALO:HALO + ts, :] = u[:, g * LANES:(g + 1) * LANES]
        conv(1, after=u[-SUBLANES:, -LANES:])
        za = _silu(project(dp, 2 * dp)) * _rows(vec("ls"), ts)
        zg_s[:, 0:dp] = za
        conv(2, after=za[-SUBLANES:, -LANES:])
        zb = _silu(project(z_b, z_b + dp))
        zg_s[:, dp:] = zb
        conv(3, after=zb[-SUBLANES:, -LANES:])

        for t0 in range(0, ts, BLOCK_ROWS):
            for g, w in enumerate(POOL_WINDOWS):
                s = {q: _strided_rows(u_s, g, HALO + t0 + q) for q in range(1 - w, ROW_STRIDE)}
                cur = s
                span = 1
                while span < w:
                    s = {q: s[q] + s[q - span] for q in s if q - span in s}
                    span *= 2
                for q in range(ROW_STRIDE):
                    inv = jnp.where(first, invc_ref[g, q], 1.0 / w) if t0 == 0 else 1.0 / w
                    pn_s[g, pl.ds(t0 + q, SUBLANES, stride=ROW_STRIDE), :] = s[q] * inv - cur[q]

        p = jnp.concatenate([pn_s[g] for g in range(n_slab)], axis=1).astype(BF16)
        half = dp // 2
        for k in range(2):
            cols = slice(k * half, (k + 1) * half)
            ya = (jnp.dot(p[:, cols], w_pool_s[k], preferred_element_type=F32)
                  + _rows(vec("b_pool", k * half, (k + 1) * half), ts)) * zg_s[:, cols]
            cat_s[:, cols] = ya.astype(BF16)

        for r in range(0, ts, LN_ROWS):
            cv = jnp.concatenate([cn_s[cb, r:r + LN_ROWS, :] for cb in range(n_slab)], axis=1)
            vn = _layer_norm(cv) * _rows(vec("lncg"), LN_ROWS) + _rows(vec("lncb"), LN_ROWS)
            c_s[r:r + LN_ROWS, :] = _silu(vn).astype(BF16)

        yb = (jnp.dot(c_s[...], w_pw_ref[...], preferred_element_type=F32)
              + _rows(vec("b_pw"), ts)) * zg_s[:, dp:]
        cat_s[:, dp:] = yb.astype(BF16)

        y = (jnp.dot(cat_s[...], w_out_ref[...], preferred_element_type=F32)
             + _rows(vec("b_out"), ts))
        yo_s[...] = x_rows(0, ts) + _rows(gate, ts) * y
        for r in range(0, ts, LN_ROWS):
            t = yo_s[r:r + LN_ROWS, :]
            o = _layer_norm(t, eps_post) * _rows(vec("lnpg"), LN_ROWS) + _rows(vec("lnpb"), LN_ROWS)
            out_ref[0, pl.ds(r0 + r, LN_ROWS), :] = o
        return carry

    lax.fori_loop(0, TILES_PER_STEP, tile, 0)


def _inv_count_table():
    q = np.arange(ROW_STRIDE, dtype=np.float64)[:, None]
    s = np.arange(SUBLANES, dtype=np.float64)[None, :]
    t1 = q + ROW_STRIDE * s + 1.0
    tab = np.stack([1.0 / np.minimum(t1, float(w)) for w in POOL_WINDOWS])
    return jnp.asarray(np.broadcast_to(tab[..., None], tab.shape + (LANES,)), dtype=F32)


def _const_spec(shape):
    return pl.BlockSpec(shape, lambda b, i: (0,) * len(shape))


@jax.jit
def kernel(x, c, w_ada, b_ada, w_in, b_in, w_pool, b_pool, ls_pool, w_dw, b_dw, ln_conv_g, ln_conv_b,
           w_pw, b_pw, w_out, b_out, ln_post_g, ln_post_b):
    bsz, seq, d = x.shape
    n_grp, gw = w_pool.shape[0], w_pool.shape[1]
    dp = n_grp * gw
    dc = w_pw.shape[0]
    d_in = w_in.shape[1]
    step_rows = SEQ_TILE * TILES_PER_STEP
    assert seq % step_rows == 0 and SEQ_TILE % BLOCK_ROWS == 0
    assert gw == LANES and n_grp == len(POOL_WINDOWS) and dp == dc
    assert d_in == 2 * dp + 3 * dc and w_dw.shape[0] == CONV_WIDTH
    assert HALO >= CONV_WIDTH - 1 and HALO >= max(POOL_WINDOWS) - 1
    n_slab = dp // LANES

    vectors = (b_in, b_pool.reshape(-1), ls_pool, b_dw, ln_conv_g, ln_conv_b, b_pw, b_out,
               ln_post_g, ln_post_b)
    vec_at, off = {}, 0
    for name, v in zip(_VEC_NAMES, vectors):
        assert v.shape[0] % LANES == 0
        vec_at[name] = (off, v.shape[0])
        off += v.shape[0]

    def whole(shape):
        return pl.BlockSpec(shape, lambda j: (0,) * len(shape))

    mod_steps = 3 * d // MOD_COLS
    assert d % mod_steps == 0 and (d // mod_steps) % (2 * SUBLANES) == 0
    w_in_rows = d // mod_steps
    mod, vec8, w_dw8, w_in_bf = pl.pallas_call(
        functools.partial(_mod_kernel, vec_at, (2 * dp, 2 * dp + dc, dc // LANES)),
        grid=(mod_steps,),
        in_specs=[whole((bsz, d)),
                  pl.BlockSpec((d, MOD_COLS), lambda j: (0, j)),
                  pl.BlockSpec((1, MOD_COLS), lambda j: (0, j)),
                  whole(w_dw.shape),
                  pl.BlockSpec((w_in_rows, d_in), lambda j: (j, 0))]
                 + [whole((1, v.shape[0])) for v in vectors],
        out_specs=[pl.BlockSpec((bsz * SUBLANES, MOD_COLS), lambda j: (0, j)),
                   whole((SUBLANES, off)), whole((CONV_WIDTH, 2 * SUBLANES, dc)),
                   pl.BlockSpec((w_in_rows, d_in), lambda j: (j, 0))],
        out_shape=[jax.ShapeDtypeStruct((bsz * SUBLANES, 3 * d), F32),
                   jax.ShapeDtypeStruct((SUBLANES, off), F32),
                   jax.ShapeDtypeStruct((CONV_WIDTH, 2 * SUBLANES, dc), BF16),
                   jax.ShapeDtypeStruct((d, d_in), BF16)],
        compiler_params=pltpu.CompilerParams(dimension_semantics=("arbitrary",)),
        name="adaln_mod",
    )(c, w_ada, b_ada[None, :], w_dw, w_in, *(v.astype(F32).reshape(1, -1) for v in vectors))
    mod = mod.reshape(bsz, SUBLANES, 3 * d)

    operands = (x, mod, vec8, w_in_bf, w_pool, _inv_count_table(), w_dw8,
                w_pw.astype(BF16), w_out.astype(BF16))
    in_specs = [
        pl.BlockSpec((1, step_rows, d), lambda b, i: (b, i, 0)),
        pl.BlockSpec((1, SUBLANES, 3 * d), lambda b, i: (b, 0, 0)),
    ] + [_const_spec(op.shape) for op in operands[2:]]

    return pl.pallas_call(
        functools.partial(_layer_kernel, vec_at),
        grid=(bsz, seq // step_rows),
        in_specs=in_specs,
        out_specs=pl.BlockSpec((1, step_rows, d), lambda b, i: (b, i, 0)),
        out_shape=jax.ShapeDtypeStruct((bsz, seq, d), x.dtype),
        scratch_shapes=[
            pltpu.VMEM((SEQ_TILE, d), BF16),
            pltpu.VMEM((n_slab, HALO + SEQ_TILE, LANES), F32),
            pltpu.VMEM((SEQ_TILE, dp + dc), F32),
            pltpu.VMEM((n_slab, SEQ_TILE, LANES), F32),
            pltpu.VMEM((n_slab, HALO + SEQ_TILE, LANES), F32),
            pltpu.VMEM((n_slab, SEQ_TILE, LANES), F32),
            pltpu.VMEM((SEQ_TILE, dc), BF16),
            pltpu.VMEM((SEQ_TILE, dp + dc), BF16),
            pltpu.VMEM((SEQ_TILE, d), F32),
            pltpu.VMEM((n_grp // 2, 2 * gw, 2 * gw), BF16),
        ],
        compiler_params=pltpu.CompilerParams(
            dimension_semantics=("arbitrary", "arbitrary"),
            vmem_limit_bytes=VMEM_LIMIT_BYTES,
        ),
        name="hybrid_layer",
    )(*operands)
```

```python
import functools

import numpy as np
import jax
import jax.numpy as jnp
from jax import lax
from jax.experimental import pallas as pl
from jax.experimental.pallas import tpu as pltpu

F32 = jnp.float32
BF16 = jnp.bfloat16

POOL_WINDOWS = (2, 4, 8, 16)
CONV_WIDTH = 31
LN_EPS = 1e-5
DEEPNORM_ALPHA = 2.0 ** 0.25

SUBLANES = 8
LANES = 128
SEQ_TILE = 512
TILES_PER_STEP = 1
HALO = 32
ROW_STRIDE = 4
BLOCK_ROWS = SUBLANES * ROW_STRIDE
CONV_GROUP = 4
LN_ROWS = 16
MOD_COLS = 768
VMEM_LIMIT_BYTES = 48 * 1024 * 1024

_VEC_NAMES = ("b_in", "b_pool", "ls", "b_dw", "lncg", "lncb", "b_pw", "b_out", "lnpg", "lnpb")


def _rows(p8, n):
    return p8 if n == SUBLANES else jnp.concatenate([p8] * (n // SUBLANES), axis=0)


def _silu(z):
    return z * jax.nn.sigmoid(z)


def _layer_norm(t, eps=LN_EPS):
    mu = jnp.mean(t, axis=-1, keepdims=True)
    tc = t - mu
    var = jnp.mean(tc * tc, axis=-1, keepdims=True)
    return tc * lax.rsqrt(var + eps)


def _strided_rows(ref, slab, start):
    return ref[slab, pl.ds(start, SUBLANES, stride=ROW_STRIDE), :]


def _mod_kernel(vec_at, glu_at, c_ref, w_ref, b_ref, w_dw_ref, w_in_ref, w_out_ref, w_pw_ref, w_pool_ref,
                *refs):
    vec_refs = refs[:-7]
    o_ref, vec8_ref, w_dw8_ref, w_in_bf_ref, w_out_bf_ref, w_pw_bf_ref, w_pool_bf_ref = refs[-7:]
    bsz = c_ref.shape[0]
    glu_a, glu_g, n_glu = glu_at
    w_out_bf_ref[...] = w_out_ref[...].astype(BF16)
    w_in_bf_ref[:, 0:glu_a] = w_in_ref[:, 0:glu_a].astype(BF16)
    for cb in range(n_glu):
        for part, src in enumerate((glu_a, glu_g)):
            dst = glu_a + (2 * cb + part) * LANES
            w_in_bf_ref[:, dst:dst + LANES] = w_in_ref[:, src + cb * LANES:src + (cb + 1) * LANES].astype(BF16)
    tail = glu_a + 2 * n_glu * LANES
    w_in_bf_ref[:, tail:] = w_in_ref[:, tail:].astype(BF16)
    c = jnp.concatenate([jnp.broadcast_to(c_ref[b:b + 1, :], (SUBLANES, c_ref.shape[1]))
                         for b in range(bsz)], axis=0)
    s = _silu(c).astype(BF16)
    o_ref[...] = jnp.dot(s, w_ref[...].astype(BF16), preferred_element_type=F32) + b_ref[...]

    @pl.when(pl.program_id(0) == 0)
    def _():
        for name, ref in zip(_VEC_NAMES, vec_refs):
            off, size = vec_at[name]
            vec8_ref[:, off:off + size] = jnp.broadcast_to(ref[...], (SUBLANES, size))
        w_dw8_ref[...] = jnp.broadcast_to(w_dw_ref[...], w_dw8_ref.shape).astype(w_dw8_ref.dtype)
        w_pw_bf_ref[...] = w_pw_ref[...].astype(BF16)
        w_pool_bf_ref[...] = jnp.zeros(w_pool_bf_ref.shape, BF16)
        for grp in range(w_pool_ref.shape[0]):
            k, lo = grp // 2, (grp % 2) * LANES
            w_pool_bf_ref[k, lo:lo + LANES, lo:lo + LANES] = w_pool_ref[grp].astype(BF16)


def _layer_kernel(vec_at, x_ref, mod_ref, vec_ref, w_in_ref, w_pool_ref, invc_ref, w_dw_ref, w_pw_ref,
                  w_out_ref, out_ref, h_s, u_s, zg_s, pn_s, v_s, cn_s, c_s, cat_s, yo_s):
    ts = SEQ_TILE
    d = x_ref.shape[-1]
    n_slab = u_s.shape[0]
    dp = n_slab * LANES
    glu_a, glu_g, z_b = 2 * dp, 3 * dp, 4 * dp

    def vec(name, lo=0, hi=None):
        off, size = vec_at[name]
        return vec_ref[:, off + lo:off + (size if hi is None else hi)]

    @pl.when(jnp.logical_and(pl.program_id(0) == 0, pl.program_id(1) == 0))
    def _():
        u_s[:, ts:ts + HALO, :] = jnp.zeros((n_slab, HALO, LANES), F32)
        v_s[:, ts:ts + HALO, :] = jnp.zeros((n_slab, HALO, LANES), F32)

    shift = mod_ref[0, :, 0:d]
    scale1 = 1.0 + mod_ref[0, :, d:2 * d]
    gate = mod_ref[0, :, 2 * d:3 * d] * (1.0 / DEEPNORM_ALPHA)
    eps_post = LN_EPS / (DEEPNORM_ALPHA * DEEPNORM_ALPHA)
    never = pl.program_id(0) < 0

    def tile(j, carry):
        r0 = pl.multiple_of(j * ts, ts)
        first = jnp.logical_and(pl.program_id(1) == 0, j == 0)

        def x_rows(r, n):
            return x_ref[0, pl.ds(r0 + r, n), :]

        u_s[:, 0:HALO, :] = jnp.where(first, 0.0, u_s[:, ts:ts + HALO, :])
        v_s[:, 0:HALO, :] = jnp.where(first, 0.0, v_s[:, ts:ts + HALO, :])

        for r in range(0, ts, LN_ROWS):
            h = _layer_norm(x_rows(r, LN_ROWS)) * _rows(scale1, LN_ROWS) + _rows(shift, LN_ROWS)
            h_s[r:r + LN_ROWS, :] = h.astype(BF16)

        h = h_s[...]

        def project(lo, hi):
            return (jnp.dot(h, w_in_ref[:, lo:hi], preferred_element_type=F32)
                    + _rows(vec("b_in", lo, hi), ts))

        def glu(cb):
            bias = jnp.concatenate([vec("b_in", glu_a + cb * LANES, glu_a + (cb + 1) * LANES),
                                    vec("b_in", glu_g + cb * LANES, glu_g + (cb + 1) * LANES)], axis=1)
            ag = (jnp.dot(h, w_in_ref[:, glu_a + 2 * cb * LANES:glu_a + 2 * (cb + 1) * LANES],
                          preferred_element_type=F32)
                  + _rows(bias, ts))
            v_s[cb, HALO:HALO + ts, :] = ag[:, 0:LANES] * jax.nn.sigmoid(ag[:, LANES:])

        def conv(cb, after=None):
            lanes = slice(cb * LANES, (cb + 1) * LANES)
            taps = [w_dw_ref[k, :, lanes] for k in range(CONV_WIDTH)]
            bias = _rows(vec("b_dw", cb * LANES, (cb + 1) * LANES), 2 * SUBLANES)
            for t0 in range(0, ts, 2 * BLOCK_ROWS):
                acc = [bias] * ROW_STRIDE
                part = [None] * ROW_STRIDE
                for qi in range(1 - CONV_WIDTH, ROW_STRIDE):
                    va = _strided_rows(v_s, cb, HALO + t0 + qi)
                    vb = _strided_rows(v_s, cb, HALO + t0 + BLOCK_ROWS + qi)
                    if after is not None and qi == 1 - CONV_WIDTH:
                        va = jnp.where(never, after, va)
                    vin = jnp.concatenate([va, vb], axis=0).astype(BF16)
                    for q in range(ROW_STRIDE):
                        delay = q - qi
                        if 0 <= delay < CONV_WIDTH:
                            term = vin * taps[CONV_WIDTH - 1 - delay]
                            part[q] = term if part[q] is None else part[q] + term
                            if delay % CONV_GROUP == 0:
                                acc[q] = acc[q] + part[q].astype(F32)
                                part[q] = None
                for q in range(ROW_STRIDE):
                    cn_s[cb, pl.ds(t0 + q, SUBLANES, stride=ROW_STRIDE), :] = acc[q][0:SUBLANES]
                    cn_s[cb, pl.ds(t0 + BLOCK_ROWS + q, SUBLANES, stride=ROW_STRIDE), :] = acc[q][SUBLANES:]

        for cb in range(n_slab):
            glu(cb)
        conv(0)
        u = project(0, dp)
        for g in range(n_slab):
            u_s[g, HALO:HALO + ts, :] = u[:, g * LANES:(g + 1) * LANES]
        conv(1, after=u[-SUBLANES:, -LANES:])
        za = _silu(project(dp, 2 * dp)) * _rows(vec("ls"), ts)
        zg_s[:, 0:dp] = za
        conv(2, after=za[-SUBLANES:, -LANES:])
        zb = _silu(project(z_b, z_b + dp))
        zg_s[:, dp:] = zb
        conv(3, after=zb[-SUBLANES:, -LANES:])

        for t0 in range(0, ts, BLOCK_ROWS):
            for g, w in enumerate(POOL_WINDOWS):
                s = {q: _strided_rows(u_s, g, HALO + t0 + q) for q in range(1 - w, ROW_STRIDE)}
                cur = s
                span = 1
                while span < w:
                    s = {q: s[q] + s[q - span] for q in s if q - span in s}
                    span *= 2
                for q in range(ROW_STRIDE):
                    inv = jnp.where(first, invc_ref[g, q], 1.0 / w) if t0 == 0 else 1.0 / w
                    pn_s[g, pl.ds(t0 + q, SUBLANES, stride=ROW_STRIDE), :] = s[q] * inv - cur[q]

        p = jnp.concatenate([pn_s[g] for g in range(n_slab)], axis=1).astype(BF16)
        half = dp // 2
        for k in range(2):
            cols = slice(k * half, (k + 1) * half)
            ya = (jnp.dot(p[:, cols], w_pool_ref[k], preferred_element_type=F32)
                  + _rows(vec("b_pool", k * half, (k + 1) * half), ts)) * zg_s[:, cols]
            cat_s[:, cols] = ya.astype(BF16)

        for r in range(0, ts, LN_ROWS):
            cv = jnp.concatenate([cn_s[cb, r:r + LN_ROWS, :] for cb in range(n_slab)], axis=1)
            vn = _layer_norm(cv) * _rows(vec("lncg"), LN_ROWS) + _rows(vec("lncb"), LN_ROWS)
            c_s[r:r + LN_ROWS, :] = _silu(vn).astype(BF16)

        yb = (jnp.dot(c_s[...], w_pw_ref[...], preferred_element_type=F32)
              + _rows(vec("b_pw"), ts)) * zg_s[:, dp:]
        cat_s[:, dp:] = yb.astype(BF16)

        y = (jnp.dot(cat_s[...], w_out_ref[...], preferred_element_type=F32)
             + _rows(vec("b_out"), ts))
        yo_s[...] = x_rows(0, ts) + _rows(gate, ts) * y
        for r in range(0, ts, LN_ROWS):
            t = yo_s[r:r + LN_ROWS, :]
            o = _layer_norm(t, eps_post) * _rows(vec("lnpg"), LN_ROWS) + _rows(vec("lnpb"), LN_ROWS)
            out_ref[0, pl.ds(r0 + r, LN_ROWS), :] = o
        return carry

    lax.fori_loop(0, TILES_PER_STEP, tile, 0)


def _inv_count_table():
    q = np.arange(ROW_STRIDE, dtype=np.float64)[:, None]
    s = np.arange(SUBLANES, dtype=np.float64)[None, :]
    t1 = q + ROW_STRIDE * s + 1.0
    tab = np.stack([1.0 / np.minimum(t1, float(w)) for w in POOL_WINDOWS])
    return jnp.asarray(np.broadcast_to(tab[..., None], tab.shape + (LANES,)), dtype=F32)


def _const_spec(shape):
    return pl.BlockSpec(shape, lambda b, i: (0,) * len(shape))


@jax.jit
def kernel(x, c, w_ada, b_ada, w_in, b_in, w_pool, b_pool, ls_pool, w_dw, b_dw, ln_conv_g, ln_conv_b,
           w_pw, b_pw, w_out, b_out, ln_post_g, ln_post_b):
    bsz, seq, d = x.shape
    n_grp, gw = w_pool.shape[0], w_pool.shape[1]
    dp = n_grp * gw
    dc = w_pw.shape[0]
    d_in = w_in.shape[1]
    step_rows = SEQ_TILE * TILES_PER_STEP
    assert seq % step_rows == 0 and SEQ_TILE % BLOCK_ROWS == 0
    assert gw == LANES and n_grp == len(POOL_WINDOWS) and dp == dc
    assert d_in == 2 * dp + 3 * dc and w_dw.shape[0] == CONV_WIDTH
    assert HALO >= CONV_WIDTH - 1 and HALO >= max(POOL_WINDOWS) - 1
    n_slab = dp // LANES

    vectors = (b_in, b_pool.reshape(-1), ls_pool, b_dw, ln_conv_g, ln_conv_b, b_pw, b_out,
               ln_post_g, ln_post_b)
    vec_at, off = {}, 0
    for name, v in zip(_VEC_NAMES, vectors):
        assert v.shape[0] % LANES == 0
        vec_at[name] = (off, v.shape[0])
        off += v.shape[0]

    def whole(shape):
        return pl.BlockSpec(shape, lambda j: (0,) * len(shape))

    mod_steps = 3 * d // MOD_COLS
    assert d % mod_steps == 0 and (d // mod_steps) % (2 * SUBLANES) == 0
    w_rows = d // mod_steps
    assert w_out.shape == (dp + dc, d) and (dp + dc) % mod_steps == 0
    w_out_rows = (dp + dc) // mod_steps
    pool_shape = (n_grp // 2, 2 * gw, 2 * gw)
    mod, vec8, w_dw8, w_in_bf, w_out_bf, w_pw_bf, w_pool_bf = pl.pallas_call(
        functools.partial(_mod_kernel, vec_at, (2 * dp, 2 * dp + dc, dc // LANES)),
        grid=(mod_steps,),
        in_specs=[whole((bsz, d)),
                  pl.BlockSpec((d, MOD_COLS), lambda j: (0, j)),
                  pl.BlockSpec((1, MOD_COLS), lambda j: (0, j)),
                  whole(w_dw.shape),
                  pl.BlockSpec((w_rows, d_in), lambda j: (j, 0)),
                  pl.BlockSpec((w_out_rows, d), lambda j: (j, 0)),
                  whole(w_pw.shape), whole(w_pool.shape)]
                 + [whole((1, v.shape[0])) for v in vectors],
        out_specs=[pl.BlockSpec((bsz * SUBLANES, MOD_COLS), lambda j: (0, j)),
                   whole((SUBLANES, off)), whole((CONV_WIDTH, 2 * SUBLANES, dc)),
                   pl.BlockSpec((w_rows, d_in), lambda j: (j, 0)),
                   pl.BlockSpec((w_out_rows, d), lambda j: (j, 0)),
                   whole(w_pw.shape), whole(pool_shape)],
        out_shape=[jax.ShapeDtypeStruct((bsz * SUBLANES, 3 * d), F32),
                   jax.ShapeDtypeStruct((SUBLANES, off), F32),
                   jax.ShapeDtypeStruct((CONV_WIDTH, 2 * SUBLANES, dc), BF16),
                   jax.ShapeDtypeStruct((d, d_in), BF16),
                   jax.ShapeDtypeStruct(w_out.shape, BF16),
                   jax.ShapeDtypeStruct(w_pw.shape, BF16),
                   jax.ShapeDtypeStruct(pool_shape, BF16)],
        compiler_params=pltpu.CompilerParams(dimension_semantics=("arbitrary",)),
        name="adaln_mod",
    )(c, w_ada, b_ada[None, :], w_dw, w_in, w_out, w_pw, w_pool,
      *(v.astype(F32).reshape(1, -1) for v in vectors))
    mod = mod.reshape(bsz, SUBLANES, 3 * d)

    operands = (x, mod, vec8, w_in_bf, w_pool_bf, _inv_count_table(), w_dw8, w_pw_bf, w_out_bf)
    in_specs = [
        pl.BlockSpec((1, step_rows, d), lambda b, i: (b, i, 0)),
        pl.BlockSpec((1, SUBLANES, 3 * d), lambda b, i: (b, 0, 0)),
    ] + [_const_spec(op.shape) for op in operands[2:]]

    return pl.pallas_call(
        functools.partial(_layer_kernel, vec_at),
        grid=(bsz, seq // step_rows),
        in_specs=in_specs,
        out_specs=pl.BlockSpec((1, step_rows, d), lambda b, i: (b, i, 0)),
        out_shape=jax.ShapeDtypeStruct((bsz, seq, d), x.dtype),
        scratch_shapes=[
            pltpu.VMEM((SEQ_TILE, d), BF16),
            pltpu.VMEM((n_slab, HALO + SEQ_TILE, LANES), F32),
            pltpu.VMEM((SEQ_TILE, dp + dc), F32),
            pltpu.VMEM((n_slab, SEQ_TILE, LANES), F32),
            pltpu.VMEM((n_slab, HALO + SEQ_TILE, LANES), F32),
            pltpu.VMEM((n_slab, SEQ_TILE, LANES), F32),
            pltpu.VMEM((SEQ_TILE, dc), BF16),
            pltpu.VMEM((SEQ_TILE, dp + dc), BF16),
            pltpu.VMEM((SEQ_TILE, d), F32),
        ],
        compiler_params=pltpu.CompilerParams(
            dimension_semantics=("arbitrary", "arbitrary"),
            vmem_limit_bytes=VMEM_LIMIT_BYTES,
        ),
        name="hybrid_layer",
    )(*operands)
```

```python
import functools

import numpy as np
import jax
import jax.numpy as jnp
from jax import lax
from jax.experimental import pallas as pl
from jax.experimental.pallas import tpu as pltpu

F32 = jnp.float32
BF16 = jnp.bfloat16

POOL_WINDOWS = (2, 4, 8, 16)
CONV_WIDTH = 31
LN_EPS = 1e-5
DEEPNORM_ALPHA = 2.0 ** 0.25

SUBLANES = 8
LANES = 128
SEQ_TILE = 512
TILES_PER_STEP = 1
HALO = 32
ROW_STRIDE = 4
BLOCK_ROWS = SUBLANES * ROW_STRIDE
CONV_GROUP = 4
LN_ROWS = 16
MOD_COLS = 384
VMEM_LIMIT_BYTES = 48 * 1024 * 1024

_VEC_NAMES = ("b_in", "b_pool", "ls", "b_dw", "lncg", "lncb", "b_pw", "b_out", "lnpg", "lnpb")


def _rows(p8, n):
    return p8 if n == SUBLANES else jnp.concatenate([p8] * (n // SUBLANES), axis=0)


def _silu(z):
    return z * jax.nn.sigmoid(z)


def _layer_norm(t, eps=LN_EPS):
    mu = jnp.mean(t, axis=-1, keepdims=True)
    tc = t - mu
    var = jnp.mean(tc * tc, axis=-1, keepdims=True)
    return tc * lax.rsqrt(var + eps)


def _strided_rows(ref, slab, start):
    return ref[slab, pl.ds(start, SUBLANES, stride=ROW_STRIDE), :]


def _mod_kernel(vec_at, glu_at, c_ref, w_ref, b_ref, w_dw_ref, w_in_ref, *refs):
    vec_refs, (o_ref, vec8_ref, w_dw8_ref, w_in_bf_ref) = refs[:-4], refs[-4:]
    bsz = c_ref.shape[0]
    glu_a, glu_g, n_glu = glu_at
    w_in_bf_ref[:, 0:glu_a] = w_in_ref[:, 0:glu_a].astype(BF16)
    for cb in range(n_glu):
        for part, src in enumerate((glu_a, glu_g)):
            dst = glu_a + (2 * cb + part) * LANES
            w_in_bf_ref[:, dst:dst + LANES] = w_in_ref[:, src + cb * LANES:src + (cb + 1) * LANES].astype(BF16)
    tail = glu_a + 2 * n_glu * LANES
    w_in_bf_ref[:, tail:] = w_in_ref[:, tail:].astype(BF16)
    c = jnp.concatenate([jnp.broadcast_to(c_ref[b:b + 1, :], (SUBLANES, c_ref.shape[1]))
                         for b in range(bsz)], axis=0)
    s = _silu(c).astype(BF16)
    o_ref[...] = jnp.dot(s, w_ref[...].astype(BF16), preferred_element_type=F32) + b_ref[...]

    @pl.when(pl.program_id(0) == 0)
    def _():
        for name, ref in zip(_VEC_NAMES, vec_refs):
            off, size = vec_at[name]
            vec8_ref[:, off:off + size] = jnp.broadcast_to(ref[...], (SUBLANES, size))
        w_dw8_ref[...] = jnp.broadcast_to(w_dw_ref[...], w_dw8_ref.shape).astype(w_dw8_ref.dtype)


def _layer_kernel(vec_at, x_ref, mod_ref, vec_ref, w_in_ref, w_pool_ref, invc_ref, w_dw_ref, w_pw_ref,
                  w_out_ref, out_ref, h_s, u_s, zg_s, pn_s, v_s, cn_s, c_s, cat_s, yo_s, w_pool_s):
    ts = SEQ_TILE
    d = x_ref.shape[-1]
    n_slab = u_s.shape[0]
    dp = n_slab * LANES
    glu_a, glu_g, z_b = 2 * dp, 3 * dp, 4 * dp

    def vec(name, lo=0, hi=None):
        off, size = vec_at[name]
        return vec_ref[:, off + lo:off + (size if hi is None else hi)]

    @pl.when(jnp.logical_and(pl.program_id(0) == 0, pl.program_id(1) == 0))
    def _():
        u_s[:, ts:ts + HALO, :] = jnp.zeros((n_slab, HALO, LANES), F32)
        v_s[:, ts:ts + HALO, :] = jnp.zeros((n_slab, HALO, LANES), F32)
        w_pool_s[...] = jnp.zeros(w_pool_s.shape, BF16)
        for grp in range(n_slab):
            k, lo = grp // 2, (grp % 2) * LANES
            w_pool_s[k, lo:lo + LANES, lo:lo + LANES] = w_pool_ref[grp].astype(BF16)

    shift = mod_ref[0, :, 0:d]
    scale1 = 1.0 + mod_ref[0, :, d:2 * d]
    gate = mod_ref[0, :, 2 * d:3 * d] * (1.0 / DEEPNORM_ALPHA)
    eps_post = LN_EPS / (DEEPNORM_ALPHA * DEEPNORM_ALPHA)
    never = pl.program_id(0) < 0

    def tile(j, carry):
        r0 = pl.multiple_of(j * ts, ts)
        first = jnp.logical_and(pl.program_id(1) == 0, j == 0)

        def x_rows(r, n):
            return x_ref[0, pl.ds(r0 + r, n), :]

        u_s[:, 0:HALO, :] = jnp.where(first, 0.0, u_s[:, ts:ts + HALO, :])
        v_s[:, 0:HALO, :] = jnp.where(first, 0.0, v_s[:, ts:ts + HALO, :])

        for r in range(0, ts, LN_ROWS):
            h = _layer_norm(x_rows(r, LN_ROWS)) * _rows(scale1, LN_ROWS) + _rows(shift, LN_ROWS)
            h_s[r:r + LN_ROWS, :] = h.astype(BF16)

        h = h_s[...]

        def project(lo, hi):
            return (jnp.dot(h, w_in_ref[:, lo:hi], preferred_element_type=F32)
                    + _rows(vec("b_in", lo, hi), ts))

        def glu(cb):
            bias = jnp.concatenate([vec("b_in", glu_a + cb * LANES, glu_a + (cb + 1) * LANES),
                                    vec("b_in", glu_g + cb * LANES, glu_g + (cb + 1) * LANES)], axis=1)
            ag = (jnp.dot(h, w_in_ref[:, glu_a + 2 * cb * LANES:glu_a + 2 * (cb + 1) * LANES],
                          preferred_element_type=F32)
                  + _rows(bias, ts))
            v_s[cb, HALO:HALO + ts, :] = ag[:, 0:LANES] * jax.nn.sigmoid(ag[:, LANES:])

        def conv(cb, after=None):
            lanes = slice(cb * LANES, (cb + 1) * LANES)
            taps = [w_dw_ref[k, :, lanes] for k in range(CONV_WIDTH)]
            bias = _rows(vec("b_dw", cb * LANES, (cb + 1) * LANES), 2 * SUBLANES)
            for t0 in range(0, ts, 2 * BLOCK_ROWS):
                acc = [bias] * ROW_STRIDE
                part = [None] * ROW_STRIDE
                for qi in range(1 - CONV_WIDTH, ROW_STRIDE):
                    va = _strided_rows(v_s, cb, HALO + t0 + qi)
                    vb = _strided_rows(v_s, cb, HALO + t0 + BLOCK_ROWS + qi)
                    if after is not None and qi == 1 - CONV_WIDTH:
                        va = jnp.where(never, after, va)
                    vin = jnp.concatenate([va, vb], axis=0).astype(BF16)
                    for q in range(ROW_STRIDE):
                        delay = q - qi
                        if 0 <= delay < CONV_WIDTH:
                            term = vin * taps[CONV_WIDTH - 1 - delay]
                            part[q] = term if part[q] is None else part[q] + term
                            if delay % CONV_GROUP == 0:
                                acc[q] = acc[q] + part[q].astype(F32)
                                part[q] = None
                for q in range(ROW_STRIDE):
                    cn_s[cb, pl.ds(t0 + q, SUBLANES, stride=ROW_STRIDE), :] = acc[q][0:SUBLANES]
                    cn_s[cb, pl.ds(t0 + BLOCK_ROWS + q, SUBLANES, stride=ROW_STRIDE), :] = acc[q][SUBLANES:]

        for cb in range(n_slab):
            glu(cb)
        conv(0)
        u = project(0, dp)
        for g in range(n_slab):
            u_s[g, HALO:HALO + ts, :] = u[:, g * LANES:(g + 1) * LANES]
        conv(1, after=u[-SUBLANES:, -LANES:])
        za = _silu(project(dp, 2 * dp)) * _rows(vec("ls"), ts)
        zg_s[:, 0:dp] = za
        conv(2, after=za[-SUBLANES:, -LANES:])
        zb = _silu(project(z_b, z_b + dp))
        zg_s[:, dp:] = zb
        conv(3, after=zb[-SUBLANES:, -LANES:])

        for t0 in range(0, ts, BLOCK_ROWS):
            for g, w in enumerate(POOL_WINDOWS):
                s = {q: _strided_rows(u_s, g, HALO + t0 + q) for q in range(1 - w, ROW_STRIDE)}
                cur = s
                span = 1
                while span < w:
                    s = {q: s[q] + s[q - span] for q in s if q - span in s}
                    span *= 2
                for q in range(ROW_STRIDE):
                    inv = jnp.where(first, invc_ref[g, q], 1.0 / w) if t0 == 0 else 1.0 / w
                    pn_s[g, pl.ds(t0 + q, SUBLANES, stride=ROW_STRIDE), :] = s[q] * inv - cur[q]

        p = jnp.concatenate([pn_s[g] for g in range(n_slab)], axis=1).astype(BF16)
        half = dp // 2
        for k in range(2):
            cols = slice(k * half, (k + 1) * half)
            ya = (jnp.dot(p[:, cols], w_pool_s[k], preferred_element_type=F32)
                  + _rows(vec("b_pool", k * half, (k + 1) * half), ts)) * zg_s[:, cols]
            cat_s[:, cols] = ya.astype(BF16)

        for r in range(0, ts, LN_ROWS):
            cv = jnp.concatenate([cn_s[cb, r:r + LN_ROWS, :] for cb in range(n_slab)], axis=1)
            vn = _layer_norm(cv) * _rows(vec("lncg"), LN_ROWS) + _rows(vec("lncb"), LN_ROWS)
            c_s[r:r + LN_ROWS, :] = _silu(vn).astype(BF16)

        yb = (jnp.dot(c_s[...], w_pw_ref[...], preferred_element_type=F32)
              + _rows(vec("b_pw"), ts)) * zg_s[:, dp:]
        cat_s[:, dp:] = yb.astype(BF16)

        y = (jnp.dot(cat_s[...], w_out_ref[...], preferred_element_type=F32)
             + _rows(vec("b_out"), ts))
        yo_s[...] = x_rows(0, ts) + _rows(gate, ts) * y
        for r in range(0, ts, LN_ROWS):
            t = yo_s[r:r + LN_ROWS, :]
            o = _layer_norm(t, eps_post) * _rows(vec("lnpg"), LN_ROWS) + _rows(vec("lnpb"), LN_ROWS)
            out_ref[0, pl.ds(r0 + r, LN_ROWS), :] = o
        return carry

    lax.fori_loop(0, TILES_PER_STEP, tile, 0)


def _inv_count_table():
    q = np.arange(ROW_STRIDE, dtype=np.float64)[:, None]
    s = np.arange(SUBLANES, dtype=np.float64)[None, :]
    t1 = q + ROW_STRIDE * s + 1.0
    tab = np.stack([1.0 / np.minimum(t1, float(w)) for w in POOL_WINDOWS])
    return jnp.asarray(np.broadcast_to(tab[..., None], tab.shape + (LANES,)), dtype=F32)


def _const_spec(shape):
    return pl.BlockSpec(shape, lambda b, i: (0,) * len(shape))


@jax.jit
def kernel(x, c, w_ada, b_ada, w_in, b_in, w_pool, b_pool, ls_pool, w_dw, b_dw, ln_conv_g, ln_conv_b,
           w_pw, b_pw, w_out, b_out, ln_post_g, ln_post_b):
    bsz, seq, d = x.shape
    n_grp, gw = w_pool.shape[0], w_pool.shape[1]
    dp = n_grp * gw
    dc = w_pw.shape[0]
    d_in = w_in.shape[1]
    step_rows = SEQ_TILE * TILES_PER_STEP
    assert seq % step_rows == 0 and SEQ_TILE % BLOCK_ROWS == 0
    assert gw == LANES and n_grp == len(POOL_WINDOWS) and dp == dc
    assert d_in == 2 * dp + 3 * dc and w_dw.shape[0] == CONV_WIDTH
    assert HALO >= CONV_WIDTH - 1 and HALO >= max(POOL_WINDOWS) - 1
    n_slab = dp // LANES

    vectors = (b_in, b_pool.reshape(-1), ls_pool, b_dw, ln_conv_g, ln_conv_b, b_pw, b_out,
               ln_post_g, ln_post_b)
    vec_at, off = {}, 0
    for name, v in zip(_VEC_NAMES, vectors):
        assert v.shape[0] % LANES == 0
        vec_at[name] = (off, v.shape[0])
        off += v.shape[0]

    def whole(shape):
        return pl.BlockSpec(shape, lambda j: (0,) * len(shape))

    mod_steps = 3 * d // MOD_COLS
    assert d % mod_steps == 0 and (d // mod_steps) % (2 * SUBLANES) == 0
    w_in_rows = d // mod_steps
    mod, vec8, w_dw8, w_in_bf = pl.pallas_call(
        functools.partial(_mod_kernel, vec_at, (2 * dp, 2 * dp + dc, dc // LANES)),
        grid=(mod_steps,),
        in_specs=[whole((bsz, d)),
                  pl.BlockSpec((d, MOD_COLS), lambda j: (0, j)),
                  pl.BlockSpec((1, MOD_COLS), lambda j: (0, j)),
                  whole(w_dw.shape),
                  pl.BlockSpec((w_in_rows, d_in), lambda j: (j, 0))]
                 + [whole((1, v.shape[0])) for v in vectors],
        out_specs=[pl.BlockSpec((bsz * SUBLANES, MOD_COLS), lambda j: (0, j)),
                   whole((SUBLANES, off)), whole((CONV_WIDTH, 2 * SUBLANES, dc)),
                   pl.BlockSpec((w_in_rows, d_in), lambda j: (j, 0))],
        out_shape=[jax.ShapeDtypeStruct((bsz * SUBLANES, 3 * d), F32),
                   jax.ShapeDtypeStruct((SUBLANES, off), F32),
                   jax.ShapeDtypeStruct((CONV_WIDTH, 2 * SUBLANES, dc), BF16),
                   jax.ShapeDtypeStruct((d, d_in), BF16)],
        compiler_params=pltpu.CompilerParams(dimension_semantics=("arbitrary",)),
        name="adaln_mod",
    )(c, w_ada, b_ada[None, :], w_dw, w_in, *(v.astype(F32).reshape(1, -1) for v in vectors))
    mod = mod.reshape(bsz, SUBLANES, 3 * d)

    operands = (x, mod, vec8, w_in_bf, w_pool, _inv_count_table(), w_dw8,
                w_pw.astype(BF16), w_out.astype(BF16))
    in_specs = [
        pl.BlockSpec((1, step_rows, d), lambda b, i: (b, i, 0)),
        pl.BlockSpec((1, SUBLANES, 3 * d), lambda b, i: (b, 0, 0)),
    ] + [_const_spec(op.shape) for op in operands[2:]]

    return pl.pallas_call(
        functools.partial(_layer_kernel, vec_at),
        grid=(bsz, seq // step_rows),
        in_specs=in_specs,
        out_specs=pl.BlockSpec((1, step_rows, d), lambda b, i: (b, i, 0)),
        out_shape=jax.ShapeDtypeStruct((bsz, seq, d), x.dtype),
        scratch_shapes=[
            pltpu.VMEM((SEQ_TILE, d), BF16),
            pltpu.VMEM((n_slab, HALO + SEQ_TILE, LANES), F32),
            pltpu.VMEM((SEQ_TILE, dp + dc), F32),
            pltpu.VMEM((n_slab, SEQ_TILE, LANES), F32),
            pltpu.VMEM((n_slab, HALO + SEQ_TILE, LANES), F32),
            pltpu.VMEM((n_slab, SEQ_TILE, LANES), F32),
            pltpu.VMEM((SEQ_TILE, dc), BF16),
            pltpu.VMEM((SEQ_TILE, dp + dc), BF16),
            pltpu.VMEM((SEQ_TILE, d), F32),
            pltpu.VMEM((n_grp // 2, 2 * gw, 2 * gw), BF16),
        ],
        compiler_params=pltpu.CompilerParams(
            dimension_semantics=("arbitrary", "arbitrary"),
            vmem_limit_bytes=VMEM_LIMIT_BYTES,
        ),
        name="hybrid_layer",
    )(*operands)
```

```python
import functools

import numpy as np
import jax
import jax.numpy as jnp
from jax import lax
from jax.experimental import pallas as pl
from jax.experimental.pallas import tpu as pltpu

F32 = jnp.float32
BF16 = jnp.bfloat16

POOL_WINDOWS = (2, 4, 8, 16)
CONV_WIDTH = 31
LN_EPS = 1e-5
DEEPNORM_ALPHA = 2.0 ** 0.25

SUBLANES = 8
LANES = 128
SEQ_TILE = 512
TILES_PER_STEP = 1
HALO = 32
ROW_STRIDE = 4
BLOCK_ROWS = SUBLANES * ROW_STRIDE
CONV_GROUP = 4
LN_ROWS = 16
MOD_COLS = 768
VMEM_LIMIT_BYTES = 48 * 1024 * 1024

_VEC_NAMES = ("b_in", "b_pool", "ls", "b_dw", "lncg", "lncb", "b_pw", "b_out", "lnpg", "lnpb")


def _rows(p8, n):
    return p8 if n == SUBLANES else jnp.concatenate([p8] * (n // SUBLANES), axis=0)


def _silu(z):
    return z * jax.nn.sigmoid(z)


def _layer_norm(t, eps=LN_EPS):
    mu = jnp.mean(t, axis=-1, keepdims=True)
    tc = t - mu
    var = jnp.mean(tc * tc, axis=-1, keepdims=True)
    return tc * lax.rsqrt(var + eps)


def _strided_rows(ref, slab, start):
    return ref[slab, pl.ds(start, SUBLANES, stride=ROW_STRIDE), :]


def _mod_kernel(vec_at, glu_at, c_ref, w_ref, b_ref, w_dw_ref, w_in_ref, *refs):
    vec_refs, (o_ref, vec8_ref, w_dw8_ref, w_in_bf_ref) = refs[:-4], refs[-4:]
    bsz = c_ref.shape[0]
    glu_a, glu_g, n_glu = glu_at
    w_in_bf_ref[:, 0:glu_a] = w_in_ref[:, 0:glu_a].astype(BF16)
    for cb in range(n_glu):
        for part, src in enumerate((glu_a, glu_g)):
            dst = glu_a + (2 * cb + part) * LANES
            w_in_bf_ref[:, dst:dst + LANES] = w_in_ref[:, src + cb * LANES:src + (cb + 1) * LANES].astype(BF16)
    tail = glu_a + 2 * n_glu * LANES
    w_in_bf_ref[:, tail:] = w_in_ref[:, tail:].astype(BF16)
    c = jnp.concatenate([jnp.broadcast_to(c_ref[b:b + 1, :], (SUBLANES, c_ref.shape[1]))
                         for b in range(bsz)], axis=0)
    s = _silu(c).astype(BF16)
    o_ref[...] = jnp.dot(s, w_ref[...].astype(BF16), preferred_element_type=F32) + b_ref[...]

    @pl.when(pl.program_id(0) == 0)
    def _():
        for name, ref in zip(_VEC_NAMES, vec_refs):
            off, size = vec_at[name]
            vec8_ref[:, off:off + size] = jnp.broadcast_to(ref[...], (SUBLANES, size))
        w_dw8_ref[...] = jnp.broadcast_to(w_dw_ref[...], w_dw8_ref.shape).astype(w_dw8_ref.dtype)


def _layer_kernel(vec_at, x_ref, mod_ref, vec_ref, w_in_ref, w_pool_ref, invc_ref, w_dw_ref, w_pw_ref,
                  w_out_ref, out_ref, u_s, pn_s, v_s, cn_s, zg_s, yo_s, h_s, c_s, cat_s, w_pool_s):
    ts = SEQ_TILE
    d = x_ref.shape[-1]
    n_slab = u_s.shape[0]
    dp = n_slab * LANES
    glu_a, glu_g, z_b = 2 * dp, 3 * dp, 4 * dp

    def vec(name, lo=0, hi=None):
        off, size = vec_at[name]
        return vec_ref[:, off + lo:off + (size if hi is None else hi)]

    @pl.when(jnp.logical_and(pl.program_id(0) == 0, pl.program_id(1) == 0))
    def _():
        u_s[:, ts:ts + HALO, :] = jnp.zeros((n_slab, HALO, LANES), F32)
        v_s[:, ts:ts + HALO, :] = jnp.zeros((n_slab, HALO, LANES), F32)
        w_pool_s[...] = jnp.zeros(w_pool_s.shape, BF16)
        for grp in range(n_slab):
            k, lo = grp // 2, (grp % 2) * LANES
            w_pool_s[k, lo:lo + LANES, lo:lo + LANES] = w_pool_ref[grp].astype(BF16)

    shift = mod_ref[0, :, 0:d]
    scale1 = 1.0 + mod_ref[0, :, d:2 * d]
    gate = mod_ref[0, :, 2 * d:3 * d] * (1.0 / DEEPNORM_ALPHA)
    eps_post = LN_EPS / (DEEPNORM_ALPHA * DEEPNORM_ALPHA)
    never = pl.program_id(0) < 0

    def tile(j, carry):
        r0 = pl.multiple_of(j * ts, ts)
        first = jnp.logical_and(pl.program_id(1) == 0, j == 0)

        def x_rows(r, n):
            return x_ref[0, pl.ds(r0 + r, n), :]

        u_s[:, 0:HALO, :] = jnp.where(first, 0.0, u_s[:, ts:ts + HALO, :])
        v_s[:, 0:HALO, :] = jnp.where(first, 0.0, v_s[:, ts:ts + HALO, :])

        for r in range(0, ts, LN_ROWS):
            h = _layer_norm(x_rows(r, LN_ROWS)) * _rows(scale1, LN_ROWS) + _rows(shift, LN_ROWS)
            h_s[r:r + LN_ROWS, :] = h.astype(BF16)

        h = h_s[...]

        def project(lo, hi):
            return (jnp.dot(h, w_in_ref[:, lo:hi], preferred_element_type=F32)
                    + _rows(vec("b_in", lo, hi), ts))

        def glu(cb):
            bias = jnp.concatenate([vec("b_in", glu_a + cb * LANES, glu_a + (cb + 1) * LANES),
                                    vec("b_in", glu_g + cb * LANES, glu_g + (cb + 1) * LANES)], axis=1)
            ag = (jnp.dot(h, w_in_ref[:, glu_a + 2 * cb * LANES:glu_a + 2 * (cb + 1) * LANES],
                          preferred_element_type=F32)
                  + _rows(bias, ts))
            v_s[cb, HALO:HALO + ts, :] = ag[:, 0:LANES] * jax.nn.sigmoid(ag[:, LANES:])

        def conv(cb, after=None):
            lanes = slice(cb * LANES, (cb + 1) * LANES)
            taps = [w_dw_ref[k, :, lanes] for k in range(CONV_WIDTH)]
            bias = _rows(vec("b_dw", cb * LANES, (cb + 1) * LANES), 2 * SUBLANES)
            for t0 in range(0, ts, 2 * BLOCK_ROWS):
                acc = [bias] * ROW_STRIDE
                part = [None] * ROW_STRIDE
                for qi in range(1 - CONV_WIDTH, ROW_STRIDE):
                    va = _strided_rows(v_s, cb, HALO + t0 + qi)
                    vb = _strided_rows(v_s, cb, HALO + t0 + BLOCK_ROWS + qi)
                    if after is not None and qi == 1 - CONV_WIDTH:
                        va = jnp.where(never, after, va)
                    vin = jnp.concatenate([va, vb], axis=0).astype(BF16)
                    for q in range(ROW_STRIDE):
                        delay = q - qi
                        if 0 <= delay < CONV_WIDTH:
                            term = vin * taps[CONV_WIDTH - 1 - delay]
                            part[q] = term if part[q] is None else part[q] + term
                            if delay % CONV_GROUP == 0:
                                acc[q] = acc[q] + part[q].astype(F32)
                                part[q] = None
                for q in range(ROW_STRIDE):
                    cn_s[cb, pl.ds(t0 + q, SUBLANES, stride=ROW_STRIDE), :] = acc[q][0:SUBLANES]
                    cn_s[cb, pl.ds(t0 + BLOCK_ROWS + q, SUBLANES, stride=ROW_STRIDE), :] = acc[q][SUBLANES:]

        for cb in range(n_slab):
            glu(cb)
        conv(0)
        u = project(0, dp)
        for g in range(n_slab):
            u_s[g, HALO:HALO + ts, :] = u[:, g * LANES:(g + 1) * LANES]
        conv(1, after=u[-SUBLANES:, -LANES:])
        za = _silu(project(dp, 2 * dp)) * _rows(vec("ls"), ts)
        zg_s[:, 0:dp] = za
        conv(2, after=za[-SUBLANES:, -LANES:])
        zb = _silu(project(z_b, z_b + dp))
        zg_s[:, dp:] = zb
        conv(3, after=zb[-SUBLANES:, -LANES:])

        for t0 in range(0, ts, BLOCK_ROWS):
            for g, w in enumerate(POOL_WINDOWS):
                s = {q: _strided_rows(u_s, g, HALO + t0 + q) for q in range(1 - w, ROW_STRIDE)}
                cur = s
                span = 1
                while span < w:
                    s = {q: s[q] + s[q - span] for q in s if q - span in s}
                    span *= 2
                for q in range(ROW_STRIDE):
                    inv = jnp.where(first, invc_ref[g, q], 1.0 / w) if t0 == 0 else 1.0 / w
                    pn_s[g, pl.ds(t0 + q, SUBLANES, stride=ROW_STRIDE), :] = s[q] * inv - cur[q]

        p = jnp.concatenate([pn_s[g] for g in range(n_slab)], axis=1).astype(BF16)
        half = dp // 2
        for k in range(2):
            cols = slice(k * half, (k + 1) * half)
            ya = (jnp.dot(p[:, cols], w_pool_s[k], preferred_element_type=F32)
                  + _rows(vec("b_pool", k * half, (k + 1) * half), ts)) * zg_s[:, cols]
            cat_s[:, cols] = ya.astype(BF16)

        for r in range(0, ts, LN_ROWS):
            cv = jnp.concatenate([cn_s[cb, r:r + LN_ROWS, :] for cb in range(n_slab)], axis=1)
            vn = _layer_norm(cv) * _rows(vec("lncg"), LN_ROWS) + _rows(vec("lncb"), LN_ROWS)
            c_s[r:r + LN_ROWS, :] = _silu(vn).astype(BF16)

        yb = (jnp.dot(c_s[...], w_pw_ref[...], preferred_element_type=F32)
              + _rows(vec("b_pw"), ts)) * zg_s[:, dp:]
        cat_s[:, dp:] = yb.astype(BF16)

        y = (jnp.dot(cat_s[...], w_out_ref[...], preferred_element_type=F32)
             + _rows(vec("b_out"), ts))
        yo_s[...] = x_rows(0, ts) + _rows(gate, ts) * y
        for r in range(0, ts, LN_ROWS):
            t = yo_s[r:r + LN_ROWS, :]
            o = _layer_norm(t, eps_post) * _rows(vec("lnpg"), LN_ROWS) + _rows(vec("lnpb"), LN_ROWS)
            out_ref[0, pl.ds(r0 + r, LN_ROWS), :] = o
        return carry

    lax.fori_loop(0, TILES_PER_STEP, tile, 0)


def _inv_count_table():
    q = np.arange(ROW_STRIDE, dtype=np.float64)[:, None]
    s = np.arange(SUBLANES, dtype=np.float64)[None, :]
    t1 = q + ROW_STRIDE * s + 1.0
    tab = np.stack([1.0 / np.minimum(t1, float(w)) for w in POOL_WINDOWS])
    return jnp.asarray(np.broadcast_to(tab[..., None], tab.shape + (LANES,)), dtype=F32)


def _const_spec(shape):
    return pl.BlockSpec(shape, lambda b, i: (0,) * len(shape))


@jax.jit
def kernel(x, c, w_ada, b_ada, w_in, b_in, w_pool, b_pool, ls_pool, w_dw, b_dw, ln_conv_g, ln_conv_b,
           w_pw, b_pw, w_out, b_out, ln_post_g, ln_post_b):
    bsz, seq, d = x.shape
    n_grp, gw = w_pool.shape[0], w_pool.shape[1]
    dp = n_grp * gw
    dc = w_pw.shape[0]
    d_in = w_in.shape[1]
    step_rows = SEQ_TILE * TILES_PER_STEP
    assert seq % step_rows == 0 and SEQ_TILE % BLOCK_ROWS == 0
    assert gw == LANES and n_grp == len(POOL_WINDOWS) and dp == dc
    assert d_in == 2 * dp + 3 * dc and w_dw.shape[0] == CONV_WIDTH
    assert HALO >= CONV_WIDTH - 1 and HALO >= max(POOL_WINDOWS) - 1
    n_slab = dp // LANES

    vectors = (b_in, b_pool.reshape(-1), ls_pool, b_dw, ln_conv_g, ln_conv_b, b_pw, b_out,
               ln_post_g, ln_post_b)
    vec_at, off = {}, 0
    for name, v in zip(_VEC_NAMES, vectors):
        assert v.shape[0] % LANES == 0
        vec_at[name] = (off, v.shape[0])
        off += v.shape[0]

    def whole(shape):
        return pl.BlockSpec(shape, lambda j: (0,) * len(shape))

    mod_steps = 3 * d // MOD_COLS
    assert d % mod_steps == 0 and (d // mod_steps) % (2 * SUBLANES) == 0
    w_in_rows = d // mod_steps
    mod, vec8, w_dw8, w_in_bf = pl.pallas_call(
        functools.partial(_mod_kernel, vec_at, (2 * dp, 2 * dp + dc, dc // LANES)),
        grid=(mod_steps,),
        in_specs=[whole((bsz, d)),
                  pl.BlockSpec((d, MOD_COLS), lambda j: (0, j)),
                  pl.BlockSpec((1, MOD_COLS), lambda j: (0, j)),
                  whole(w_dw.shape),
                  pl.BlockSpec((w_in_rows, d_in), lambda j: (j, 0))]
                 + [whole((1, v.shape[0])) for v in vectors],
        out_specs=[pl.BlockSpec((bsz * SUBLANES, MOD_COLS), lambda j: (0, j)),
                   whole((SUBLANES, off)), whole((CONV_WIDTH, 2 * SUBLANES, dc)),
                   pl.BlockSpec((w_in_rows, d_in), lambda j: (j, 0))],
        out_shape=[jax.ShapeDtypeStruct((bsz * SUBLANES, 3 * d), F32),
                   jax.ShapeDtypeStruct((SUBLANES, off), F32),
                   jax.ShapeDtypeStruct((CONV_WIDTH, 2 * SUBLANES, dc), BF16),
                   jax.ShapeDtypeStruct((d, d_in), BF16)],
        compiler_params=pltpu.CompilerParams(dimension_semantics=("arbitrary",)),
        name="adaln_mod",
    )(c, w_ada, b_ada[None, :], w_dw, w_in, *(v.astype(F32).reshape(1, -1) for v in vectors))
    mod = mod.reshape(bsz, SUBLANES, 3 * d)

    operands = (x, mod, vec8, w_in_bf, w_pool, _inv_count_table(), w_dw8,
                w_pw.astype(BF16), w_out.astype(BF16))
    in_specs = [
        pl.BlockSpec((1, step_rows, d), lambda b, i: (b, i, 0)),
        pl.BlockSpec((1, SUBLANES, 3 * d), lambda b, i: (b, 0, 0)),
    ] + [_const_spec(op.shape) for op in operands[2:]]

    return pl.pallas_call(
        functools.partial(_layer_kernel, vec_at),
        grid=(bsz, seq // step_rows),
        in_specs=in_specs,
        out_specs=pl.BlockSpec((1, step_rows, d), lambda b, i: (b, i, 0)),
        out_shape=jax.ShapeDtypeStruct((bsz, seq, d), x.dtype),
        scratch_shapes=[
            pltpu.VMEM((n_slab, HALO + SEQ_TILE, LANES), F32),
            pltpu.VMEM((n_slab, SEQ_TILE, LANES), F32),
            pltpu.VMEM((n_slab, HALO + SEQ_TILE, LANES), F32),
            pltpu.VMEM((n_slab, SEQ_TILE, LANES), F32),
            pltpu.VMEM((SEQ_TILE, dp + dc), F32),
            pltpu.VMEM((SEQ_TILE, d), F32),
            pltpu.VMEM((SEQ_TILE, d), BF16),
            pltpu.VMEM((SEQ_TILE, dc), BF16),
            pltpu.VMEM((SEQ_TILE, dp + dc), BF16),
            pltpu.VMEM((n_grp // 2, 2 * gw, 2 * gw), BF16),
        ],
        compiler_params=pltpu.CompilerParams(
            dimension_semantics=("arbitrary", "arbitrary"),
            vmem_limit_bytes=VMEM_LIMIT_BYTES,
        ),
        name="hybrid_layer",
    )(*operands)
```

```python
import functools

import numpy as np
import jax
import jax.numpy as jnp
from jax import lax
from jax.experimental import pallas as pl
from jax.experimental.pallas import tpu as pltpu

F32 = jnp.float32
BF16 = jnp.bfloat16

POOL_WINDOWS = (2, 4, 8, 16)
CONV_WIDTH = 31
LN_EPS = 1e-5
DEEPNORM_ALPHA = 2.0 ** 0.25

SUBLANES = 8
LANES = 128
SEQ_TILE = 512
TILES_PER_STEP = 1
HALO = 32
ROW_STRIDE = 4
BLOCK_ROWS = SUBLANES * ROW_STRIDE
CONV_GROUP = 4
LN_ROWS = 32
MOD_COLS = 768
VMEM_LIMIT_BYTES = 48 * 1024 * 1024

_VEC_NAMES = ("b_in", "b_pool", "ls", "b_dw", "lncg", "lncb", "b_pw", "b_out", "lnpg", "lnpb")


def _rows(p8, n):
    return p8 if n == SUBLANES else jnp.concatenate([p8] * (n // SUBLANES), axis=0)


def _silu(z):
    return z * jax.nn.sigmoid(z)


def _layer_norm(t, eps=LN_EPS):
    mu = jnp.mean(t, axis=-1, keepdims=True)
    tc = t - mu
    var = jnp.mean(tc * tc, axis=-1, keepdims=True)
    return tc * lax.rsqrt(var + eps)


def _strided_rows(ref, slab, start):
    return ref[slab, pl.ds(start, SUBLANES, stride=ROW_STRIDE), :]


def _mod_kernel(vec_at, glu_at, c_ref, w_ref, b_ref, w_dw_ref, w_in_ref, *refs):
    vec_refs, (o_ref, vec8_ref, w_dw8_ref, w_in_bf_ref) = refs[:-4], refs[-4:]
    bsz = c_ref.shape[0]
    glu_a, glu_g, n_glu = glu_at
    w_in_bf_ref[:, 0:glu_a] = w_in_ref[:, 0:glu_a].astype(BF16)
    for cb in range(n_glu):
        for part, src in enumerate((glu_a, glu_g)):
            dst = glu_a + (2 * cb + part) * LANES
            w_in_bf_ref[:, dst:dst + LANES] = w_in_ref[:, src + cb * LANES:src + (cb + 1) * LANES].astype(BF16)
    tail = glu_a + 2 * n_glu * LANES
    w_in_bf_ref[:, tail:] = w_in_ref[:, tail:].astype(BF16)
    c = jnp.concatenate([jnp.broadcast_to(c_ref[b:b + 1, :], (SUBLANES, c_ref.shape[1]))
                         for b in range(bsz)], axis=0)
    s = _silu(c).astype(BF16)
    o_ref[...] = jnp.dot(s, w_ref[...].astype(BF16), preferred_element_type=F32) + b_ref[...]

    @pl.when(pl.program_id(0) == 0)
    def _():
        for name, ref in zip(_VEC_NAMES, vec_refs):
            off, size = vec_at[name]
            vec8_ref[:, off:off + size] = jnp.broadcast_to(ref[...], (SUBLANES, size))
        w_dw8_ref[...] = jnp.broadcast_to(w_dw_ref[...], w_dw8_ref.shape).astype(w_dw8_ref.dtype)


def _layer_kernel(vec_at, x_ref, mod_ref, vec_ref, w_in_ref, w_pool_ref, invc_ref, w_dw_ref, w_pw_ref,
                  w_out_ref, out_ref, u_s, pn_s, v_s, cn_s, zg_s, yo_s, h_s, c_s, cat_s, w_pool_s):
    ts = SEQ_TILE
    d = x_ref.shape[-1]
    n_slab = u_s.shape[0]
    dp = n_slab * LANES
    glu_a, glu_g, z_b = 2 * dp, 3 * dp, 4 * dp

    def vec(name, lo=0, hi=None):
        off, size = vec_at[name]
        return vec_ref[:, off + lo:off + (size if hi is None else hi)]

    @pl.when(jnp.logical_and(pl.program_id(0) == 0, pl.program_id(1) == 0))
    def _():
        u_s[:, ts:ts + HALO, :] = jnp.zeros((n_slab, HALO, LANES), F32)
        v_s[:, ts:ts + HALO, :] = jnp.zeros((n_slab, HALO, LANES), F32)
        w_pool_s[...] = jnp.zeros(w_pool_s.shape, BF16)
        for grp in range(n_slab):
            k, lo = grp // 2, (grp % 2) * LANES
            w_pool_s[k, lo:lo + LANES, lo:lo + LANES] = w_pool_ref[grp].astype(BF16)

    shift = mod_ref[0, :, 0:d]
    scale1 = 1.0 + mod_ref[0, :, d:2 * d]
    gate = mod_ref[0, :, 2 * d:3 * d] * (1.0 / DEEPNORM_ALPHA)
    eps_post = LN_EPS / (DEEPNORM_ALPHA * DEEPNORM_ALPHA)
    never = pl.program_id(0) < 0

    def tile(j, carry):
        r0 = pl.multiple_of(j * ts, ts)
        first = jnp.logical_and(pl.program_id(1) == 0, j == 0)

        def x_rows(r, n):
            return x_ref[0, pl.ds(r0 + r, n), :]

        u_s[:, 0:HALO, :] = jnp.where(first, 0.0, u_s[:, ts:ts + HALO, :])
        v_s[:, 0:HALO, :] = jnp.where(first, 0.0, v_s[:, ts:ts + HALO, :])

        for r in range(0, ts, LN_ROWS):
            h = _layer_norm(x_rows(r, LN_ROWS)) * _rows(scale1, LN_ROWS) + _rows(shift, LN_ROWS)
            h_s[r:r + LN_ROWS, :] = h.astype(BF16)

        h = h_s[...]

        def project(lo, hi):
            return (jnp.dot(h, w_in_ref[:, lo:hi], preferred_element_type=F32)
                    + _rows(vec("b_in", lo, hi), ts))

        def glu(cb):
            bias = jnp.concatenate([vec("b_in", glu_a + cb * LANES, glu_a + (cb + 1) * LANES),
                                    vec("b_in", glu_g + cb * LANES, glu_g + (cb + 1) * LANES)], axis=1)
            ag = (jnp.dot(h, w_in_ref[:, glu_a + 2 * cb * LANES:glu_a + 2 * (cb + 1) * LANES],
                          preferred_element_type=F32)
                  + _rows(bias, ts))
            v_s[cb, HALO:HALO + ts, :] = ag[:, 0:LANES] * jax.nn.sigmoid(ag[:, LANES:])

        def conv(cb, after=None):
            lanes = slice(cb * LANES, (cb + 1) * LANES)
            taps = [w_dw_ref[k, :, lanes] for k in range(CONV_WIDTH)]
            bias = _rows(vec("b_dw", cb * LANES, (cb + 1) * LANES), 2 * SUBLANES)
            for t0 in range(0, ts, 2 * BLOCK_ROWS):
                acc = [bias] * ROW_STRIDE
                part = [None] * ROW_STRIDE
                for qi in range(1 - CONV_WIDTH, ROW_STRIDE):
                    va = _strided_rows(v_s, cb, HALO + t0 + qi)
                    vb = _strided_rows(v_s, cb, HALO + t0 + BLOCK_ROWS + qi)
                    if after is not None and qi == 1 - CONV_WIDTH:
                        va = jnp.where(never, after, va)
                    vin = jnp.concatenate([va, vb], axis=0).astype(BF16)
                    for q in range(ROW_STRIDE):
                        delay = q - qi
                        if 0 <= delay < CONV_WIDTH:
                            term = vin * taps[CONV_WIDTH - 1 - delay]
                            part[q] = term if part[q] is None else part[q] + term
                            if delay % CONV_GROUP == 0:
                                acc[q] = acc[q] + part[q].astype(F32)
                                part[q] = None
                for q in range(ROW_STRIDE):
                    cn_s[cb, pl.ds(t0 + q, SUBLANES, stride=ROW_STRIDE), :] = acc[q][0:SUBLANES]
                    cn_s[cb, pl.ds(t0 + BLOCK_ROWS + q, SUBLANES, stride=ROW_STRIDE), :] = acc[q][SUBLANES:]

        for cb in range(n_slab):
            glu(cb)
        conv(0)
        u = project(0, dp)
        for g in range(n_slab):
            u_s[g, HALO:HALO + ts, :] = u[:, g * LANES:(g + 1) * LANES]
        conv(1, after=u[-SUBLANES:, -LANES:])
        za = _silu(project(dp, 2 * dp)) * _rows(vec("ls"), ts)
        zg_s[:, 0:dp] = za
        conv(2, after=za[-SUBLANES:, -LANES:])
        zb = _silu(project(z_b, z_b + dp))
        zg_s[:, dp:] = zb
        conv(3, after=zb[-SUBLANES:, -LANES:])

        for t0 in range(0, ts, BLOCK_ROWS):
            for g, w in enumerate(POOL_WINDOWS):
                s = {q: _strided_rows(u_s, g, HALO + t0 + q) for q in range(1 - w, ROW_STRIDE)}
                cur = s
                span = 1
                while span < w:
                    s = {q: s[q] + s[q - span] for q in s if q - span in s}
                    span *= 2
                for q in range(ROW_STRIDE):
                    inv = jnp.where(first, invc_ref[g, q], 1.0 / w) if t0 == 0 else 1.0 / w
                    pn_s[g, pl.ds(t0 + q, SUBLANES, stride=ROW_STRIDE), :] = s[q] * inv - cur[q]

        p = jnp.concatenate([pn_s[g] for g in range(n_slab)], axis=1).astype(BF16)
        half = dp // 2
        for k in range(2):
            cols = slice(k * half, (k + 1) * half)
            ya = (jnp.dot(p[:, cols], w_pool_s[k], preferred_element_type=F32)
                  + _rows(vec("b_pool", k * half, (k + 1) * half), ts)) * zg_s[:, cols]
            cat_s[:, cols] = ya.astype(BF16)

        for r in range(0, ts, LN_ROWS):
            cv = jnp.concatenate([cn_s[cb, r:r + LN_ROWS, :] for cb in range(n_slab)], axis=1)
            vn = _layer_norm(cv) * _rows(vec("lncg"), LN_ROWS) + _rows(vec("lncb"), LN_ROWS)
            c_s[r:r + LN_ROWS, :] = _silu(vn).astype(BF16)

        yb = (jnp.dot(c_s[...], w_pw_ref[...], preferred_element_type=F32)
              + _rows(vec("b_pw"), ts)) * zg_s[:, dp:]
        cat_s[:, dp:] = yb.astype(BF16)

        y = (jnp.dot(cat_s[...], w_out_ref[...], preferred_element_type=F32)
             + _rows(vec("b_out"), ts))
        yo_s[...] = x_rows(0, ts) + _rows(gate, ts) * y
        for r in range(0, ts, LN_ROWS):
            t = yo_s[r:r + LN_ROWS, :]
            o = _layer_norm(t, eps_post) * _rows(vec("lnpg"), LN_ROWS) + _rows(vec("lnpb"), LN_ROWS)
            out_ref[0, pl.ds(r0 + r, LN_ROWS), :] = o
        return carry

    lax.fori_loop(0, TILES_PER_STEP, tile, 0)


def _inv_count_table():
    q = np.arange(ROW_STRIDE, dtype=np.float64)[:, None]
    s = np.arange(SUBLANES, dtype=np.float64)[None, :]
    t1 = q + ROW_STRIDE * s + 1.0
    tab = np.stack([1.0 / np.minimum(t1, float(w)) for w in POOL_WINDOWS])
    return jnp.asarray(np.broadcast_to(tab[..., None], tab.shape + (LANES,)), dtype=F32)


def _const_spec(shape):
    return pl.BlockSpec(shape, lambda b, i: (0,) * len(shape))


@jax.jit
def kernel(x, c, w_ada, b_ada, w_in, b_in, w_pool, b_pool, ls_pool, w_dw, b_dw, ln_conv_g, ln_conv_b,
           w_pw, b_pw, w_out, b_out, ln_post_g, ln_post_b):
    bsz, seq, d = x.shape
    n_grp, gw = w_pool.shape[0], w_pool.shape[1]
    dp = n_grp * gw
    dc = w_pw.shape[0]
    d_in = w_in.shape[1]
    step_rows = SEQ_TILE * TILES_PER_STEP
    assert seq % step_rows == 0 and SEQ_TILE % BLOCK_ROWS == 0
    assert gw == LANES and n_grp == len(POOL_WINDOWS) and dp == dc
    assert d_in == 2 * dp + 3 * dc and w_dw.shape[0] == CONV_WIDTH
    assert HALO >= CONV_WIDTH - 1 and HALO >= max(POOL_WINDOWS) - 1
    n_slab = dp // LANES

    vectors = (b_in, b_pool.reshape(-1), ls_pool, b_dw, ln_conv_g, ln_conv_b, b_pw, b_out,
               ln_post_g, ln_post_b)
    vec_at, off = {}, 0
    for name, v in zip(_VEC_NAMES, vectors):
        assert v.shape[0] % LANES == 0
        vec_at[name] = (off, v.shape[0])
        off += v.shape[0]

    def whole(shape):
        return pl.BlockSpec(shape, lambda j: (0,) * len(shape))

    mod_steps = 3 * d // MOD_COLS
    assert d % mod_steps == 0 and (d // mod_steps) % (2 * SUBLANES) == 0
    w_in_rows = d // mod_steps
    mod, vec8, w_dw8, w_in_bf = pl.pallas_call(
        functools.partial(_mod_kernel, vec_at, (2 * dp, 2 * dp + dc, dc // LANES)),
        grid=(mod_steps,),
        in_specs=[whole((bsz, d)),
                  pl.BlockSpec((d, MOD_COLS), lambda j: (0, j)),
                  pl.BlockSpec((1, MOD_COLS), lambda j: (0, j)),
                  whole(w_dw.shape),
                  pl.BlockSpec((w_in_rows, d_in), lambda j: (j, 0))]
                 + [whole((1, v.shape[0])) for v in vectors],
        out_specs=[pl.BlockSpec((bsz * SUBLANES, MOD_COLS), lambda j: (0, j)),
                   whole((SUBLANES, off)), whole((CONV_WIDTH, 2 * SUBLANES, dc)),
                   pl.BlockSpec((w_in_rows, d_in), lambda j: (j, 0))],
        out_shape=[jax.ShapeDtypeStruct((bsz * SUBLANES, 3 * d), F32),
                   jax.ShapeDtypeStruct((SUBLANES, off), F32),
                   jax.ShapeDtypeStruct((CONV_WIDTH, 2 * SUBLANES, dc), BF16),
                   jax.ShapeDtypeStruct((d, d_in), BF16)],
        compiler_params=pltpu.CompilerParams(dimension_semantics=("arbitrary",)),
        name="adaln_mod",
    )(c, w_ada, b_ada[None, :], w_dw, w_in, *(v.astype(F32).reshape(1, -1) for v in vectors))
    mod = mod.reshape(bsz, SUBLANES, 3 * d)

    operands = (x, mod, vec8, w_in_bf, w_pool, _inv_count_table(), w_dw8,
                w_pw.astype(BF16), w_out.astype(BF16))
    in_specs = [
        pl.BlockSpec((1, step_rows, d), lambda b, i: (b, i, 0)),
        pl.BlockSpec((1, SUBLANES, 3 * d), lambda b, i: (b, 0, 0)),
    ] + [_const_spec(op.shape) for op in operands[2:]]

    return pl.pallas_call(
        functools.partial(_layer_kernel, vec_at),
        grid=(bsz, seq // step_rows),
        in_specs=in_specs,
        out_specs=pl.BlockSpec((1, step_rows, d), lambda b, i: (b, i, 0)),
        out_shape=jax.ShapeDtypeStruct((bsz, seq, d), x.dtype),
        scratch_shapes=[
            pltpu.VMEM((n_slab, HALO + SEQ_TILE, LANES), F32),
            pltpu.VMEM((n_slab, SEQ_TILE, LANES), F32),
            pltpu.VMEM((n_slab, HALO + SEQ_TILE, LANES), F32),
            pltpu.VMEM((n_slab, SEQ_TILE, LANES), F32),
            pltpu.VMEM((SEQ_TILE, dp + dc), F32),
            pltpu.VMEM((SEQ_TILE, d), F32),
            pltpu.VMEM((SEQ_TILE, d), BF16),
            pltpu.VMEM((SEQ_TILE, dc), BF16),
            pltpu.VMEM((SEQ_TILE, dp + dc), BF16),
            pltpu.VMEM((n_grp // 2, 2 * gw, 2 * gw), BF16),
        ],
        compiler_params=pltpu.CompilerParams(
            dimension_semantics=("arbitrary", "arbitrary"),
            vmem_limit_bytes=VMEM_LIMIT_BYTES,
        ),
        name="hybrid_layer",
    )(*operands)
```

```python
import functools

import numpy as np
import jax
import jax.numpy as jnp
from jax import lax
from jax.experimental import pallas as pl
from jax.experimental.pallas import tpu as pltpu

F32 = jnp.float32
BF16 = jnp.bfloat16

POOL_WINDOWS = (2, 4, 8, 16)
CONV_WIDTH = 31
LN_EPS = 1e-5
DEEPNORM_ALPHA = 2.0 ** 0.25

SUBLANES = 8
LANES = 128
SEQ_TILE = 512
TILES_PER_STEP = 1
HALO = 32
ROW_STRIDE = 4
BLOCK_ROWS = SUBLANES * ROW_STRIDE
CONV_GROUP = 4
LN_ROWS = 16
MOD_COLS = 768
VMEM_LIMIT_BYTES = 48 * 1024 * 1024

_VEC_NAMES = ("b_in", "b_pool", "ls", "b_dw", "lncg", "lncb", "b_pw", "b_out", "lnpg", "lnpb")


def _rows(p8, n):
    return p8 if n == SUBLANES else jnp.concatenate([p8] * (n // SUBLANES), axis=0)


def _silu(z):
    return z * jax.nn.sigmoid(z)


def _layer_norm(t, eps=LN_EPS):
    mu = jnp.mean(t, axis=-1, keepdims=True)
    tc = t - mu
    var = jnp.mean(tc * tc, axis=-1, keepdims=True)
    return tc * lax.rsqrt(var + eps)


def _strided_rows(ref, slab, start):
    return ref[slab, pl.ds(start, SUBLANES, stride=ROW_STRIDE), :]


def _mod_kernel(vec_at, glu_at, c_ref, w_ref, b_ref, w_dw_ref, w_in_ref, *refs):
    vec_refs, (o_ref, vec8_ref, w_dw8_ref, w_in_bf_ref) = refs[:-4], refs[-4:]
    bsz = c_ref.shape[0]
    glu_a, glu_g, n_glu = glu_at
    w_in_bf_ref[:, 0:glu_a] = w_in_ref[:, 0:glu_a].astype(BF16)
    for cb in range(n_glu):
        for part, src in enumerate((glu_a, glu_g)):
            dst = glu_a + (2 * cb + part) * LANES
            w_in_bf_ref[:, dst:dst + LANES] = w_in_ref[:, src + cb * LANES:src + (cb + 1) * LANES].astype(BF16)
    tail = glu_a + 2 * n_glu * LANES
    w_in_bf_ref[:, tail:] = w_in_ref[:, tail:].astype(BF16)
    c = jnp.concatenate([jnp.broadcast_to(c_ref[b:b + 1, :], (SUBLANES, c_ref.shape[1]))
                         for b in range(bsz)], axis=0)
    s = _silu(c).astype(BF16)
    o_ref[...] = jnp.dot(s, w_ref[...].astype(BF16), preferred_element_type=F32) + b_ref[...]

    @pl.when(pl.program_id(0) == 0)
    def _():
        for name, ref in zip(_VEC_NAMES, vec_refs):
            off, size = vec_at[name]
            vec8_ref[:, off:off + size] = jnp.broadcast_to(ref[...], (SUBLANES, size))
        w_dw8_ref[...] = jnp.broadcast_to(w_dw_ref[...], w_dw8_ref.shape).astype(w_dw8_ref.dtype)


def _layer_kernel(vec_at, x_ref, mod_ref, vec_ref, w_in_ref, w_pool_ref, invc_ref, w_dw_ref, w_pw_ref,
                  w_out_ref, out_ref, u_s, pn_s, v_s, cn_s, zg_s, yo_s, h_s, c_s, cat_s, w_pool_s):
    ts = SEQ_TILE
    d = x_ref.shape[-1]
    n_slab = u_s.shape[0]
    dp = n_slab * LANES
    glu_a, glu_g, z_b = 2 * dp, 3 * dp, 4 * dp

    def vec(name, lo=0, hi=None):
        off, size = vec_at[name]
        return vec_ref[:, off + lo:off + (size if hi is None else hi)]

    @pl.when(jnp.logical_and(pl.program_id(0) == 0, pl.program_id(1) == 0))
    def _():
        u_s[:, ts:ts + HALO, :] = jnp.zeros((n_slab, HALO, LANES), F32)
        v_s[:, ts:ts + HALO, :] = jnp.zeros((n_slab, HALO, LANES), F32)
        w_pool_s[...] = jnp.zeros(w_pool_s.shape, BF16)
        for grp in range(n_slab):
            k, lo = grp // 2, (grp % 2) * LANES
            w_pool_s[k, lo:lo + LANES, lo:lo + LANES] = w_pool_ref[grp].astype(BF16)

    shift = mod_ref[0, :, 0:d]
    scale1 = 1.0 + mod_ref[0, :, d:2 * d]
    gate = mod_ref[0, :, 2 * d:3 * d] * (1.0 / DEEPNORM_ALPHA)
    eps_post = LN_EPS / (DEEPNORM_ALPHA * DEEPNORM_ALPHA)
    never = pl.program_id(0) < 0

    def tile(j, carry):
        r0 = pl.multiple_of(j * ts, ts)
        first = jnp.logical_and(pl.program_id(1) == 0, j == 0)

        def x_rows(r, n):
            return x_ref[0, pl.ds(r0 + r, n), :]

        u_s[:, 0:HALO, :] = jnp.where(first, 0.0, u_s[:, ts:ts + HALO, :])
        v_s[:, 0:HALO, :] = jnp.where(first, 0.0, v_s[:, ts:ts + HALO, :])

        for r in range(0, ts, LN_ROWS):
            h = _layer_norm(x_rows(r, LN_ROWS)) * _rows(scale1, LN_ROWS) + _rows(shift, LN_ROWS)
            h_s[r:r + LN_ROWS, :] = h.astype(BF16)

        h = h_s[...]

        def project(lo, hi):
            return (jnp.dot(h, w_in_ref[:, lo:hi], preferred_element_type=F32)
                    + _rows(vec("b_in", lo, hi), ts))

        def glu(cb):
            bias = jnp.concatenate([vec("b_in", glu_a + cb * LANES, glu_a + (cb + 1) * LANES),
                                    vec("b_in", glu_g + cb * LANES, glu_g + (cb + 1) * LANES)], axis=1)
            ag = (jnp.dot(h, w_in_ref[:, glu_a + 2 * cb * LANES:glu_a + 2 * (cb + 1) * LANES],
                          preferred_element_type=F32)
                  + _rows(bias, ts))
            v_s[cb, HALO:HALO + ts, :] = ag[:, 0:LANES] * jax.nn.sigmoid(ag[:, LANES:])

        def conv(cb, after=None):
            lanes = slice(cb * LANES, (cb + 1) * LANES)
            taps = [w_dw_ref[k, :, lanes] for k in range(CONV_WIDTH)]
            bias = _rows(vec("b_dw", cb * LANES, (cb + 1) * LANES), 2 * SUBLANES)
            for t0 in range(0, ts, 2 * BLOCK_ROWS):
                acc = [bias] * ROW_STRIDE
                part = [None] * ROW_STRIDE
                for qi in range(1 - CONV_WIDTH, ROW_STRIDE):
                    va = _strided_rows(v_s, cb, HALO + t0 + qi)
                    vb = _strided_rows(v_s, cb, HALO + t0 + BLOCK_ROWS + qi)
                    if after is not None and qi == 1 - CONV_WIDTH:
                        va = jnp.where(never, after, va)
                    vin = jnp.concatenate([va, vb], axis=0).astype(BF16)
                    for q in range(ROW_STRIDE):
                        delay = q - qi
                        if 0 <= delay < CONV_WIDTH:
                            term = vin * taps[CONV_WIDTH - 1 - delay]
                            part[q] = term if part[q] is None else part[q] + term
                            if delay % CONV_GROUP == 0:
                                acc[q] = acc[q] + part[q].astype(F32)
                                part[q] = None
                for q in range(ROW_STRIDE):
                    cn_s[cb, pl.ds(t0 + q, SUBLANES, stride=ROW_STRIDE), :] = acc[q][0:SUBLANES]
                    cn_s[cb, pl.ds(t0 + BLOCK_ROWS + q, SUBLANES, stride=ROW_STRIDE), :] = acc[q][SUBLANES:]

        for cb in range(n_slab):
            glu(cb)
        conv(0)
        u = project(0, dp)
        for g in range(n_slab):
            u_s[g, HALO:HALO + ts, :] = u[:, g * LANES:(g + 1) * LANES]
        conv(1, after=u[-SUBLANES:, -LANES:])
        za = _silu(project(dp, 2 * dp)) * _rows(vec("ls"), ts)
        zg_s[:, 0:dp] = za
        conv(2, after=za[-SUBLANES:, -LANES:])
        zb = _silu(project(z_b, z_b + dp))
        zg_s[:, dp:] = zb
        conv(3, after=zb[-SUBLANES:, -LANES:])

        for t0 in range(0, ts, BLOCK_ROWS):
            for g, w in enumerate(POOL_WINDOWS):
                s = {q: _strided_rows(u_s, g, HALO + t0 + q) for q in range(1 - w, ROW_STRIDE)}
                cur = s
                span = 1
                while span < w:
                    s = {q: s[q] + s[q - span] for q in s if q - span in s}
                    span *= 2
                for q in range(ROW_STRIDE):
                    inv = jnp.where(first, invc_ref[g, q], 1.0 / w) if t0 == 0 else 1.0 / w
                    pn_s[g, pl.ds(t0 + q, SUBLANES, stride=ROW_STRIDE), :] = s[q] * inv - cur[q]

        p = jnp.concatenate([pn_s[g] for g in range(n_slab)], axis=1).astype(BF16)
        half = dp // 2
        for k in range(2):
            cols = slice(k * half, (k + 1) * half)
            ya = (jnp.dot(p[:, cols], w_pool_s[k], preferred_element_type=F32)
                  + _rows(vec("b_pool", k * half, (k + 1) * half), ts)) * zg_s[:, cols]
            cat_s[:, cols] = ya.astype(BF16)

        for r in range(0, ts, LN_ROWS):
            cv = jnp.concatenate([cn_s[cb, r:r + LN_ROWS, :] for cb in range(n_slab)], axis=1)
            vn = _layer_norm(cv) * _rows(vec("lncg"), LN_ROWS) + _rows(vec("lncb"), LN_ROWS)
            c_s[r:r + LN_ROWS, :] = _silu(vn).astype(BF16)

        yb = (jnp.dot(c_s[...], w_pw_ref[...], preferred_element_type=F32)
              + _rows(vec("b_pw"), ts)) * zg_s[:, dp:]
        cat_s[:, dp:] = yb.astype(BF16)

        y = (jnp.dot(cat_s[...], w_out_ref[...], preferred_element_type=F32)
             + _rows(vec("b_out"), ts))
        yo_s[...] = x_rows(0, ts) + _rows(gate, ts) * y
        for r in range(0, ts, LN_ROWS):
            t = yo_s[r:r + LN_ROWS, :]
            o = _layer_norm(t, eps_post) * _rows(vec("lnpg"), LN_ROWS) + _rows(vec("lnpb"), LN_ROWS)
            out_ref[0, pl.ds(r0 + r, LN_ROWS), :] = o
        return carry

    lax.fori_loop(0, TILES_PER_STEP, tile, 0)


def _inv_count_table():
    q = np.arange(ROW_STRIDE, dtype=np.float64)[:, None]
    s = np.arange(SUBLANES, dtype=np.float64)[None, :]
    t1 = q + ROW_STRIDE * s + 1.0
    tab = np.stack([1.0 / np.minimum(t1, float(w)) for w in POOL_WINDOWS])
    return jnp.asarray(np.broadcast_to(tab[..., None], tab.shape + (LANES,)), dtype=F32)


def _const_spec(shape):
    return pl.BlockSpec(shape, lambda b, i: (0,) * len(shape), pipeline_mode=pl.Buffered(1))


@jax.jit
def kernel(x, c, w_ada, b_ada, w_in, b_in, w_pool, b_pool, ls_pool, w_dw, b_dw, ln_conv_g, ln_conv_b,
           w_pw, b_pw, w_out, b_out, ln_post_g, ln_post_b):
    bsz, seq, d = x.shape
    n_grp, gw = w_pool.shape[0], w_pool.shape[1]
    dp = n_grp * gw
    dc = w_pw.shape[0]
    d_in = w_in.shape[1]
    step_rows = SEQ_TILE * TILES_PER_STEP
    assert seq % step_rows == 0 and SEQ_TILE % BLOCK_ROWS == 0
    assert gw == LANES and n_grp == len(POOL_WINDOWS) and dp == dc
    assert d_in == 2 * dp + 3 * dc and w_dw.shape[0] == CONV_WIDTH
    assert HALO >= CONV_WIDTH - 1 and HALO >= max(POOL_WINDOWS) - 1
    n_slab = dp // LANES

    vectors = (b_in, b_pool.reshape(-1), ls_pool, b_dw, ln_conv_g, ln_conv_b, b_pw, b_out,
               ln_post_g, ln_post_b)
    vec_at, off = {}, 0
    for name, v in zip(_VEC_NAMES, vectors):
        assert v.shape[0] % LANES == 0
        vec_at[name] = (off, v.shape[0])
        off += v.shape[0]

    def whole(shape):
        return pl.BlockSpec(shape, lambda j: (0,) * len(shape))

    mod_steps = 3 * d // MOD_COLS
    assert d % mod_steps == 0 and (d // mod_steps) % (2 * SUBLANES) == 0
    w_in_rows = d // mod_steps
    mod, vec8, w_dw8, w_in_bf = pl.pallas_call(
        functools.partial(_mod_kernel, vec_at, (2 * dp, 2 * dp + dc, dc // LANES)),
        grid=(mod_steps,),
        in_specs=[whole((bsz, d)),
                  pl.BlockSpec((d, MOD_COLS), lambda j: (0, j)),
                  pl.BlockSpec((1, MOD_COLS), lambda j: (0, j)),
                  whole(w_dw.shape),
                  pl.BlockSpec((w_in_rows, d_in), lambda j: (j, 0))]
                 + [whole((1, v.shape[0])) for v in vectors],
        out_specs=[pl.BlockSpec((bsz * SUBLANES, MOD_COLS), lambda j: (0, j)),
                   whole((SUBLANES, off)), whole((CONV_WIDTH, 2 * SUBLANES, dc)),
                   pl.BlockSpec((w_in_rows, d_in), lambda j: (j, 0))],
        out_shape=[jax.ShapeDtypeStruct((bsz * SUBLANES, 3 * d), F32),
                   jax.ShapeDtypeStruct((SUBLANES, off), F32),
                   jax.ShapeDtypeStruct((CONV_WIDTH, 2 * SUBLANES, dc), BF16),
                   jax.ShapeDtypeStruct((d, d_in), BF16)],
        compiler_params=pltpu.CompilerParams(dimension_semantics=("arbitrary",)),
        name="adaln_mod",
    )(c, w_ada, b_ada[None, :], w_dw, w_in, *(v.astype(F32).reshape(1, -1) for v in vectors))
    mod = mod.reshape(bsz, SUBLANES, 3 * d)

    operands = (x, mod, vec8, w_in_bf, w_pool, _inv_count_table(), w_dw8,
                w_pw.astype(BF16), w_out.astype(BF16))
    in_specs = [
        pl.BlockSpec((1, step_rows, d), lambda b, i: (b, i, 0)),
        pl.BlockSpec((1, SUBLANES, 3 * d), lambda b, i: (b, 0, 0)),
    ] + [_const_spec(op.shape) for op in operands[2:]]

    return pl.pallas_call(
        functools.partial(_layer_kernel, vec_at),
        grid=(bsz, seq // step_rows),
        in_specs=in_specs,
        out_specs=pl.BlockSpec((1, step_rows, d), lambda b, i: (b, i, 0)),
        out_shape=jax.ShapeDtypeStruct((bsz, seq, d), x.dtype),
        scratch_shapes=[
            pltpu.VMEM((n_slab, HALO + SEQ_TILE, LANES), F32),
            pltpu.VMEM((n_slab, SEQ_TILE, LANES), F32),
            pltpu.VMEM((n_slab, HALO + SEQ_TILE, LANES), F32),
            pltpu.VMEM((n_slab, SEQ_TILE, LANES), F32),
            pltpu.VMEM((SEQ_TILE, dp + dc), F32),
            pltpu.VMEM((SEQ_TILE, d), F32),
            pltpu.VMEM((SEQ_TILE, d), BF16),
            pltpu.VMEM((SEQ_TILE, dc), BF16),
            pltpu.VMEM((SEQ_TILE, dp + dc), BF16),
            pltpu.VMEM((n_grp // 2, 2 * gw, 2 * gw), BF16),
        ],
        compiler_params=pltpu.CompilerParams(
            dimension_semantics=("arbitrary", "arbitrary"),
            vmem_limit_bytes=VMEM_LIMIT_BYTES,
        ),
        name="hybrid_layer",
    )(*operands)
```
